```python
import jax
import jax.numpy as jnp
from jax import lax
import numpy as np

D_MODEL = 2048
BATCH = 4
SEQ = 4096
DEPTH = 1

NSA_HEADS = 16
NSA_GROUPS = 4
NSA_HEADS_PER_GROUP = NSA_HEADS // NSA_GROUPS
NSA_HEAD_DIM = 64
NSA_ROPE_DIM = NSA_HEAD_DIM // 4
CMP_BLOCK = 32
CMP_STRIDE = 16
CMP_HIDDEN = 256
SEL_BLOCK = 64
SEL_TOPK = 16
WINDOW = 512
N_NSA_BRANCHES = 3
MLA_HEADS = 16
MLA_Q_RANK = 512
MLA_KV_RANK = 256
MLA_NOPE_DIM = 64
MLA_ROPE_DIM = 32
MLA_V_DIM = 64
PEER_HEADS = 8
PEER_N_KEYS = 128
PEER_N_EXPERTS = PEER_N_KEYS * PEER_N_KEYS
PEER_QUERY_DIM = 256
PEER_TOPK = 16
PEER_TOKEN_BLOCK = 128
N_BRANCHES = 2
ROPE_THETA = 500000.0
NORM_EPS = 1e-6
Q_BLOCK = 128
SEL_Q_BLOCK = 32
NEG_INF = -1e30
FORCE_BONUS = 1e4

IN_SIZES = (NSA_HEADS * NSA_HEAD_DIM,
            N_NSA_BRANCHES * 2 * NSA_GROUPS * NSA_HEAD_DIM,
            N_NSA_BRANCHES * NSA_HEADS,
            MLA_Q_RANK,
            MLA_KV_RANK,
            MLA_ROPE_DIM,
            N_BRANCHES * D_MODEL)
IN_COLS = sum(IN_SIZES)

kernel_name = 'hybrid_nsa_mla_peer_block'


def rms_norm(x, g):
    xf = x.astype(jnp.float32)
    y = xf * lax.rsqrt(jnp.mean(xf * xf, axis=-1, keepdims=True) + NORM_EPS)
    return (y * g.astype(jnp.float32)).astype(x.dtype)


def masked_softmax(s, mask):
    s = jnp.where(mask, s.astype(jnp.float32), NEG_INF)
    return jax.nn.softmax(s, axis=-1) * mask


def apply_rope(x, positions, rot_dim):
    half = rot_dim // 2
    inv_freq = ROPE_THETA ** (-jnp.arange(half, dtype=jnp.float32) * (2.0 / rot_dim))
    ang = positions.astype(jnp.float32)[:, :, None] * inv_freq
    cos = jnp.cos(ang)[:, :, None, :]
    sin = jnp.sin(ang)[:, :, None, :]
    xr = x[..., :rot_dim].astype(jnp.float32)
    x1, x2 = xr[..., :half], xr[..., half:]
    rot = jnp.concatenate([x1 * cos - x2 * sin, x2 * cos + x1 * sin], axis=-1)
    return jnp.concatenate([rot.astype(x.dtype), x[..., rot_dim:]], axis=-1)


def map_query_blocks(fn, seq_len, block):
    starts = jnp.arange(seq_len // block, dtype=jnp.int32) * block
    out = lax.map(fn, starts)

    def merge(a):
        a = jnp.moveaxis(a, 0, 1)
        return a.reshape((a.shape[0], seq_len) + a.shape[3:])
    return jax.tree_util.tree_map(merge, out)


def compress_blocks(kraw, pe, w1, b1, w2):
    B, T, G, d = kraw.shape
    n_cmp = (T - CMP_BLOCK) // CMP_STRIDE + 1
    idx = jnp.arange(n_cmp)[:, None] * CMP_STRIDE + jnp.arange(CMP_BLOCK)[None, :]
    blk = kraw[:, idx] + pe[:, None, :]
    blk = jnp.transpose(blk, (0, 1, 3, 2, 4)).reshape(B, n_cmp, G, CMP_BLOCK * d)
    return jax.nn.gelu(blk @ w1 + b1) @ w2


def nsa_attention(q_in, kv_in, gate_in, positions, cmp_pe, cmp_w1, cmp_b1, cmp_w2):
    B, T, _ = q_in.shape
    dt = q_in.dtype
    G, R, d = NSA_GROUPS, NSA_HEADS_PER_GROUP, NSA_HEAD_DIM
    scale = d ** -0.5
    q = apply_rope(q_in.reshape(B, T, NSA_HEADS, d), positions, NSA_ROPE_DIM).reshape(B, T, G, R, d)
    kv = kv_in.reshape(B, T, N_NSA_BRANCHES, 2, G, d)
    k_cmp = compress_blocks(apply_rope(kv[:, :, 0, 0], positions, NSA_ROPE_DIM),
                            cmp_pe[0], cmp_w1[0], cmp_b1[0], cmp_w2[0])
    v_cmp = compress_blocks(kv[:, :, 0, 1], cmp_pe[1], cmp_w1[1], cmp_b1[1], cmp_w2[1])
    k_sel = apply_rope(kv[:, :, 1, 0], positions, NSA_ROPE_DIM)
    v_sel = kv[:, :, 1, 1]
    k_win = apply_rope(kv[:, :, 2, 0], positions, NSA_ROPE_DIM)
    v_win = kv[:, :, 2, 1]

    n_cmp = k_cmp.shape[1]
    n_sel = T // SEL_BLOCK
    n_top = min(SEL_TOPK, n_sel)
    cmp_start = jnp.arange(n_cmp, dtype=jnp.int32) * CMP_STRIDE
    cmp_end = cmp_start + CMP_BLOCK - 1
    sel_start = jnp.arange(n_sel, dtype=jnp.int32) * SEL_BLOCK
    overlap = ((cmp_start[:, None] < sel_start[None, :] + SEL_BLOCK)
               & (cmp_end[:, None] >= sel_start[None, :])).astype(jnp.float32)
    blk_id = jnp.arange(n_sel, dtype=jnp.int32)
    k_blk = jnp.transpose(k_sel.reshape(B, n_sel, SEL_BLOCK, G, d), (0, 3, 1, 2, 4))
    v_blk = jnp.transpose(v_sel.reshape(B, n_sel, SEL_BLOCK, G, d), (0, 3, 1, 2, 4))
    b_ix = jnp.arange(B)[:, None, None, None]
    g_ix = jnp.arange(G)[None, None, :, None]
    in_blk = jnp.arange(SEL_BLOCK, dtype=jnp.int32)

    def cmp_sel_block(s):
        qb = lax.dynamic_slice_in_dim(q, s, SEL_Q_BLOCK, axis=1)
        t = s + jnp.arange(SEL_Q_BLOCK, dtype=jnp.int32)
        sc = jnp.einsum('bqgrd,bngd->bgrqn', qb, k_cmp) * scale
        p_cmp = masked_softmax(sc, cmp_end[None, :] <= t[:, None])
        o_cmp = jnp.einsum('bgrqn,bngd->bqgrd', p_cmp.astype(dt), v_cmp)
        imp = jnp.einsum('bgrqn,ns->bqgs', p_cmp, overlap)
        t_blk = t // SEL_BLOCK
        forced = ((blk_id[None, :] == 0) | (blk_id[None, :] == t_blk[:, None])
                  | (blk_id[None, :] == t_blk[:, None] - 1))
        valid = sel_start[None, :] <= t[:, None]
        imp = jnp.where(valid[None, :, None, :], imp + FORCE_BONUS * forced[None, :, None, :], NEG_INF)
        _, sel = lax.top_k(imp, n_top)
        kg = k_blk[b_ix, g_ix, sel]
        vg = v_blk[b_ix, g_ix, sel]
        tok = sel[..., None] * SEL_BLOCK + in_blk
        smask = (tok <= t[None, :, None, None, None]).reshape(B, SEL_Q_BLOCK, G, 1, n_top * SEL_BLOCK)
        ss = jnp.einsum('bqgrd,bqgkld->bqgrkl', qb, kg) * scale
        p_sel = masked_softmax(ss.reshape(B, SEL_Q_BLOCK, G, R, n_top * SEL_BLOCK), smask)
        o_sel = jnp.einsum('bqgrm,bqgmd->bqgrd', p_sel.astype(dt),
                           vg.reshape(B, SEL_Q_BLOCK, G, n_top * SEL_BLOCK, d))
        return o_cmp, o_sel

    k_pad = jnp.pad(k_win, ((0, 0), (WINDOW, 0), (0, 0), (0, 0)))
    v_pad = jnp.pad(v_win, ((0, 0), (WINDOW, 0), (0, 0), (0, 0)))

    def win_block(s):
        qb = lax.dynamic_slice_in_dim(q, s, Q_BLOCK, axis=1)
        kb = lax.dynamic_slice_in_dim(k_pad, s, WINDOW + Q_BLOCK, axis=1)
        vb = lax.dynamic_slice_in_dim(v_pad, s, WINDOW + Q_BLOCK, axis=1)
        t = s + jnp.arange(Q_BLOCK, dtype=jnp.int32)
        j = s - WINDOW + jnp.arange(WINDOW + Q_BLOCK, dtype=jnp.int32)
        band = (j[None, :] >= 0) & (j[None, :] <= t[:, None]) & (t[:, None] - j[None, :] < WINDOW)
        sc = jnp.einsum('bqgrd,bkgd->bqgrk', qb, kb) * scale
        p = masked_softmax(sc, band[None, :, None, None, :])
        return jnp.einsum('bqgrk,bkgd->bqgrd', p.astype(dt), vb)

    o_cmp, o_sel = map_query_blocks(cmp_sel_block, T, SEL_Q_BLOCK)
    o_win = map_query_blocks(win_block, T, Q_BLOCK)
    g = jax.nn.sigmoid(gate_in.astype(jnp.float32)).astype(dt).reshape(B, T, G, R, N_NSA_BRANCHES)
    o = g[..., 0:1] * o_cmp + g[..., 1:2] * o_sel + g[..., 2:3] * o_win
    return o.reshape(B, T, NSA_HEADS * d)


def mla_attention(c_q, c_kv, k_r, positions, q_norm_g, w_q_up, kv_norm_g, w_kv_up):
    B, T, _ = c_q.shape
    dt = c_q.dtype
    H = MLA_HEADS
    scale = (MLA_NOPE_DIM + MLA_ROPE_DIM) ** -0.5
    q = (rms_norm(c_q, q_norm_g) @ w_q_up).reshape(B, T, H, MLA_NOPE_DIM + MLA_ROPE_DIM)
    q_nope = q[..., :MLA_NOPE_DIM]
    q_rope = apply_rope(q[..., MLA_NOPE_DIM:], positions, MLA_ROPE_DIM)
    kv = (rms_norm(c_kv, kv_norm_g) @ w_kv_up).reshape(B, T, H, MLA_NOPE_DIM + MLA_V_DIM)
    k_nope = kv[..., :MLA_NOPE_DIM]
    v = kv[..., MLA_NOPE_DIM:]
    k_rope = apply_rope(k_r[:, :, None, :], positions, MLA_ROPE_DIM)[:, :, 0]
    key_pos = jnp.arange(T, dtype=jnp.int32)

    def mla_block(s):
        qn = lax.dynamic_slice_in_dim(q_nope, s, Q_BLOCK, axis=1)
        qr = lax.dynamic_slice_in_dim(q_rope, s, Q_BLOCK, axis=1)
        t = s + jnp.arange(Q_BLOCK, dtype=jnp.int32)
        sc = (jnp.einsum('bqhd,bkhd->bhqk', qn, k_nope)
              + jnp.einsum('bqhd,bkd->bhqk', qr, k_rope)) * scale
        p = masked_softmax(sc, (key_pos[None, :] <= t[:, None])[None, None])
        return jnp.einsum('bhqk,bkhd->bqhd', p.astype(dt), v)

    o = map_query_blocks(mla_block, T, Q_BLOCK)
    return o.reshape(B, T, H * MLA_V_DIM)


def token_mixer(h, positions, w_in, cmp_pe, cmp_w1, cmp_b1, cmp_w2, mla_q_norm_g, mla_w_q_up,
                mla_kv_norm_g, mla_w_kv_up, w_branch_nsa, w_branch_mla, w_out):
    B, T, _ = h.shape
    dt = h.dtype
    proj = h @ w_in
    parts = []
    off = 0
    for n in IN_SIZES:
        parts.append(proj[..., off:off + n])
        off += n
    q_nsa, kv_nsa, g_nsa, c_q, c_kv, k_r, g_merge = parts
    o_nsa = nsa_attention(q_nsa, kv_nsa, g_nsa, positions, cmp_pe, cmp_w1, cmp_b1, cmp_w2)
    o_mla = mla_attention(c_q, c_kv, k_r, positions, mla_q_norm_g, mla_w_q_up, mla_kv_norm_g, mla_w_kv_up)
    gates = jax.nn.sigmoid(g_merge.astype(jnp.float32)).astype(dt).reshape(B, T, N_BRANCHES, D_MODEL)
    merged = gates[:, :, 0] * (o_nsa @ w_branch_nsa) + gates[:, :, 1] * (o_mla @ w_branch_mla)
    return merged @ w_out


def peer_ffn(h, w_q, sub_keys, u, v):
    B, T, D = h.shape
    dt = h.dtype
    half = PEER_QUERY_DIM // 2
    qry = (h @ w_q).reshape(B, T, PEER_HEADS, 2, half)
    s = jnp.einsum('bthpd,hpnd->bthpn', qry, sub_keys).astype(jnp.float32)
    s1, i1 = lax.top_k(s[:, :, :, 0], PEER_TOPK)
    s2, i2 = lax.top_k(s[:, :, :, 1], PEER_TOPK)
    n_cand = PEER_TOPK * PEER_TOPK
    cand_s = (s1[..., :, None] + s2[..., None, :]).reshape(B, T, PEER_HEADS, n_cand)
    cand_i = (i1[..., :, None] * PEER_N_KEYS + i2[..., None, :]).reshape(B, T, PEER_HEADS, n_cand)
    top_s, pos = lax.top_k(cand_s, PEER_TOPK)
    expert = jnp.take_along_axis(cand_i, pos, axis=-1)
    gate = jax.nn.softmax(top_s, axis=-1).astype(dt)
    n_tok = B * T
    n_pick = PEER_HEADS * PEER_TOPK
    hf = h.reshape(n_tok, D)
    ef = expert.reshape(n_tok, n_pick)
    gf = gate.reshape(n_tok, n_pick)

    def expert_block(s0):
        hs = lax.dynamic_slice_in_dim(hf, s0, PEER_TOKEN_BLOCK, 0)
        es = lax.dynamic_slice_in_dim(ef, s0, PEER_TOKEN_BLOCK, 0)
        gs = lax.dynamic_slice_in_dim(gf, s0, PEER_TOKEN_BLOCK, 0)
        act = jax.nn.gelu(jnp.einsum('cd,ced->ce', hs, u[es]))
        return jnp.einsum('ce,ced->cd', gs * act, v[es])

    starts = jnp.arange(n_tok // PEER_TOKEN_BLOCK, dtype=jnp.int32) * PEER_TOKEN_BLOCK
    y = lax.map(expert_block, starts)
    return y.reshape(B, T, D)


def setup_inputs(seed: int = 0) -> dict:
    key = jax.random.key(seed)
    ks = jax.random.split(key, 26)
    L, D = DEPTH, D_MODEL

    def nrm(k, shape, scale):
        return jax.random.normal(k, shape, jnp.float32) * scale

    def gain(k, shape):
        return 1.0 + 0.01 * jax.random.normal(k, shape, jnp.float32)

    offset = jax.random.randint(ks[2], (BATCH, 1), 0, 1024, dtype=jnp.int32)
    nsa_w = NSA_HEADS * NSA_HEAD_DIM
    mla_w = MLA_HEADS * MLA_V_DIM
    return {
        'x': nrm(ks[0], (BATCH, SEQ, D), 1.0),
        'c': nrm(ks[1], (BATCH, D), 1.0),
        'positions': offset + jnp.arange(SEQ, dtype=jnp.int32)[None, :],
        'ada_w': nrm(ks[3], (L, D, 6 * D), 0.5 * D ** -0.5),
        'ada_b': nrm(ks[4], (L, 6 * D), 0.02),
        'attn_pre_g': gain(ks[5], (L, D)),
        'attn_post_g': gain(ks[6], (L, D)),
        'w_in': nrm(ks[7], (L, D, IN_COLS), D ** -0.5),
        'cmp_pe': nrm(ks[8], (L, 2, CMP_BLOCK, NSA_HEAD_DIM), 0.1),
        'cmp_w1': nrm(ks[9], (L, 2, CMP_BLOCK * NSA_HEAD_DIM, CMP_HIDDEN), (CMP_BLOCK * NSA_HEAD_DIM) ** -0.5),
        'cmp_b1': nrm(ks[10], (L, 2, CMP_HIDDEN), 0.02),
        'cmp_w2': nrm(ks[11], (L, 2, CMP_HIDDEN, NSA_HEAD_DIM), CMP_HIDDEN ** -0.5),
        'mla_q_norm_g': gain(ks[12], (L, MLA_Q_RANK)),
        'mla_w_q_up': nrm(ks[13], (L, MLA_Q_RANK, MLA_HEADS * (MLA_NOPE_DIM + MLA_ROPE_DIM)), MLA_Q_RANK ** -0.5),
        'mla_kv_norm_g': gain(ks[14], (L, MLA_KV_RANK)),
        'mla_w_kv_up': nrm(ks[15], (L, MLA_KV_RANK, MLA_HEADS * (MLA_NOPE_DIM + MLA_V_DIM)), MLA_KV_RANK ** -0.5),
        'w_branch_nsa': nrm(ks[16], (L, nsa_w, D), nsa_w ** -0.5),
        'w_branch_mla': nrm(ks[17], (L, mla_w, D), mla_w ** -0.5),
        'w_out': nrm(ks[18], (L, D, D), D ** -0.5),
        'ffn_pre_g': gain(ks[19], (L, D)),
        'ffn_post_g': gain(ks[20], (L, D)),
        'peer_w_q': nrm(ks[21], (L, D, PEER_HEADS * PEER_QUERY_DIM), D ** -0.5),
        'peer_sub_keys': nrm(ks[22], (L, PEER_HEADS, 2, PEER_N_KEYS, PEER_QUERY_DIM // 2), (PEER_QUERY_DIM // 2) ** -0.5),
        'peer_u': nrm(ks[23], (L, PEER_N_EXPERTS, D), D ** -0.5),
        'peer_v': nrm(ks[24], (L, PEER_N_EXPERTS, D), D ** -0.5),
    }


def reference(x, c, positions, ada_w, ada_b, attn_pre_g, attn_post_g, w_in, cmp_pe, cmp_w1, cmp_b1,
              cmp_w2, mla_q_norm_g, mla_w_q_up, mla_kv_norm_g, mla_w_kv_up, w_branch_nsa, w_branch_mla,
              w_out, ffn_pre_g, ffn_post_g, peer_w_q, peer_sub_keys, peer_u, peer_v):
    for l in range(DEPTH):
        mod = jax.nn.silu(c) @ ada_w[l] + ada_b[l]
        sh_a, sc_a, g_a, sh_f, sc_f, g_f = [m[:, None, :] for m in jnp.split(mod, 6, axis=-1)]
        h = rms_norm(x, attn_pre_g[l]) * (1 + sc_a) + sh_a
        y = token_mixer(h, positions, w_in[l], cmp_pe[l], cmp_w1[l], cmp_b1[l], cmp_w2[l],
                        mla_q_norm_g[l], mla_w_q_up[l], mla_kv_norm_g[l], mla_w_kv_up[l],
                        w_branch_nsa[l], w_branch_mla[l], w_out[l])
        x = x + g_a * rms_norm(y, attn_post_g[l])
        h = rms_norm(x, ffn_pre_g[l]) * (1 + sc_f) + sh_f
        y = peer_ffn(h, peer_w_q[l], peer_sub_keys[l], peer_u[l], peer_v[l])
        x = x + g_f * rms_norm(y, ffn_post_g[l])
    return x
```

```python
import functools

import numpy as np
import jax
import jax.numpy as jnp
from jax import lax
from jax.experimental import pallas as pl
from jax.experimental.pallas import tpu as pltpu

F32 = jnp.float32
BF16 = jnp.bfloat16
I32 = jnp.int32

LANES = 128
SUBLANES = 8

NSA_HEADS = 16
NSA_GROUPS = 4
NSA_R = NSA_HEADS // NSA_GROUPS
NSA_HEAD_DIM = 64
NSA_ROPE_DIM = NSA_HEAD_DIM // 4
CMP_BLOCK = 32
CMP_STRIDE = 16
CMP_HIDDEN = 256
SEL_BLOCK = 64
SEL_SHIFT = 6
SEL_TOPK = 16
WINDOW = 512
MLA_HEADS = 16
MLA_NOPE_DIM = 64
MLA_ROPE_DIM = 32
MLA_V_DIM = 64
PEER_HEADS = 8
PEER_N_KEYS = 128
PEER_TOPK = 16
ROPE_THETA = 500000.0
NORM_EPS = 1e-6
NEG = -1e30
FORCE_BONUS = 1e4
MLA_ROPE_OFF = MLA_NOPE_DIM


def _cparams(sem, vmem_mb=None):
    kw = dict(dimension_semantics=sem)
    if vmem_mb is not None:
        kw["vmem_limit_bytes"] = vmem_mb * 1024 * 1024
    return pltpu.CompilerParams(**kw)


def _rms(x, g):
    return x * lax.rsqrt(jnp.mean(x * x, axis=-1, keepdims=True) + NORM_EPS) * g


def _gelu(x):
    return jax.nn.gelu(x, approximate=True)


def _adaln_kernel(c_ref, w_ref, b_ref, o_ref):
    c = c_ref[...]
    s = (c * jax.nn.sigmoid(c)).astype(BF16)
    o_ref[...] = jnp.dot(s, w_ref[...].astype(BF16), preferred_element_type=F32) + b_ref[...]


def _adaln(c, w, b, tn=1024):
    bsz, d = c.shape
    n = w.shape[1]
    cp = jnp.pad(c, ((0, SUBLANES - bsz), (0, 0)))
    out = pl.pallas_call(
        _adaln_kernel,
        grid=(n // tn,),
        in_specs=[pl.BlockSpec((SUBLANES, d), lambda j: (0, 0)),
                  pl.BlockSpec((d, tn), lambda j: (0, j)),
                  pl.BlockSpec((1, tn), lambda j: (0, j))],
        out_specs=pl.BlockSpec((SUBLANES, tn), lambda j: (0, j)),
        out_shape=jax.ShapeDtypeStruct((SUBLANES, n), F32),
        compiler_params=_cparams(("arbitrary",), 40),
        name="adaln",
    )(cp, w, b.reshape(1, n))
    return out[:bsz]


def _rope_table_kernel(pos_ref, inv_ref, sgn_ref, o_ref):
    pos = pos_ref[...]
    for k in range(2):
        ang = pos * inv_ref[k:k + 1, :]
        o_ref[:, (2 * k) * LANES:(2 * k + 1) * LANES] = jnp.cos(ang)
        o_ref[:, (2 * k + 1) * LANES:(2 * k + 2) * LANES] = jnp.sin(ang) * sgn_ref[k:k + 1, :]


def _rope_tables(positions, tm=512):
    n = positions.size
    half_a = NSA_ROPE_DIM // 2
    half_b = MLA_ROPE_DIM // 2
    inv_a = ROPE_THETA ** (-jnp.arange(half_a, dtype=F32) * (2.0 / NSA_ROPE_DIM))
    inv_b = ROPE_THETA ** (-jnp.arange(half_b, dtype=F32) * (2.0 / MLA_ROPE_DIM))
    inv = jnp.zeros((2, LANES), F32)
    inv = inv.at[0, 0:half_a].set(inv_a).at[0, half_a:2 * half_a].set(inv_a)
    o = MLA_ROPE_OFF
    inv = inv.at[1, o:o + half_b].set(inv_b).at[1, o + half_b:o + 2 * half_b].set(inv_b)
    sgn = np.zeros((2, LANES), np.float32)
    sgn[0, 0:half_a] = -1.0
    sgn[0, half_a:2 * half_a] = 1.0
    sgn[1, o:o + half_b] = -1.0
    sgn[1, o + half_b:o + 2 * half_b] = 1.0
    posf = jnp.broadcast_to(positions.reshape(n, 1).astype(F32), (n, LANES))
    return pl.pallas_call(
        _rope_table_kernel,
        grid=(n // tm,),
        in_specs=[pl.BlockSpec((tm, LANES), lambda i: (i, 0)),
                  pl.BlockSpec((2, LANES), lambda i: (0, 0)),
                  pl.BlockSpec((2, LANES), lambda i: (0, 0))],
        out_specs=pl.BlockSpec((tm, 4 * LANES), lambda i: (i, 0)),
        out_shape=jax.ShapeDtypeStruct((n, 4 * LANES), F32),
        compiler_params=_cparams(("arbitrary",)),
        name="rope_tables",
    )(posf, inv, jnp.asarray(sgn))


def _rope_slot(a, cos, sin, half, off):
    lane = lax.broadcasted_iota(I32, (1, LANES), 1)
    first = (lane - off) < half
    partner = jnp.where(first, pltpu.roll(a, LANES - half, 1), pltpu.roll(a, half, 1))
    return a * cos + partner * sin


def _normmod_kernel(x_ref, g_ref, sc_ref, sh_ref, o_ref):
    y = _rms(x_ref[...], g_ref[...])
    o_ref[...] = (y * (1.0 + sc_ref[0]) + sh_ref[0]).astype(o_ref.dtype)


def _normmod(x, g, sc, sh, seq, tm=512):
    n, d = x.shape
    bsz = sc.shape[0]
    bspec = pl.BlockSpec((1, 1, d), lambda i: ((i * tm) // seq, 0, 0))
    return pl.pallas_call(
        _normmod_kernel,
        grid=(n // tm,),
        in_specs=[pl.BlockSpec((tm, d), lambda i: (i, 0)),
                  pl.BlockSpec((1, d), lambda i: (0, 0)), bspec, bspec],
        out_specs=pl.BlockSpec((tm, d), lambda i: (i, 0)),
        out_shape=jax.ShapeDtypeStruct((n, d), BF16),
        compiler_params=_cparams(("arbitrary",)),
        name="normmod",
    )(x, g.reshape(1, d), sc.reshape(bsz, 1, d), sh.reshape(bsz, 1, d))


def _norm_res_kernel(x_ref, y_ref, g_ref, ga_ref, o_ref):
    o_ref[...] = x_ref[...] + ga_ref[0] * _rms(y_ref[...], g_ref[...])


def _norm_res(x, y, g, gate, seq, tm=512):
    n, d = x.shape
    bsz = gate.shape[0]
    row = pl.BlockSpec((tm, d), lambda i: (i, 0))
    return pl.pallas_call(
        _norm_res_kernel,
        grid=(n // tm,),
        in_specs=[row, row, pl.BlockSpec((1, d), lambda i: (0, 0)),
                  pl.BlockSpec((1, 1, d), lambda i: ((i * tm) // seq, 0, 0))],
        out_specs=row,
        out_shape=jax.ShapeDtypeStruct((n, d), F32),
        compiler_params=_cparams(("arbitrary",)),
        name="norm_res",
    )(x, y, g.reshape(1, d), gate.reshape(bsz, 1, d))


def _mm(x, w, epi, *, tm, tn, out_dtype, name, extras=(), pro=None, x_spec=None, vmem_mb=None):
    m = x.shape[0]
    k, nc = w.shape
    if x_spec is None:
        x_spec = pl.BlockSpec((tm, k), lambda i, j: (i, 0))
    in_specs = [x_spec, pl.BlockSpec((k, tn), lambda i, j: (0, j))]
    args = [x, w]
    for arr, spec in extras:
        in_specs.append(spec)
        args.append(arr)

    def kern(x_ref, w_ref, *rest):
        o_ref = rest[-1]
        ex = rest[:-1]
        xv = x_ref[...]
        if pro is not None:
            xv = pro(xv, *ex)
        acc = jnp.dot(xv, w_ref[...], preferred_element_type=F32)
        epi(acc, o_ref, *ex)

    return pl.pallas_call(
        kern,
        grid=(m // tm, nc // tn),
        in_specs=in_specs,
        out_specs=pl.BlockSpec((tm, tn), lambda i, j: (i, j)),
        out_shape=jax.ShapeDtypeStruct((m, nc), out_dtype),
        compiler_params=_cparams(("arbitrary", "arbitrary"), vmem_mb),
        name=name,
    )(*args)


def _epi_plain(acc, o_ref, *ex):
    o_ref[...] = acc.astype(o_ref.dtype)


def _epi_sigmoid(acc, o_ref, *ex):
    o_ref[...] = jax.nn.sigmoid(acc).astype(o_ref.dtype)


def _make_epi_rope_a(scale, seq=None, tm=None, onehot=False):
    half = NSA_ROPE_DIM // 2

    def epi(acc, o_ref, tab_ref):
        cos = tab_ref[:, 0:LANES]
        sin = tab_ref[:, LANES:2 * LANES]
        if onehot:
            base = lax.rem(pl.program_id(0) * tm, seq)
            t = base + lax.broadcasted_iota(I32, (acc.shape[0], LANES), 0)
            lane = lax.broadcasted_iota(I32, (acc.shape[0], LANES), 1)
            hot = jnp.where(lane - SEL_BLOCK == jnp.right_shift(t, SEL_SHIFT), 1.0, 0.0)
        for s in range(acc.shape[1] // LANES):
            r = _rope_slot(acc[:, s * LANES:(s + 1) * LANES], cos, sin, half, 0)
            if scale != 1.0:
                r = r * scale
            if onehot:
                r = r + hot
            o_ref[:, s * LANES:(s + 1) * LANES] = r.astype(o_ref.dtype)
    return epi


def _epi_rope_b(acc, o_ref, tab_ref, *ex):
    cos = tab_ref[:, 2 * LANES:3 * LANES]
    sin = tab_ref[:, 3 * LANES:4 * LANES]
    for s in range(acc.shape[1] // LANES):
        r = _rope_slot(acc[:, s * LANES:(s + 1) * LANES], cos, sin, MLA_ROPE_DIM // 2, MLA_ROPE_OFF)
        o_ref[:, s * LANES:(s + 1) * LANES] = r.astype(o_ref.dtype)


def _epi_gate_kr(acc, o_ref, tab_ref):
    o_ref[:, 0:LANES] = jax.nn.sigmoid(acc[:, 0:LANES]).astype(o_ref.dtype)
    cos = tab_ref[:, 2 * LANES:3 * LANES]
    sin = tab_ref[:, 3 * LANES:4 * LANES]
    r = _rope_slot(acc[:, LANES:2 * LANES], cos, sin, MLA_ROPE_DIM // 2, MLA_ROPE_OFF)
    o_ref[:, LANES:2 * LANES] = r.astype(o_ref.dtype)


def _pro_rms(xv, g_ref, *ex):
    return _rms(xv, g_ref[...]).astype(BF16)


def _epi_add_kr(acc, o_ref, g_ref, kr_ref):
    kr = kr_ref[...].astype(F32)
    for s in range(acc.shape[1] // LANES):
        o_ref[:, s * LANES:(s + 1) * LANES] = (acc[:, s * LANES:(s + 1) * LANES] + kr).astype(o_ref.dtype)


def _epi_plain_g(acc, o_ref, g_ref):
    o_ref[...] = acc.astype(o_ref.dtype)


def _compress_kernel(k16_ref, v16_ref, w1_ref, pe_ref, b1_ref, w2_ref, kc_ref, vc_ref):
    half = w1_ref.shape[1] // 2
    for which, (src, dst) in enumerate(((k16_ref, kc_ref), (v16_ref, vc_ref))):
        xb = src[...]
        a = jnp.dot(xb, w1_ref[which, 0:half, :], preferred_element_type=F32)
        b = jnp.dot(xb, w1_ref[which, half:2 * half, :], preferred_element_type=F32)
        c = jnp.dot(pe_ref[which], w1_ref[which], preferred_element_type=F32)[0:1, :] + b1_ref[which]
        rows = a.shape[0]
        pre = a + pltpu.roll(b, rows - 1, 0) + c
        hid = _gelu(pre).astype(BF16)
        dst[...] = jnp.dot(hid, w2_ref[which], preferred_element_type=F32).astype(dst.dtype)


def _compress(k16, v16, w1, pe, b1, w2p):
    bg, rows, feat = k16.shape
    blk = pl.BlockSpec((None, rows, feat), lambda i: (i, 0, 0))
    full = lambda a: pl.BlockSpec(a.shape, lambda i: (0,) * a.ndim)
    out = pl.BlockSpec((None, rows, LANES), lambda i: (i, 0, 0))
    return pl.pallas_call(
        _compress_kernel,
        grid=(bg,),
        in_specs=[blk, blk, full(w1), full(pe), full(b1), full(w2p)],
        out_specs=[out, out],
        out_shape=[jax.ShapeDtypeStruct((bg, rows, LANES), BF16)] * 2,
        compiler_params=_cparams(("arbitrary",)),
        name="nsa_compress",
    )(k16, v16, w1, pe, b1, w2p)


def _cmp_attn_kernel(q_ref, kc_ref, vc_ref, ovt_ref, o_ref, bias_ref, *, tq, n_sel):
    qi = pl.program_id(2)
    ncmp = kc_ref.shape[0]
    t_row = qi * tq + lax.broadcasted_iota(I32, (tq, ncmp), 0)
    n_col = lax.broadcasted_iota(I32, (tq, ncmp), 1)
    cmask = (n_col * CMP_STRIDE + (CMP_BLOCK - 1)) <= t_row
    kc = kc_ref[...]
    vc = vc_ref[...]
    imp_t = jnp.zeros((n_sel, tq), F32)
    for r in range(NSA_R):
        q = q_ref[:, r * LANES:(r + 1) * LANES]
        s = lax.dot_general(q, kc, (((1,), (1,)), ((), ())), preferred_element_type=F32)
        s = jnp.where(cmask, s, NEG)
        e = jnp.exp(s - jnp.max(s, axis=1, keepdims=True))
        p = e / jnp.sum(e, axis=1, keepdims=True)
        p = jnp.where(cmask, p, 0.0).astype(BF16)
        o_ref[:, r * LANES:(r + 1) * LANES] = jnp.dot(p, vc, preferred_element_type=F32).astype(o_ref.dtype)
        imp_t = imp_t + lax.dot_general(ovt_ref[...], p, (((1,), (1,)), ((), ())),
                                        preferred_element_type=F32)
    blk = lax.broadcasted_iota(I32, (n_sel, tq), 0)
    t = qi * tq + lax.broadcasted_iota(I32, (n_sel, tq), 1)
    tb = jnp.right_shift(t, SEL_SHIFT)
    forced = (blk == 0) | (blk == tb) | (blk == tb - 1)
    valid = blk * SEL_BLOCK <= t
    x = jnp.where(valid, imp_t + jnp.where(forced, FORCE_BONUS, 0.0), NEG)
    sel = jnp.zeros((n_sel, tq), F32)
    for _ in range(min(SEL_TOPK, n_sel)):
        m = jnp.max(x, axis=0, keepdims=True)
        idx = jnp.min(jnp.where(x == m, blk, n_sel), axis=0, keepdims=True)
        hit = blk == idx
        sel = jnp.where(hit, 1.0, sel)
        x = jnp.where(hit, -jnp.inf, x)
    bias_t = jnp.where(sel > 0.5, 0.0, NEG)
    parts = [jnp.zeros((SEL_BLOCK, tq), F32), bias_t]
    if LANES - SEL_BLOCK - n_sel > 0:
        parts.append(jnp.zeros((LANES - SEL_BLOCK - n_sel, tq), F32))
    full_t = jnp.concatenate(parts, axis=0)
    bias_ref[...] = full_t.T.astype(bias_ref.dtype)


def _cmp_attn(qn, kc, vc, ovt, bsz, seq, tq=256):
    n = qn.shape[0]
    nq = seq // tq
    n_sel = seq // SEL_BLOCK
    ncmp = kc.shape[1]
    qspec = pl.BlockSpec((tq, NSA_R * LANES), lambda b, g, qi: (b * nq + qi, g))
    kspec = pl.BlockSpec((None, ncmp, LANES), lambda b, g, qi: (b * NSA_GROUPS + g, 0, 0))
    return pl.pallas_call(
        functools.partial(_cmp_attn_kernel, tq=tq, n_sel=n_sel),
        grid=(bsz, NSA_GROUPS, nq),
        in_specs=[qspec, kspec, kspec, pl.BlockSpec(ovt.shape, lambda b, g, qi: (0, 0))],
        out_specs=[qspec, pl.BlockSpec((tq, LANES), lambda b, g, qi: (b * nq + qi, g))],
        out_shape=[jax.ShapeDtypeStruct((n, NSA_HEADS * LANES), BF16),
                   jax.ShapeDtypeStruct((n, NSA_GROUPS * LANES), BF16)],
        compiler_params=_cparams(("arbitrary",) * 3),
        name="nsa_cmp_attn",
    )(qn, kc, vc, ovt)


def _flash_kernel(*refs, mode, tq, tk, nr, scale, window):
    if mode == "sel":
        q_ref, k_ref, v_ref, bias_ref, o_ref, qs, m_s, l_s, acc_s = refs
    else:
        q_ref, k_ref, v_ref, o_ref, qs, m_s, l_s, acc_s = refs
    qi = pl.program_id(2)
    kk = pl.program_id(3)
    nk = pl.num_programs(3)
    rows = nr * tq

    @pl.when(kk == 0)
    def _init():
        for r in range(nr):
            qr = q_ref[:, r * LANES:(r + 1) * LANES]
            if mode == "sel":
                qr = qr + bias_ref[...]
            qs[r * tq:(r + 1) * tq, :] = qr
        m_s[...] = jnp.full(m_s.shape, -jnp.inf, F32)
        l_s[...] = jnp.zeros(l_s.shape, F32)
        acc_s[...] = jnp.zeros(acc_s.shape, F32)

    if mode == "win":
        kidx = qi * (tq // tk) - window // tk + kk
        needed = kidx >= 0
        full_vis = (kidx * tk + tk - 1 <= qi * tq) & (qi * tq + tq - 1 - kidx * tk < window)
    else:
        kidx = kk
        needed = kk * tk <= qi * tq + tq - 1
        full_vis = kk * tk + tk - 1 <= qi * tq

    def update(masked):
        s = lax.dot_general(qs[...], k_ref[...], (((1,), (1,)), ((), ())), preferred_element_type=F32)
        if scale != 1.0:
            s = s * scale
        if masked:
            t = qi * tq + (lax.broadcasted_iota(I32, (rows, tk), 0) & (tq - 1))
            col = kidx * tk + lax.broadcasted_iota(I32, (rows, tk), 1)
            vis = col <= t
            if mode == "win":
                vis = vis & (t - col < window)
            s = jnp.where(vis, s, NEG)
        m_prev = m_s[...]
        m_new = jnp.maximum(m_prev, jnp.max(s, axis=1, keepdims=True))
        p = jnp.exp(s - m_new)
        alpha = jnp.exp(m_prev - m_new)
        l_s[...] = alpha * l_s[...] + jnp.sum(p, axis=1, keepdims=True)
        acc_s[...] = alpha * acc_s[...] + jnp.dot(p.astype(BF16), v_ref[...], preferred_element_type=F32)
        m_s[...] = m_new

    @pl.when(needed & full_vis)
    def _full():
        update(False)

    @pl.when(needed & jnp.logical_not(full_vis))
    def _edge():
        update(True)

    @pl.when(kk == nk - 1)
    def _fin():
        out = acc_s[...] / l_s[...]
        for r in range(nr):
            o_ref[:, r * LANES:(r + 1) * LANES] = out[r * tq:(r + 1) * tq, :].astype(o_ref.dtype)


def _flash(q, k, v, *, mode, bsz, seq, nslots, nr, tq, tk, k_off=0, v_off=0, bias=None,
           scale=1.0, name):
    assert tq & (tq - 1) == 0
    n = q.shape[0]
    nq = seq // tq
    nkb = seq // tk
    if mode == "win":
        assert tq % tk == 0 and WINDOW % tk == 0
        steps = WINDOW // tk + tq // tk

        def krow(b, qi, kk):
            return b * nkb + jnp.maximum(qi * (tq // tk) - WINDOW // tk + kk, 0)
    else:
        steps = nkb

        def krow(b, qi, kk):
            return b * nkb + jnp.minimum(kk, (qi * tq + tq - 1) // tk)

    qspec = pl.BlockSpec((tq, nr * LANES), lambda b, g, qi, kk: (b * nq + qi, g))
    in_specs = [qspec,
                pl.BlockSpec((tk, LANES), lambda b, g, qi, kk: (krow(b, qi, kk), k_off + g)),
                pl.BlockSpec((tk, LANES), lambda b, g, qi, kk: (krow(b, qi, kk), v_off + g))]
    args = [q, k, v]
    if mode == "sel":
        in_specs.append(pl.BlockSpec((tq, LANES), lambda b, g, qi, kk: (b * nq + qi, g)))
        args.append(bias)
    rows = nr * tq
    return pl.pallas_call(
        functools.partial(_flash_kernel, mode=mode, tq=tq, tk=tk, nr=nr, scale=scale, window=WINDOW),
        grid=(bsz, nslots, nq, steps),
        in_specs=in_specs,
        out_specs=qspec,
        out_shape=jax.ShapeDtypeStruct((n, nslots * nr * LANES), BF16),
        scratch_shapes=[pltpu.VMEM((rows, LANES), BF16), pltpu.VMEM((rows, 1), F32),
                        pltpu.VMEM((rows, 1), F32), pltpu.VMEM((rows, LANES), F32)],
        compiler_params=_cparams(("arbitrary",) * 4, 48),
        name=name,
    )(*args)


def _combine_kernel(oc_ref, os_ref, ow_ref, g_ref, e_ref, o_ref):
    g = g_ref[...]
    acc = None
    for br, src in enumerate((oc_ref, os_ref, ow_ref)):
        ge = jnp.dot(g, e_ref[br], preferred_element_type=F32)
        term = ge * src[...].astype(F32)
        acc = term if acc is None else acc + term
    o_ref[...] = acc.astype(o_ref.dtype)


def _combine(o_cmp, o_sel, o_win, gates, gates_col, tm=512):
    n, w = o_cmp.shape
    e = np.zeros((3, LANES, w), np.float32)
    for h in range(NSA_HEADS):
        for br in range(3):
            e[br, h * 3 + br, h * LANES:h * LANES + NSA_HEAD_DIM] = 1.0
    row = pl.BlockSpec((tm, w), lambda i: (i, 0))
    return pl.pallas_call(
        _combine_kernel,
        grid=(n // tm,),
        in_specs=[row, row, row, pl.BlockSpec((tm, LANES), lambda i: (i, gates_col)),
                  pl.BlockSpec(e.shape, lambda i: (0, 0, 0))],
        out_specs=row,
        out_shape=jax.ShapeDtypeStruct((n, w), BF16),
        compiler_params=_cparams(("arbitrary",)),
        name="nsa_combine",
    )(o_cmp, o_sel, o_win, gates, jnp.asarray(e, BF16))


def _merge_kernel(a_ref, wa_ref, b_ref, wb_ref, g0_ref, g1_ref, o_ref):
    ya = jnp.dot(a_ref[...], wa_ref[...], preferred_element_type=F32)
    yb = jnp.dot(b_ref[...], wb_ref[...], preferred_element_type=F32)
    o_ref[...] = (g0_ref[...].astype(F32) * ya + g1_ref[...].astype(F32) * yb).astype(o_ref.dtype)


def _merge(o_nsa, w_nsa, o_mla, w_mla, gm, tm=512, tn=512):
    n, k = o_nsa.shape
    d = w_nsa.shape[1]
    nj = d // tn
    row = pl.BlockSpec((tm, k), lambda i, j: (i, 0))
    wsp = pl.BlockSpec((k, tn), lambda i, j: (0, j))
    return pl.pallas_call(
        _merge_kernel,
        grid=(n // tm, nj),
        in_specs=[row, wsp, row, wsp,
                  pl.BlockSpec((tm, tn), lambda i, j: (i, j)),
                  pl.BlockSpec((tm, tn), lambda i, j: (i, nj + j))],
        out_specs=pl.BlockSpec((tm, tn), lambda i, j: (i, j)),
        out_shape=jax.ShapeDtypeStruct((n, d), BF16),
        compiler_params=_cparams(("arbitrary", "arbitrary")),
        name="branch_merge",
    )(o_nsa, w_nsa, o_mla, w_mla, gm, gm)


def _extract_sorted(x, nrounds, out_rows):
    iota = lax.broadcasted_iota(I32, x.shape, 0)
    orow = lax.broadcasted_iota(I32, (out_rows, x.shape[1]), 0)
    vals = jnp.full((out_rows, x.shape[1]), -jnp.inf, F32)
    m = None
    for k in range(nrounds):
        m = jnp.max(x, axis=0, keepdims=True)
        idx = jnp.min(jnp.where(x == m, iota, x.shape[0]), axis=0, keepdims=True)
        x = jnp.where(iota == idx, -jnp.inf, x)
        vals = jnp.where(orow == k, m, vals)
    return vals


def _peer_route_kernel(q_ref, sk_ref, st_ref, et_ref, tau_ref):
    nk = PEER_N_KEYS
    tr = q_ref.shape[0]
    for h in range(PEER_HEADS):
        s = []
        for p in range(2):
            c = (2 * h + p) * nk
            s.append(lax.dot_general(sk_ref[2 * h + p], q_ref[:, c:c + nk], (((1,), (1,)), ((), ())),
                                     preferred_element_type=F32))
        s1, s2 = s
        v1 = _extract_sorted(s1, PEER_TOPK, PEER_TOPK)
        v2 = _extract_sorted(s2, PEER_TOPK, PEER_TOPK)
        cands = []
        for i in range(PEER_TOPK):
            cnt = PEER_TOPK // (i + 1)
            rows = PEER_TOPK if i == 0 else SUBLANES
            c = v1[i:i + 1, :] + v2[0:rows, :]
            if cnt < rows:
                c = jnp.where(lax.broadcasted_iota(I32, (rows, tr), 0) < cnt, c, -jnp.inf)
            cands.append(c)
        top = _extract_sorted(jnp.concatenate(cands, axis=0), PEER_TOPK, PEER_TOPK)
        tau = top[PEER_TOPK - 1:PEER_TOPK, :]
        z = jnp.sum(jnp.exp(top - top[0:1, :]), axis=0, keepdims=True)
        st_ref[(2 * h) * nk:(2 * h + 1) * nk, :] = s1
        st_ref[(2 * h + 1) * nk:(2 * h + 2) * nk, :] = s2
        et_ref[(2 * h) * nk:(2 * h + 1) * nk, :] = jnp.exp(s1 - v1[0:1, :]) / z
        et_ref[(2 * h + 1) * nk:(2 * h + 2) * nk, :] = jnp.exp(s2 - v2[0:1, :])
        tau_ref[h:h + 1, :] = tau


def _peer_route(qry, sk, tr=256):
    n, w = qry.shape
    col = pl.BlockSpec((w, tr), lambda i: (0, i))
    return pl.pallas_call(
        _peer_route_kernel,
        grid=(n // tr,),
        in_specs=[pl.BlockSpec((tr, w), lambda i: (i, 0)),
                  pl.BlockSpec(sk.shape, lambda i: (0, 0, 0))],
        out_specs=[col, col, pl.BlockSpec((PEER_HEADS, tr), lambda i: (0, i))],
        out_shape=[jax.ShapeDtypeStruct((w, n), F32), jax.ShapeDtypeStruct((w, n), F32),
                   jax.ShapeDtypeStruct((PEER_HEADS, n), F32)],
        compiler_params=_cparams(("arbitrary",), 40),
        name="peer_route",
    )(qry, sk)


def _peer_dense_kernel(h_ref, u_ref, vt_ref, st_ref, et_ref, tau_ref, o_ref, acc_s, pt_s, *, te):
    j = pl.program_id(1)
    nk = PEER_N_KEYS

    @pl.when(j == 0)
    def _init():
        acc_s[...] = jnp.zeros(acc_s.shape, F32)

    zt = lax.dot_general(u_ref[...], h_ref[...], (((1,), (1,)), ((), ())), preferred_element_type=F32)
    for al in range(te // nk):
        a = j * (te // nk) + al
        g = None
        for h in range(PEER_HEADS):
            s1 = st_ref[pl.ds(2 * h * nk + a, 1), :]
            e1 = et_ref[pl.ds(2 * h * nk + a, 1), :]
            s2 = st_ref[(2 * h + 1) * nk:(2 * h + 2) * nk, :]
            e2 = et_ref[(2 * h + 1) * nk:(2 * h + 2) * nk, :]
            term = jnp.where((s1 + s2) >= tau_ref[h:h + 1, :], e1 * e2, 0.0)
            g = term if g is None else g + term
        pt_s[al * nk:(al + 1) * nk, :] = (g * _gelu(zt[al * nk:(al + 1) * nk, :])).astype(BF16)
    acc_s[...] += jnp.dot(vt_ref[...], pt_s[...], preferred_element_type=F32)

    @pl.when(j == pl.num_programs(1) - 1)
    def _fin():
        o_ref[...] = acc_s[...].T


def _peer_dense(h2, u_bf, vt_bf, st, et, tau, tm=512, te=512):
    n, d = h2.shape
    ne = u_bf.shape[0]
    w = st.shape[0]
    return pl.pallas_call(
        functools.partial(_peer_dense_kernel, te=te),
        grid=(n // tm, ne // te),
        in_specs=[pl.BlockSpec((tm, d), lambda i, j: (i, 0)),
                  pl.BlockSpec((te, d), lambda i, j: (j, 0)),
                  pl.BlockSpec((d, te), lambda i, j: (0, j)),
                  pl.BlockSpec((w, tm), lambda i, j: (0, i)),
                  pl.BlockSpec((w, tm), lambda i, j: (0, i)),
                  pl.BlockSpec((PEER_HEADS, tm), lambda i, j: (0, i))],
        out_specs=pl.BlockSpec((tm, d), lambda i, j: (i, 0)),
        out_shape=jax.ShapeDtypeStruct((n, d), F32),
        scratch_shapes=[pltpu.VMEM((d, tm), F32), pltpu.VMEM((te, tm), BF16)],
        compiler_params=_cparams(("arbitrary", "arbitrary"), 56),
        name="peer_dense",
    )(h2, u_bf, vt_bf, st, et, tau)


def _slots(w, n, width, off=0):
    k = w.shape[0]
    w = w.reshape(k, n, width)
    w = jnp.pad(w, ((0, 0), (0, 0), (off, LANES - width - off)))
    return w.reshape(k, n * LANES)


def _slot_rows(w, n, width):
    d = w.shape[1]
    w = w.reshape(n, width, d)
    w = jnp.pad(w, ((0, 0), (0, LANES - width), (0, 0)))
    return w.reshape(n * LANES, d)


def _overlap_t(seq):
    n_cmp_rows = seq // CMP_STRIDE
    n_sel = seq // SEL_BLOCK
    cs = np.arange(n_cmp_rows) * CMP_STRIDE
    ce = cs + CMP_BLOCK - 1
    ss = np.arange(n_sel) * SEL_BLOCK
    ov = (cs[None, :] < ss[:, None] + SEL_BLOCK) & (ce[None, :] >= ss[:, None])
    ov[:, n_cmp_rows - 1] = False
    return jnp.asarray(ov.astype(np.float32), BF16)


def _token_mixer(h, positions, bsz, seq, w_in, cmp_pe, cmp_w1, cmp_b1, cmp_w2, mla_q_norm_g,
                 mla_w_q_up, mla_kv_norm_g, mla_w_kv_up, w_branch_nsa, w_branch_mla, w_out):
    d = h.shape[1]
    g_, hd = NSA_GROUPS, NSA_HEAD_DIM
    tab = _rope_tables(positions)
    tab_spec = pl.BlockSpec((512, 4 * LANES), lambda i, j: (i, 0))

    wb = w_in.astype(BF16)
    o1 = NSA_HEADS * hd
    kv = wb[:, o1:o1 + 3 * 2 * g_ * hd].reshape(d, 3, 2, g_ * hd)
    o2 = o1 + 3 * 2 * g_ * hd
    o3 = o2 + 3 * NSA_HEADS
    mq = mla_w_q_up.shape[0]
    mkv = mla_w_kv_up.shape[0]
    o4 = o3 + mq
    o5 = o4 + mkv
    o6 = o5 + MLA_ROPE_DIM
    w_q = _slots(wb[:, :o1], NSA_HEADS, hd)
    w_kk = jnp.concatenate([_slots(kv[:, 0, 0], g_, hd), _slots(kv[:, 2, 0], g_, hd)], axis=1)
    w_ks = _slots(kv[:, 1, 0], g_, hd)
    w_vv = jnp.concatenate([_slots(kv[:, 1, 1], g_, hd), _slots(kv[:, 2, 1], g_, hd),
                            kv[:, 0, 1], jnp.zeros((d, g_ * hd), BF16)], axis=1)
    w_c = wb[:, o3:o5]
    w_gk = jnp.concatenate([jnp.pad(wb[:, o2:o3], ((0, 0), (0, LANES - 3 * NSA_HEADS))),
                            _slots(wb[:, o5:o6], 1, MLA_ROPE_DIM, MLA_ROPE_OFF)], axis=1)
    w_gm = wb[:, o6:]

    tm = 512
    qn = _mm(h, w_q, _make_epi_rope_a(hd ** -0.5), tm=tm, tn=512, out_dtype=BF16, name="proj_q",
             extras=[(tab, tab_spec)])
    kk = _mm(h, w_kk, _make_epi_rope_a(1.0), tm=tm, tn=512, out_dtype=BF16, name="proj_k",
             extras=[(tab, tab_spec)])
    ks = _mm(h, w_ks, _make_epi_rope_a(1.0, seq, tm, True), tm=tm, tn=512, out_dtype=BF16,
             name="proj_ksel", extras=[(tab, tab_spec)])
    vv = _mm(h, w_vv, _epi_plain, tm=tm, tn=512, out_dtype=BF16, name="proj_v")
    cqkv = _mm(h, w_c, _epi_plain, tm=tm, tn=mq + mkv, out_dtype=F32, name="proj_c")
    gk = _mm(h, w_gk, _epi_gate_kr, tm=tm, tn=2 * LANES, out_dtype=BF16, name="proj_gate_kr",
             extras=[(tab, tab_spec)])
    gm = _mm(h, w_gm, _epi_sigmoid, tm=tm, tn=512, out_dtype=BF16, name="proj_gmerge")

    def to16(a, width):
        a = a.reshape(bsz, seq, g_, width)[..., :hd]
        a = jnp.transpose(a, (0, 2, 1, 3))
        return a.reshape(bsz * g_, seq // CMP_STRIDE, CMP_STRIDE * hd)

    k16 = to16(kk[:, :g_ * LANES], LANES)
    v16 = to16(vv[:, 2 * g_ * LANES:2 * g_ * LANES + g_ * hd], hd)
    pe = jnp.pad(cmp_pe.reshape(2, 1, CMP_BLOCK * hd), ((0, 0), (0, SUBLANES - 1), (0, 0))).astype(BF16)
    w2p = jnp.pad(cmp_w2, ((0, 0), (0, 0), (0, LANES - hd))).astype(BF16)
    kc, vc = _compress(k16, v16, cmp_w1.astype(BF16), pe, cmp_b1.reshape(2, 1, CMP_HIDDEN), w2p)
    o_cmp, bias = _cmp_attn(qn, kc, vc, _overlap_t(seq), bsz, seq)
    o_sel = _flash(qn, ks, vv, mode="sel", bsz=bsz, seq=seq, nslots=g_, nr=NSA_R, tq=128, tk=512,
                   bias=bias, name="nsa_sel_attn")
    o_win = _flash(qn, kk, vv, mode="win", bsz=bsz, seq=seq, nslots=g_, nr=NSA_R, tq=256, tk=256,
                   k_off=g_, v_off=g_, name="nsa_win_attn")
    o_nsa = _combine(o_cmp, o_sel, o_win, gk, 0)

    qd = MLA_NOPE_DIM + MLA_ROPE_DIM
    w_qup = _slots(mla_w_q_up.astype(BF16), MLA_HEADS, qd)
    kvu = mla_w_kv_up.astype(BF16).reshape(mkv, MLA_HEADS, MLA_NOPE_DIM + MLA_V_DIM)
    w_kup = _slots(kvu[:, :, :MLA_NOPE_DIM].reshape(mkv, -1), MLA_HEADS, MLA_NOPE_DIM)
    w_vup = _slots(kvu[:, :, MLA_NOPE_DIM:].reshape(mkv, -1), MLA_HEADS, MLA_V_DIM)
    gq_spec = pl.BlockSpec((1, mq), lambda i, j: (0, 0))
    gkv_spec = pl.BlockSpec((1, mkv), lambda i, j: (0, 0))
    ckv_spec = pl.BlockSpec((tm, mkv), lambda i, j: (i, mq // mkv))
    q_mla = _mm(cqkv, w_qup, lambda acc, o, g, t: _epi_rope_b(acc, o, t), tm=tm, tn=512,
                out_dtype=BF16, name="mla_q_up", pro=_pro_rms,
                x_spec=pl.BlockSpec((tm, mq), lambda i, j: (i, 0)),
                extras=[(mla_q_norm_g.reshape(1, mq), gq_spec), (tab, tab_spec)])
    k_mla = _mm(cqkv, w_kup, _epi_add_kr, tm=tm, tn=512, out_dtype=BF16, name="mla_k_up",
                pro=_pro_rms, x_spec=ckv_spec,
                extras=[(mla_kv_norm_g.reshape(1, mkv), gkv_spec),
                        (gk, pl.BlockSpec((tm, LANES), lambda i, j: (i, 1)))])
    v_mla = _mm(cqkv, w_vup, _epi_plain_g, tm=tm, tn=512, out_dtype=BF16, name="mla_v_up",
                pro=_pro_rms, x_spec=ckv_spec,
                extras=[(mla_kv_norm_g.reshape(1, mkv), gkv_spec)])
    o_mla = _flash(q_mla, k_mla, v_mla, mode="causal", bsz=bsz, seq=seq, nslots=MLA_HEADS, nr=1,
                   tq=512, tk=512, scale=qd ** -0.5, name="mla_attn")

    merged = _merge(o_nsa, _slot_rows(w_branch_nsa.astype(BF16), NSA_HEADS, hd),
                    o_mla, _slot_rows(w_branch_mla.astype(BF16), MLA_HEADS, MLA_V_DIM), gm)
    return _mm(merged, w_out.astype(BF16), _epi_plain, tm=tm, tn=512, out_dtype=F32, name="out_proj")


def _peer(h2, peer_w_q, peer_sub_keys, peer_u, peer_v):
    qry = _mm(h2, peer_w_q.astype(BF16), _epi_plain, tm=512, tn=512, out_dtype=BF16, name="peer_q")
    sk = peer_sub_keys.astype(BF16).reshape(2 * PEER_HEADS, PEER_N_KEYS, -1)
    st, et, tau = _peer_route(qry, sk)
    return _peer_dense(h2, peer_u.astype(BF16), jnp.transpose(peer_v).astype(BF16), st, et, tau)


def _block(x, c, positions, ada_w, ada_b, attn_pre_g, attn_post_g, w_in, cmp_pe, cmp_w1, cmp_b1,
           cmp_w2, mla_q_norm_g, mla_w_q_up, mla_kv_norm_g, mla_w_kv_up, w_branch_nsa, w_branch_mla,
           w_out, ffn_pre_g, ffn_post_g, peer_w_q, peer_sub_keys, peer_u, peer_v):
    bsz, seq, d = x.shape
    x2 = x.reshape(bsz * seq, d)
    mod = _adaln(c, ada_w, ada_b)
    sh_a, sc_a, g_a, sh_f, sc_f, g_f = [mod[:, i * d:(i + 1) * d] for i in range(6)]

    h = _normmod(x2, attn_pre_g, sc_a, sh_a, seq)
    y_attn = _token_mixer(h, positions, bsz, seq, w_in, cmp_pe, cmp_w1, cmp_b1, cmp_w2, mla_q_norm_g,
                          mla_w_q_up, mla_kv_norm_g, mla_w_kv_up, w_branch_nsa, w_branch_mla, w_out)
    x1 = _norm_res(x2, y_attn, attn_post_g, g_a, seq)

    h2 = _normmod(x1, ffn_pre_g, sc_f, sh_f, seq)
    y_ffn = _peer(h2, peer_w_q, peer_sub_keys, peer_u, peer_v)
    out = _norm_res(x1, y_ffn, ffn_post_g, g_f, seq)
    return out.reshape(bsz, seq, d)


def kernel(x, c, positions, ada_w, ada_b, attn_pre_g, attn_post_g, w_in, cmp_pe, cmp_w1, cmp_b1,
           cmp_w2, mla_q_norm_g, mla_w_q_up, mla_kv_norm_g, mla_w_kv_up, w_branch_nsa, w_branch_mla,
           w_out, ffn_pre_g, ffn_post_g, peer_w_q, peer_sub_keys, peer_u, peer_v):
    depth = ada_w.shape[0]
    for l in range(depth):
        x = _block(x, c, positions, ada_w[l], ada_b[l], attn_pre_g[l], attn_post_g[l], w_in[l],
                   cmp_pe[l], cmp_w1[l], cmp_b1[l], cmp_w2[l], mla_q_norm_g[l], mla_w_q_up[l],
                   mla_kv_norm_g[l], mla_w_kv_up[l], w_branch_nsa[l], w_branch_mla[l], w_out[l],
                   ffn_pre_g[l], ffn_post_g[l], peer_w_q[l], peer_sub_keys[l], peer_u[l], peer_v[l])
    return x
```

```python
import functools

import numpy as np
import jax
import jax.numpy as jnp
from jax import lax
from jax.experimental import pallas as pl
from jax.experimental.pallas import tpu as pltpu

F32 = jnp.float32
BF16 = jnp.bfloat16
I32 = jnp.int32

LANES = 128
SUBLANES = 8

NSA_HEADS = 16
NSA_GROUPS = 4
NSA_R = NSA_HEADS // NSA_GROUPS
NSA_HEAD_DIM = 64
NSA_ROPE_DIM = NSA_HEAD_DIM // 4
CMP_BLOCK = 32
CMP_STRIDE = 16
CMP_HIDDEN = 256
SEL_BLOCK = 64
SEL_SHIFT = 6
SEL_TOPK = 16
WINDOW = 512
MLA_HEADS = 16
MLA_NOPE_DIM = 64
MLA_ROPE_DIM = 32
MLA_V_DIM = 64
PEER_HEADS = 8
PEER_N_KEYS = 128
PEER_TOPK = 16
ROPE_THETA = 500000.0
NORM_EPS = 1e-6
NEG = -1e30
FORCE_BONUS = 1e4
LOG2E = 1.4426950408889634
MLA_ROPE_OFF = MLA_NOPE_DIM


def _cparams(sem, vmem_mb=None):
    kw = dict(dimension_semantics=sem)
    if vmem_mb is not None:
        kw["vmem_limit_bytes"] = vmem_mb * 1024 * 1024
    return pltpu.CompilerParams(**kw)


def _rms(x, g):
    return x * lax.rsqrt(jnp.mean(x * x, axis=-1, keepdims=True) + NORM_EPS) * g


def _gelu(x):
    return jax.nn.gelu(x, approximate=True)


def _adaln_kernel(c_ref, w_ref, b_ref, o_ref):
    c = c_ref[...]
    s = (c * jax.nn.sigmoid(c)).astype(BF16)
    o_ref[...] = jnp.dot(s, w_ref[...].astype(BF16), preferred_element_type=F32) + b_ref[...]


def _adaln(c, w, b, tn=1024):
    bsz, d = c.shape
    n = w.shape[1]
    cp = jnp.pad(c, ((0, SUBLANES - bsz), (0, 0)))
    out = pl.pallas_call(
        _adaln_kernel,
        grid=(n // tn,),
        in_specs=[pl.BlockSpec((SUBLANES, d), lambda j: (0, 0)),
                  pl.BlockSpec((d, tn), lambda j: (0, j)),
                  pl.BlockSpec((1, tn), lambda j: (0, j))],
        out_specs=pl.BlockSpec((SUBLANES, tn), lambda j: (0, j)),
        out_shape=jax.ShapeDtypeStruct((SUBLANES, n), F32),
        compiler_params=_cparams(("arbitrary",), 40),
        name="adaln",
    )(cp, w, b.reshape(1, n))
    return out[:bsz]


def _rope_table_kernel(pos_ref, inv_ref, sgn_ref, o_ref):
    pos = pos_ref[...]
    for k in range(2):
        ang = pos * inv_ref[k:k + 1, :]
        o_ref[:, (2 * k) * LANES:(2 * k + 1) * LANES] = jnp.cos(ang)
        o_ref[:, (2 * k + 1) * LANES:(2 * k + 2) * LANES] = jnp.sin(ang) * sgn_ref[k:k + 1, :]


def _rope_tables(positions, tm=512):
    n = positions.size
    half_a = NSA_ROPE_DIM // 2
    half_b = MLA_ROPE_DIM // 2
    inv_a = ROPE_THETA ** (-jnp.arange(half_a, dtype=F32) * (2.0 / NSA_ROPE_DIM))
    inv_b = ROPE_THETA ** (-jnp.arange(half_b, dtype=F32) * (2.0 / MLA_ROPE_DIM))
    inv = jnp.zeros((2, LANES), F32)
    inv = inv.at[0, 0:half_a].set(inv_a).at[0, half_a:2 * half_a].set(inv_a)
    o = MLA_ROPE_OFF
    inv = inv.at[1, o:o + half_b].set(inv_b).at[1, o + half_b:o + 2 * half_b].set(inv_b)
    sgn = np.zeros((2, LANES), np.float32)
    sgn[0, 0:half_a] = -1.0
    sgn[0, half_a:2 * half_a] = 1.0
    sgn[1, o:o + half_b] = -1.0
    sgn[1, o + half_b:o + 2 * half_b] = 1.0
    posf = jnp.broadcast_to(positions.reshape(n, 1).astype(F32), (n, LANES))
    return pl.pallas_call(
        _rope_table_kernel,
        grid=(n // tm,),
        in_specs=[pl.BlockSpec((tm, LANES), lambda i: (i, 0)),
                  pl.BlockSpec((2, LANES), lambda i: (0, 0)),
                  pl.BlockSpec((2, LANES), lambda i: (0, 0))],
        out_specs=pl.BlockSpec((tm, 4 * LANES), lambda i: (i, 0)),
        out_shape=jax.ShapeDtypeStruct((n, 4 * LANES), F32),
        compiler_params=_cparams(("arbitrary",)),
        name="rope_tables",
    )(posf, inv, jnp.asarray(sgn))


def _rope_slot(a, cos, sin, half, off):
    lane = lax.broadcasted_iota(I32, (1, LANES), 1)
    first = (lane - off) < half
    partner = jnp.where(first, pltpu.roll(a, LANES - half, 1), pltpu.roll(a, half, 1))
    return a * cos + partner * sin


def _normmod_kernel(x_ref, g_ref, sc_ref, sh_ref, o_ref):
    y = _rms(x_ref[...], g_ref[...])
    o_ref[...] = (y * (1.0 + sc_ref[0]) + sh_ref[0]).astype(o_ref.dtype)


def _normmod(x, g, sc, sh, seq, tm=512):
    n, d = x.shape
    bsz = sc.shape[0]
    bspec = pl.BlockSpec((1, 1, d), lambda i: ((i * tm) // seq, 0, 0))
    return pl.pallas_call(
        _normmod_kernel,
        grid=(n // tm,),
        in_specs=[pl.BlockSpec((tm, d), lambda i: (i, 0)),
                  pl.BlockSpec((1, d), lambda i: (0, 0)), bspec, bspec],
        out_specs=pl.BlockSpec((tm, d), lambda i: (i, 0)),
        out_shape=jax.ShapeDtypeStruct((n, d), BF16),
        compiler_params=_cparams(("arbitrary",)),
        name="normmod",
    )(x, g.reshape(1, d), sc.reshape(bsz, 1, d), sh.reshape(bsz, 1, d))


def _norm_res_kernel(x_ref, y_ref, g_ref, ga_ref, o_ref):
    o_ref[...] = x_ref[...] + ga_ref[0] * _rms(y_ref[...], g_ref[...])


def _norm_res(x, y, g, gate, seq, tm=512):
    n, d = x.shape
    bsz = gate.shape[0]
    row = pl.BlockSpec((tm, d), lambda i: (i, 0))
    return pl.pallas_call(
        _norm_res_kernel,
        grid=(n // tm,),
        in_specs=[row, row, pl.BlockSpec((1, d), lambda i: (0, 0)),
                  pl.BlockSpec((1, 1, d), lambda i: ((i * tm) // seq, 0, 0))],
        out_specs=row,
        out_shape=jax.ShapeDtypeStruct((n, d), F32),
        compiler_params=_cparams(("arbitrary",)),
        name="norm_res",
    )(x, y, g.reshape(1, d), gate.reshape(bsz, 1, d))


def _mm(x, w, epi, *, tm, tn, out_dtype, name, extras=(), pro=None, x_spec=None, vmem_mb=None):
    m = x.shape[0]
    k, nc = w.shape
    if x_spec is None:
        x_spec = pl.BlockSpec((tm, k), lambda i, j: (i, 0))
    in_specs = [x_spec, pl.BlockSpec((k, tn), lambda i, j: (0, j))]
    args = [x, w]
    for arr, spec in extras:
        in_specs.append(spec)
        args.append(arr)

    def kern(x_ref, w_ref, *rest):
        o_ref = rest[-1]
        ex = rest[:-1]
        xv = x_ref[...]
        if pro is not None:
            xv = pro(xv, *ex)
        acc = jnp.dot(xv, w_ref[...], preferred_element_type=F32)
        epi(acc, o_ref, *ex)

    return pl.pallas_call(
        kern,
        grid=(m // tm, nc // tn),
        in_specs=in_specs,
        out_specs=pl.BlockSpec((tm, tn), lambda i, j: (i, j)),
        out_shape=jax.ShapeDtypeStruct((m, nc), out_dtype),
        compiler_params=_cparams(("arbitrary", "arbitrary"), vmem_mb),
        name=name,
    )(*args)


def _epi_plain(acc, o_ref, *ex):
    o_ref[...] = acc.astype(o_ref.dtype)


def _epi_sigmoid(acc, o_ref, *ex):
    o_ref[...] = jax.nn.sigmoid(acc).astype(o_ref.dtype)


def _make_epi_rope_a(scale, seq=None, tm=None, onehot=False):
    half = NSA_ROPE_DIM // 2

    def epi(acc, o_ref, tab_ref):
        cos = tab_ref[:, 0:LANES]
        sin = tab_ref[:, LANES:2 * LANES]
        if onehot:
            base = lax.rem(pl.program_id(0) * tm, seq)
            t = base + lax.broadcasted_iota(I32, (acc.shape[0], LANES), 0)
            lane = lax.broadcasted_iota(I32, (acc.shape[0], LANES), 1)
            hot = jnp.where(lane - SEL_BLOCK == jnp.right_shift(t, SEL_SHIFT), 1.0, 0.0)
        for s in range(acc.shape[1] // LANES):
            r = _rope_slot(acc[:, s * LANES:(s + 1) * LANES], cos, sin, half, 0)
            if scale != 1.0:
                r = r * scale
            if onehot:
                r = r + hot
            o_ref[:, s * LANES:(s + 1) * LANES] = r.astype(o_ref.dtype)
    return epi


def _epi_rope_b_scaled(acc, o_ref, g_ref, tab_ref):
    cos = tab_ref[:, 2 * LANES:3 * LANES]
    sin = tab_ref[:, 3 * LANES:4 * LANES]
    scale = (MLA_NOPE_DIM + MLA_ROPE_DIM) ** -0.5 * LOG2E
    for s in range(acc.shape[1] // LANES):
        r = _rope_slot(acc[:, s * LANES:(s + 1) * LANES], cos, sin, MLA_ROPE_DIM // 2, MLA_ROPE_OFF)
        o_ref[:, s * LANES:(s + 1) * LANES] = (r * scale).astype(o_ref.dtype)


def _epi_gate_kr(acc, o_ref, tab_ref):
    o_ref[:, 0:LANES] = jax.nn.sigmoid(acc[:, 0:LANES]).astype(o_ref.dtype)
    cos = tab_ref[:, 2 * LANES:3 * LANES]
    sin = tab_ref[:, 3 * LANES:4 * LANES]
    r = _rope_slot(acc[:, LANES:2 * LANES], cos, sin, MLA_ROPE_DIM // 2, MLA_ROPE_OFF)
    o_ref[:, LANES:2 * LANES] = r.astype(o_ref.dtype)


def _pro_rms(xv, g_ref, *ex):
    return _rms(xv, g_ref[...]).astype(BF16)


def _epi_add_kr(acc, o_ref, g_ref, kr_ref):
    kr = kr_ref[...].astype(F32)
    for s in range(acc.shape[1] // LANES):
        o_ref[:, s * LANES:(s + 1) * LANES] = (acc[:, s * LANES:(s + 1) * LANES] + kr).astype(o_ref.dtype)


def _epi_plain_g(acc, o_ref, g_ref):
    o_ref[...] = acc.astype(o_ref.dtype)


def _compress_kernel(k16_ref, v16_ref, w1_ref, pe_ref, b1_ref, w2_ref, kc_ref, vc_ref):
    half = w1_ref.shape[1] // 2
    for which, (src, dst) in enumerate(((k16_ref, kc_ref), (v16_ref, vc_ref))):
        xb = src[...]
        a = jnp.dot(xb, w1_ref[which, 0:half, :], preferred_element_type=F32)
        b = jnp.dot(xb, w1_ref[which, half:2 * half, :], preferred_element_type=F32)
        c = jnp.dot(pe_ref[which], w1_ref[which], preferred_element_type=F32)[0:1, :] + b1_ref[which]
        rows = a.shape[0]
        pre = a + pltpu.roll(b, rows - 1, 0) + c
        hid = _gelu(pre).astype(BF16)
        dst[...] = jnp.dot(hid, w2_ref[which], preferred_element_type=F32).astype(dst.dtype)


def _compress(k16, v16, w1, pe, b1, w2p):
    bg, rows, feat = k16.shape
    blk = pl.BlockSpec((None, rows, feat), lambda i: (i, 0, 0))
    full = lambda a: pl.BlockSpec(a.shape, lambda i: (0,) * a.ndim)
    out = pl.BlockSpec((None, rows, LANES), lambda i: (i, 0, 0))
    return pl.pallas_call(
        _compress_kernel,
        grid=(bg,),
        in_specs=[blk, blk, full(w1), full(pe), full(b1), full(w2p)],
        out_specs=[out, out],
        out_shape=[jax.ShapeDtypeStruct((bg, rows, LANES), BF16)] * 2,
        compiler_params=_cparams(("arbitrary",)),
        name="nsa_compress",
    )(k16, v16, w1, pe, b1, w2p)


def _cmp_attn_kernel(q_ref, kc_ref, vc_ref, ovt_ref, o_ref, bias_ref, *, tq, n_sel):
    qi = pl.program_id(2)
    ncmp = kc_ref.shape[0]
    t_row = qi * tq + lax.broadcasted_iota(I32, (tq, ncmp), 0)
    n_col = lax.broadcasted_iota(I32, (tq, ncmp), 1)
    cmask = (n_col * CMP_STRIDE + (CMP_BLOCK - 1)) <= t_row
    kc = kc_ref[...]
    vc = vc_ref[...]
    imp_t = jnp.zeros((n_sel, tq), F32)
    for r in range(NSA_R):
        q = q_ref[:, r * LANES:(r + 1) * LANES]
        s = lax.dot_general(q, kc, (((1,), (1,)), ((), ())), preferred_element_type=F32)
        s = jnp.where(cmask, s, NEG)
        e = jnp.exp2(s - jnp.max(s, axis=1, keepdims=True))
        p = e / jnp.sum(e, axis=1, keepdims=True)
        p = jnp.where(cmask, p, 0.0).astype(BF16)
        o_ref[:, r * LANES:(r + 1) * LANES] = jnp.dot(p, vc, preferred_element_type=F32).astype(o_ref.dtype)
        imp_t = imp_t + lax.dot_general(ovt_ref[...], p, (((1,), (1,)), ((), ())),
                                        preferred_element_type=F32)
    blk = lax.broadcasted_iota(I32, (n_sel, tq), 0)
    t = qi * tq + lax.broadcasted_iota(I32, (n_sel, tq), 1)
    tb = jnp.right_shift(t, SEL_SHIFT)
    forced = (blk == 0) | (blk == tb) | (blk == tb - 1)
    valid = blk * SEL_BLOCK <= t
    x = jnp.where(valid, imp_t + jnp.where(forced, FORCE_BONUS, 0.0), NEG)
    sel = jnp.zeros((n_sel, tq), F32)
    for _ in range(min(SEL_TOPK, n_sel)):
        m = jnp.max(x, axis=0, keepdims=True)
        idx = jnp.min(jnp.where(x == m, blk, n_sel), axis=0, keepdims=True)
        hit = blk == idx
        sel = jnp.where(hit, 1.0, sel)
        x = jnp.where(hit, -jnp.inf, x)
    bias_t = jnp.where(sel > 0.5, 0.0, NEG)
    parts = [jnp.zeros((SEL_BLOCK, tq), F32), bias_t]
    if LANES - SEL_BLOCK - n_sel > 0:
        parts.append(jnp.zeros((LANES - SEL_BLOCK - n_sel, tq), F32))
    full_t = jnp.concatenate(parts, axis=0)
    bias_ref[...] = full_t.T.astype(bias_ref.dtype)


def _cmp_attn(qn, kc, vc, ovt, bsz, seq, tq=256):
    n = qn.shape[0]
    nq = seq // tq
    n_sel = seq // SEL_BLOCK
    ncmp = kc.shape[1]
    qspec = pl.BlockSpec((tq, NSA_R * LANES), lambda b, g, qi: (b * nq + qi, g))
    kspec = pl.BlockSpec((None, ncmp, LANES), lambda b, g, qi: (b * NSA_GROUPS + g, 0, 0))
    return pl.pallas_call(
        functools.partial(_cmp_attn_kernel, tq=tq, n_sel=n_sel),
        grid=(bsz, NSA_GROUPS, nq),
        in_specs=[qspec, kspec, kspec, pl.BlockSpec(ovt.shape, lambda b, g, qi: (0, 0))],
        out_specs=[qspec, pl.BlockSpec((tq, LANES), lambda b, g, qi: (b * nq + qi, g))],
        out_shape=[jax.ShapeDtypeStruct((n, NSA_HEADS * LANES), BF16),
                   jax.ShapeDtypeStruct((n, NSA_GROUPS * LANES), BF16)],
        compiler_params=_cparams(("arbitrary",) * 3),
        name="nsa_cmp_attn",
    )(qn, kc, vc, ovt)


def _flash_kernel(*refs, mode, tq, tk, nr, window):
    if mode == "sel":
        q_ref, k_ref, v_ref, bias_ref, o_ref, qs, m_s, l_s, acc_s = refs
    else:
        q_ref, k_ref, v_ref, o_ref, qs, m_s, l_s, acc_s = refs
    qi = pl.program_id(2)
    kk = pl.program_id(3)
    nk = pl.num_programs(3)
    rows = nr * tq

    @pl.when(kk == 0)
    def _init():
        for r in range(nr):
            qr = q_ref[:, r * LANES:(r + 1) * LANES]
            if mode == "sel":
                qr = qr + bias_ref[...]
            qs[r * tq:(r + 1) * tq, :] = qr
        m_s[...] = jnp.full(m_s.shape, -jnp.inf, F32)
        l_s[...] = jnp.zeros(l_s.shape, F32)
        acc_s[...] = jnp.zeros(acc_s.shape, F32)

    if mode == "win":
        kidx = qi * (tq // tk) - window // tk + kk
        needed = kidx >= 0
        full_vis = (kidx * tk + tk - 1 <= qi * tq) & (qi * tq + tq - 1 - kidx * tk < window)
    else:
        kidx = kk
        needed = kk * tk <= qi * tq + tq - 1
        full_vis = kk * tk + tk - 1 <= qi * tq

    nt = tk // LANES

    def update(masked):
        s = lax.dot_general(qs[...], k_ref[...], (((1,), (1,)), ((), ())), preferred_element_type=F32)
        if masked:
            rel = (lax.broadcasted_iota(I32, (rows, tk), 0) & (tq - 1)) - lax.broadcasted_iota(I32, (rows, tk), 1)
            off = kidx * tk - qi * tq
            vis = rel >= off
            if mode == "win":
                vis = vis & (rel < off + window)
            s = jnp.where(vis, s, NEG)
        m_prev = m_s[...]
        m_new = jnp.maximum(m_prev, jnp.max(s, axis=1, keepdims=True))
        p = jnp.exp2(s - jnp.concatenate([m_new] * nt, axis=1))
        alpha = jnp.exp2(m_prev - m_new)
        psum = p[:, 0:LANES]
        for c in range(1, nt):
            psum = psum + p[:, c * LANES:(c + 1) * LANES]
        l_s[...] = alpha * l_s[...] + psum
        acc_s[...] = alpha * acc_s[...] + jnp.dot(p.astype(BF16), v_ref[...], preferred_element_type=F32)
        m_s[...] = m_new

    @pl.when(needed & full_vis)
    def _full():
        update(False)

    @pl.when(needed & jnp.logical_not(full_vis))
    def _edge():
        update(True)

    @pl.when(kk == nk - 1)
    def _fin():
        out = acc_s[...] / jnp.sum(l_s[...], axis=1, keepdims=True)
        for r in range(nr):
            o_ref[:, r * LANES:(r + 1) * LANES] = out[r * tq:(r + 1) * tq, :].astype(o_ref.dtype)


def _flash(q, k, v, *, mode, bsz, seq, nslots, nr, tq, tk, k_off=0, v_off=0, bias=None,
           name):
    assert tq & (tq - 1) == 0
    n = q.shape[0]
    nq = seq // tq
    nkb = seq // tk
    if mode == "win":
        assert tq % tk == 0 and WINDOW % tk == 0
        steps = WINDOW // tk + tq // tk

        def krow(b, qi, kk):
            return b * nkb + jnp.maximum(qi * (tq // tk) - WINDOW // tk + kk, 0)
    else:
        steps = nkb

        def krow(b, qi, kk):
            return b * nkb + jnp.minimum(kk, (qi * tq + tq - 1) // tk)

    qspec = pl.BlockSpec((tq, nr * LANES), lambda b, g, qi, kk: (b * nq + qi, g))
    in_specs = [qspec,
                pl.BlockSpec((tk, LANES), lambda b, g, qi, kk: (krow(b, qi, kk), k_off + g)),
                pl.BlockSpec((tk, LANES), lambda b, g, qi, kk: (krow(b, qi, kk), v_off + g))]
    args = [q, k, v]
    if mode == "sel":
        in_specs.append(pl.BlockSpec((tq, LANES), lambda b, g, qi, kk: (b * nq + qi, g)))
        args.append(bias)
    rows = nr * tq
    return pl.pallas_call(
        functools.partial(_flash_kernel, mode=mode, tq=tq, tk=tk, nr=nr, window=WINDOW),
        grid=(bsz, nslots, nq, steps),
        in_specs=in_specs,
        out_specs=qspec,
        out_shape=jax.ShapeDtypeStruct((n, nslots * nr * LANES), BF16),
        scratch_shapes=[pltpu.VMEM((rows, LANES), BF16), pltpu.VMEM((rows, LANES), F32),
                        pltpu.VMEM((rows, LANES), F32), pltpu.VMEM((rows, LANES), F32)],
        compiler_params=_cparams(("arbitrary",) * 4, 48),
        name=name,
    )(*args)


def _combine_kernel(oc_ref, os_ref, ow_ref, g_ref, e_ref, o_ref):
    g = g_ref[...]
    acc = None
    for br, src in enumerate((oc_ref, os_ref, ow_ref)):
        ge = jnp.dot(g, e_ref[br], preferred_element_type=F32)
        term = ge * src[...].astype(F32)
        acc = term if acc is None else acc + term
    o_ref[...] = acc.astype(o_ref.dtype)


def _combine(o_cmp, o_sel, o_win, gates, gates_col, tm=512):
    n, w = o_cmp.shape
    e = np.zeros((3, LANES, w), np.float32)
    for h in range(NSA_HEADS):
        for br in range(3):
            e[br, h * 3 + br, h * LANES:h * LANES + NSA_HEAD_DIM] = 1.0
    row = pl.BlockSpec((tm, w), lambda i: (i, 0))
    return pl.pallas_call(
        _combine_kernel,
        grid=(n // tm,),
        in_specs=[row, row, row, pl.BlockSpec((tm, LANES), lambda i: (i, gates_col)),
                  pl.BlockSpec(e.shape, lambda i: (0, 0, 0))],
        out_specs=row,
        out_shape=jax.ShapeDtypeStruct((n, w), BF16),
        compiler_params=_cparams(("arbitrary",)),
        name="nsa_combine",
    )(o_cmp, o_sel, o_win, gates, jnp.asarray(e, BF16))


def _merge_kernel(a_ref, wa_ref, b_ref, wb_ref, g0_ref, g1_ref, o_ref):
    ya = jnp.dot(a_ref[...], wa_ref[...], preferred_element_type=F32)
    yb = jnp.dot(b_ref[...], wb_ref[...], preferred_element_type=F32)
    o_ref[...] = (g0_ref[...].astype(F32) * ya + g1_ref[...].astype(F32) * yb).astype(o_ref.dtype)


def _merge(o_nsa, w_nsa, o_mla, w_mla, gm, tm=512, tn=512):
    n, k = o_nsa.shape
    d = w_nsa.shape[1]
    nj = d // tn
    row = pl.BlockSpec((tm, k), lambda i, j: (i, 0))
    wsp = pl.BlockSpec((k, tn), lambda i, j: (0, j))
    return pl.pallas_call(
        _merge_kernel,
        grid=(n // tm, nj),
        in_specs=[row, wsp, row, wsp,
                  pl.BlockSpec((tm, tn), lambda i, j: (i, j)),
                  pl.BlockSpec((tm, tn), lambda i, j: (i, nj + j))],
        out_specs=pl.BlockSpec((tm, tn), lambda i, j: (i, j)),
        out_shape=jax.ShapeDtypeStruct((n, d), BF16),
        compiler_params=_cparams(("arbitrary", "arbitrary")),
        name="branch_merge",
    )(o_nsa, w_nsa, o_mla, w_mla, gm, gm)


def _extract_sorted(x, nrounds, out_rows):
    iota = lax.broadcasted_iota(I32, x.shape, 0)
    orow = lax.broadcasted_iota(I32, (out_rows, x.shape[1]), 0)
    vals = jnp.full((out_rows, x.shape[1]), -jnp.inf, F32)
    m = None
    for k in range(nrounds):
        m = jnp.max(x, axis=0, keepdims=True)
        idx = jnp.min(jnp.where(x == m, iota, x.shape[0]), axis=0, keepdims=True)
        x = jnp.where(iota == idx, -jnp.inf, x)
        vals = jnp.where(orow == k, m, vals)
    return vals


def _peer_route_kernel(q_ref, sk_ref, st_ref, et_ref, tau_ref):
    nk = PEER_N_KEYS
    tr = q_ref.shape[0]
    for h in range(PEER_HEADS):
        s = []
        for p in range(2):
            c = (2 * h + p) * nk
            s.append(lax.dot_general(sk_ref[2 * h + p], q_ref[:, c:c + nk], (((1,), (1,)), ((), ())),
                                     preferred_element_type=F32))
        s1, s2 = s
        v1 = _extract_sorted(s1, PEER_TOPK, PEER_TOPK)
        v2 = _extract_sorted(s2, PEER_TOPK, PEER_TOPK)
        cands = []
        for i in range(PEER_TOPK):
            cnt = PEER_TOPK // (i + 1)
            rows = PEER_TOPK if i == 0 else SUBLANES
            c = v1[i:i + 1, :] + v2[0:rows, :]
            if cnt < rows:
                c = jnp.where(lax.broadcasted_iota(I32, (rows, tr), 0) < cnt, c, -jnp.inf)
            cands.append(c)
        top = _extract_sorted(jnp.concatenate(cands, axis=0), PEER_TOPK, PEER_TOPK)
        tau = top[PEER_TOPK - 1:PEER_TOPK, :]
        z = jnp.sum(jnp.exp(top - top[0:1, :]), axis=0, keepdims=True)
        st_ref[(2 * h) * nk:(2 * h + 1) * nk, :] = s1
        st_ref[(2 * h + 1) * nk:(2 * h + 2) * nk, :] = s2
        et_ref[(2 * h) * nk:(2 * h + 1) * nk, :] = jnp.exp(s1 - v1[0:1, :]) / z
        et_ref[(2 * h + 1) * nk:(2 * h + 2) * nk, :] = jnp.exp(s2 - v2[0:1, :])
        tau_ref[h:h + 1, :] = tau


def _peer_route(qry, sk, tr=256):
    n, w = qry.shape
    col = pl.BlockSpec((w, tr), lambda i: (0, i))
    return pl.pallas_call(
        _peer_route_kernel,
        grid=(n // tr,),
        in_specs=[pl.BlockSpec((tr, w), lambda i: (i, 0)),
                  pl.BlockSpec(sk.shape, lambda i: (0, 0, 0))],
        out_specs=[col, col, pl.BlockSpec((PEER_HEADS, tr), lambda i: (0, i))],
        out_shape=[jax.ShapeDtypeStruct((w, n), F32), jax.ShapeDtypeStruct((w, n), F32),
                   jax.ShapeDtypeStruct((PEER_HEADS, n), F32)],
        compiler_params=_cparams(("arbitrary",), 40),
        name="peer_route",
    )(qry, sk)


def _peer_dense_kernel(h_ref, u_ref, vt_ref, st_ref, et_ref, tau_ref, o_ref, acc_s, pt_s, *, te):
    j = pl.program_id(1)
    nk = PEER_N_KEYS

    @pl.when(j == 0)
    def _init():
        acc_s[...] = jnp.zeros(acc_s.shape, F32)

    zt = lax.dot_general(u_ref[...], h_ref[...], (((1,), (1,)), ((), ())), preferred_element_type=F32)
    for al in range(te // nk):
        a = j * (te // nk) + al
        g = None
        for h in range(PEER_HEADS):
            s1 = st_ref[pl.ds(2 * h * nk + a, 1), :]
            e1 = et_ref[pl.ds(2 * h * nk + a, 1), :]
            s2 = st_ref[(2 * h + 1) * nk:(2 * h + 2) * nk, :]
            e2 = et_ref[(2 * h + 1) * nk:(2 * h + 2) * nk, :]
            term = jnp.where((s1 + s2) >= tau_ref[h:h + 1, :], e1 * e2, 0.0)
            g = term if g is None else g + term
        pt_s[al * nk:(al + 1) * nk, :] = (g * _gelu(zt[al * nk:(al + 1) * nk, :])).astype(BF16)
    acc_s[...] += jnp.dot(vt_ref[...], pt_s[...], preferred_element_type=F32)

    @pl.when(j == pl.num_programs(1) - 1)
    def _fin():
        o_ref[...] = acc_s[...].T


def _peer_dense(h2, u_bf, vt_bf, st, et, tau, tm=512, te=512):
    n, d = h2.shape
    ne = u_bf.shape[0]
    w = st.shape[0]
    return pl.pallas_call(
        functools.partial(_peer_dense_kernel, te=te),
        grid=(n // tm, ne // te),
        in_specs=[pl.BlockSpec((tm, d), lambda i, j: (i, 0)),
                  pl.BlockSpec((te, d), lambda i, j: (j, 0)),
                  pl.BlockSpec((d, te), lambda i, j: (0, j)),
                  pl.BlockSpec((w, tm), lambda i, j: (0, i)),
                  pl.BlockSpec((w, tm), lambda i, j: (0, i)),
                  pl.BlockSpec((PEER_HEADS, tm), lambda i, j: (0, i))],
        out_specs=pl.BlockSpec((tm, d), lambda i, j: (i, 0)),
        out_shape=jax.ShapeDtypeStruct((n, d), F32),
        scratch_shapes=[pltpu.VMEM((d, tm), F32), pltpu.VMEM((te, tm), BF16)],
        compiler_params=_cparams(("arbitrary", "arbitrary"), 56),
        name="peer_dense",
    )(h2, u_bf, vt_bf, st, et, tau)


def _slots(w, n, width, off=0):
    k = w.shape[0]
    w = w.reshape(k, n, width)
    w = jnp.pad(w, ((0, 0), (0, 0), (off, LANES - width - off)))
    return w.reshape(k, n * LANES)


def _slot_rows(w, n, width):
    d = w.shape[1]
    w = w.reshape(n, width, d)
    w = jnp.pad(w, ((0, 0), (0, LANES - width), (0, 0)))
    return w.reshape(n * LANES, d)


def _overlap_t(seq):
    n_cmp_rows = seq // CMP_STRIDE
    n_sel = seq // SEL_BLOCK
    cs = np.arange(n_cmp_rows) * CMP_STRIDE
    ce = cs + CMP_BLOCK - 1
    ss = np.arange(n_sel) * SEL_BLOCK
    ov = (cs[None, :] < ss[:, None] + SEL_BLOCK) & (ce[None, :] >= ss[:, None])
    ov[:, n_cmp_rows - 1] = False
    return jnp.asarray(ov.astype(np.float32), BF16)


def _token_mixer(h, positions, bsz, seq, w_in, cmp_pe, cmp_w1, cmp_b1, cmp_w2, mla_q_norm_g,
                 mla_w_q_up, mla_kv_norm_g, mla_w_kv_up, w_branch_nsa, w_branch_mla, w_out):
    d = h.shape[1]
    g_, hd = NSA_GROUPS, NSA_HEAD_DIM
    tab = _rope_tables(positions)
    tab_spec = pl.BlockSpec((512, 4 * LANES), lambda i, j: (i, 0))

    wb = w_in.astype(BF16)
    o1 = NSA_HEADS * hd
    kv = wb[:, o1:o1 + 3 * 2 * g_ * hd].reshape(d, 3, 2, g_ * hd)
    o2 = o1 + 3 * 2 * g_ * hd
    o3 = o2 + 3 * NSA_HEADS
    mq = mla_w_q_up.shape[0]
    mkv = mla_w_kv_up.shape[0]
    o4 = o3 + mq
    o5 = o4 + mkv
    o6 = o5 + MLA_ROPE_DIM
    w_q = _slots(wb[:, :o1], NSA_HEADS, hd)
    w_kk = jnp.concatenate([_slots(kv[:, 0, 0], g_, hd), _slots(kv[:, 2, 0], g_, hd)], axis=1)
    w_ks = _slots(kv[:, 1, 0], g_, hd)
    w_vv = jnp.concatenate([_slots(kv[:, 1, 1], g_, hd), _slots(kv[:, 2, 1], g_, hd),
                            kv[:, 0, 1], jnp.zeros((d, g_ * hd), BF16)], axis=1)
    w_c = wb[:, o3:o5]
    w_gk = jnp.concatenate([jnp.pad(wb[:, o2:o3], ((0, 0), (0, LANES - 3 * NSA_HEADS))),
                            _slots(wb[:, o5:o6], 1, MLA_ROPE_DIM, MLA_ROPE_OFF)], axis=1)
    w_gm = wb[:, o6:]

    tm = 512
    qn = _mm(h, w_q, _make_epi_rope_a(hd ** -0.5 * LOG2E), tm=tm, tn=512, out_dtype=BF16, name="proj_q",
             extras=[(tab, tab_spec)])
    kk = _mm(h, w_kk, _make_epi_rope_a(1.0), tm=tm, tn=512, out_dtype=BF16, name="proj_k",
             extras=[(tab, tab_spec)])
    ks = _mm(h, w_ks, _make_epi_rope_a(1.0, seq, tm, True), tm=tm, tn=512, out_dtype=BF16,
             name="proj_ksel", extras=[(tab, tab_spec)])
    vv = _mm(h, w_vv, _epi_plain, tm=tm, tn=512, out_dtype=BF16, name="proj_v")
    cqkv = _mm(h, w_c, _epi_plain, tm=tm, tn=mq + mkv, out_dtype=F32, name="proj_c")
    gk = _mm(h, w_gk, _epi_gate_kr, tm=tm, tn=2 * LANES, out_dtype=BF16, name="proj_gate_kr",
             extras=[(tab, tab_spec)])
    gm = _mm(h, w_gm, _epi_sigmoid, tm=tm, tn=512, out_dtype=BF16, name="proj_gmerge")

    def to16(a, width):
        a = a.reshape(bsz, seq, g_, width)[..., :hd]
        a = jnp.transpose(a, (0, 2, 1, 3))
        return a.reshape(bsz * g_, seq // CMP_STRIDE, CMP_STRIDE * hd)

    k16 = to16(kk[:, :g_ * LANES], LANES)
    v16 = to16(vv[:, 2 * g_ * LANES:2 * g_ * LANES + g_ * hd], hd)
    pe = jnp.pad(cmp_pe.reshape(2, 1, CMP_BLOCK * hd), ((0, 0), (0, SUBLANES - 1), (0, 0))).astype(BF16)
    w2p = jnp.pad(cmp_w2, ((0, 0), (0, 0), (0, LANES - hd))).astype(BF16)
    kc, vc = _compress(k16, v16, cmp_w1.astype(BF16), pe, cmp_b1.reshape(2, 1, CMP_HIDDEN), w2p)
    o_cmp, bias = _cmp_attn(qn, kc, vc, _overlap_t(seq), bsz, seq)
    o_sel = _flash(qn, ks, vv, mode="sel", bsz=bsz, seq=seq, nslots=g_, nr=NSA_R, tq=128, tk=512,
                   bias=bias, name="nsa_sel_attn")
    o_win = _flash(qn, kk, vv, mode="win", bsz=bsz, seq=seq, nslots=g_, nr=NSA_R, tq=256, tk=256,
                   k_off=g_, v_off=g_, name="nsa_win_attn")
    o_nsa = _combine(o_cmp, o_sel, o_win, gk, 0)

    qd = MLA_NOPE_DIM + MLA_ROPE_DIM
    w_qup = _slots(mla_w_q_up.astype(BF16), MLA_HEADS, qd)
    kvu = mla_w_kv_up.astype(BF16).reshape(mkv, MLA_HEADS, MLA_NOPE_DIM + MLA_V_DIM)
    w_kup = _slots(kvu[:, :, :MLA_NOPE_DIM].reshape(mkv, -1), MLA_HEADS, MLA_NOPE_DIM)
    w_vup = _slots(kvu[:, :, MLA_NOPE_DIM:].reshape(mkv, -1), MLA_HEADS, MLA_V_DIM)
    gq_spec = pl.BlockSpec((1, mq), lambda i, j: (0, 0))
    gkv_spec = pl.BlockSpec((1, mkv), lambda i, j: (0, 0))
    ckv_spec = pl.BlockSpec((tm, mkv), lambda i, j: (i, mq // mkv))
    q_mla = _mm(cqkv, w_qup, _epi_rope_b_scaled, tm=tm, tn=512,
                out_dtype=BF16, name="mla_q_up", pro=_pro_rms,
                x_spec=pl.BlockSpec((tm, mq), lambda i, j: (i, 0)),
                extras=[(mla_q_norm_g.reshape(1, mq), gq_spec), (tab, tab_spec)])
    k_mla = _mm(cqkv, w_kup, _epi_add_kr, tm=tm, tn=512, out_dtype=BF16, name="mla_k_up",
                pro=_pro_rms, x_spec=ckv_spec,
                extras=[(mla_kv_norm_g.reshape(1, mkv), gkv_spec),
                        (gk, pl.BlockSpec((tm, LANES), lambda i, j: (i, 1)))])
    v_mla = _mm(cqkv, w_vup, _epi_plain_g, tm=tm, tn=512, out_dtype=BF16, name="mla_v_up",
                pro=_pro_rms, x_spec=ckv_spec,
                extras=[(mla_kv_norm_g.reshape(1, mkv), gkv_spec)])
    o_mla = _flash(q_mla, k_mla, v_mla, mode="causal", bsz=bsz, seq=seq, nslots=MLA_HEADS, nr=1,
                   tq=512, tk=512, name="mla_attn")

    merged = _merge(o_nsa, _slot_rows(w_branch_nsa.astype(BF16), NSA_HEADS, hd),
                    o_mla, _slot_rows(w_branch_mla.astype(BF16), MLA_HEADS, MLA_V_DIM), gm)
    return _mm(merged, w_out.astype(BF16), _epi_plain, tm=tm, tn=512, out_dtype=F32, name="out_proj")


def _peer(h2, peer_w_q, peer_sub_keys, peer_u, peer_v):
    qry = _mm(h2, peer_w_q.astype(BF16), _epi_plain, tm=512, tn=512, out_dtype=BF16, name="peer_q")
    sk = peer_sub_keys.astype(BF16).reshape(2 * PEER_HEADS, PEER_N_KEYS, -1)
    st, et, tau = _peer_route(qry, sk)
    return _peer_dense(h2, peer_u.astype(BF16), jnp.transpose(peer_v).astype(BF16), st, et, tau)


def _block(x, c, positions, ada_w, ada_b, attn_pre_g, attn_post_g, w_in, cmp_pe, cmp_w1, cmp_b1,
           cmp_w2, mla_q_norm_g, mla_w_q_up, mla_kv_norm_g, mla_w_kv_up, w_branch_nsa, w_branch_mla,
           w_out, ffn_pre_g, ffn_post_g, peer_w_q, peer_sub_keys, peer_u, peer_v):
    bsz, seq, d = x.shape
    x2 = x.reshape(bsz * seq, d)
    mod = _adaln(c, ada_w, ada_b)
    sh_a, sc_a, g_a, sh_f, sc_f, g_f = [mod[:, i * d:(i + 1) * d] for i in range(6)]

    h = _normmod(x2, attn_pre_g, sc_a, sh_a, seq)
    y_attn = _token_mixer(h, positions, bsz, seq, w_in, cmp_pe, cmp_w1, cmp_b1, cmp_w2, mla_q_norm_g,
                          mla_w_q_up, mla_kv_norm_g, mla_w_kv_up, w_branch_nsa, w_branch_mla, w_out)
    x1 = _norm_res(x2, y_attn, attn_post_g, g_a, seq)

    h2 = _normmod(x1, ffn_pre_g, sc_f, sh_f, seq)
    y_ffn = _peer(h2, peer_w_q, peer_sub_keys, peer_u, peer_v)
    out = _norm_res(x1, y_ffn, ffn_post_g, g_f, seq)
    return out.reshape(bsz, seq, d)


def kernel(x, c, positions, ada_w, ada_b, attn_pre_g, attn_post_g, w_in, cmp_pe, cmp_w1, cmp_b1,
           cmp_w2, mla_q_norm_g, mla_w_q_up, mla_kv_norm_g, mla_w_kv_up, w_branch_nsa, w_branch_mla,
           w_out, ffn_pre_g, ffn_post_g, peer_w_q, peer_sub_keys, peer_u, peer_v):
    depth = ada_w.shape[0]
    for l in range(depth):
        x = _block(x, c, positions, ada_w[l], ada_b[l], attn_pre_g[l], attn_post_g[l], w_in[l],
                   cmp_pe[l], cmp_w1[l], cmp_b1[l], cmp_w2[l], mla_q_norm_g[l], mla_w_q_up[l],
                   mla_kv_norm_g[l], mla_w_kv_up[l], w_branch_nsa[l], w_branch_mla[l], w_out[l],
                   ffn_pre_g[l], ffn_post_g[l], peer_w_q[l], peer_sub_keys[l], peer_u[l], peer_v[l])
    return x
```

```python
import functools

import numpy as np
import jax
import jax.numpy as jnp
from jax import lax
from jax.experimental import pallas as pl
from jax.experimental.pallas import tpu as pltpu

F32 = jnp.float32
BF16 = jnp.bfloat16
I32 = jnp.int32

LANES = 128
SUBLANES = 8

NSA_HEADS = 16
NSA_GROUPS = 4
NSA_R = NSA_HEADS // NSA_GROUPS
NSA_HEAD_DIM = 64
NSA_ROPE_DIM = NSA_HEAD_DIM // 4
CMP_BLOCK = 32
CMP_STRIDE = 16
CMP_HIDDEN = 256
SEL_BLOCK = 64
SEL_SHIFT = 6
SEL_TOPK = 16
WINDOW = 512
MLA_HEADS = 16
MLA_NOPE_DIM = 64
MLA_ROPE_DIM = 32
MLA_V_DIM = 64
PEER_HEADS = 8
PEER_N_KEYS = 128
PEER_TOPK = 16
ROPE_THETA = 500000.0
NORM_EPS = 1e-6
NEG = -1e30
FORCE_BONUS = 1e4
LOG2E = 1.4426950408889634
MLA_ROPE_OFF = MLA_NOPE_DIM


def _cparams(sem, vmem_mb=None):
    kw = dict(dimension_semantics=sem)
    if vmem_mb is not None:
        kw["vmem_limit_bytes"] = vmem_mb * 1024 * 1024
    return pltpu.CompilerParams(**kw)


def _rms(x, g):
    return x * lax.rsqrt(jnp.mean(x * x, axis=-1, keepdims=True) + NORM_EPS) * g


def _gelu(x):
    return jax.nn.gelu(x, approximate=True)


def _adaln_kernel(c_ref, w_ref, b_ref, o_ref):
    c = c_ref[...]
    s = (c * jax.nn.sigmoid(c)).astype(BF16)
    o_ref[...] = jnp.dot(s, w_ref[...].astype(BF16), preferred_element_type=F32) + b_ref[...]


def _adaln(c, w, b, tn=1024):
    bsz, d = c.shape
    n = w.shape[1]
    cp = jnp.pad(c, ((0, SUBLANES - bsz), (0, 0)))
    out = pl.pallas_call(
        _adaln_kernel,
        grid=(n // tn,),
        in_specs=[pl.BlockSpec((SUBLANES, d), lambda j: (0, 0)),
                  pl.BlockSpec((d, tn), lambda j: (0, j)),
                  pl.BlockSpec((1, tn), lambda j: (0, j))],
        out_specs=pl.BlockSpec((SUBLANES, tn), lambda j: (0, j)),
        out_shape=jax.ShapeDtypeStruct((SUBLANES, n), F32),
        compiler_params=_cparams(("arbitrary",), 40),
        name="adaln",
    )(cp, w, b.reshape(1, n))
    return out[:bsz]


def _rope_table_kernel(pos_ref, inv_ref, sgn_ref, o_ref):
    pos = pos_ref[...]
    for k in range(2):
        ang = pos * inv_ref[k:k + 1, :]
        o_ref[:, (2 * k) * LANES:(2 * k + 1) * LANES] = jnp.cos(ang)
        o_ref[:, (2 * k + 1) * LANES:(2 * k + 2) * LANES] = jnp.sin(ang) * sgn_ref[k:k + 1, :]


def _rope_tables(positions, tm=512):
    n = positions.size
    half_a = NSA_ROPE_DIM // 2
    half_b = MLA_ROPE_DIM // 2
    inv_a = ROPE_THETA ** (-jnp.arange(half_a, dtype=F32) * (2.0 / NSA_ROPE_DIM))
    inv_b = ROPE_THETA ** (-jnp.arange(half_b, dtype=F32) * (2.0 / MLA_ROPE_DIM))
    inv = jnp.zeros((2, LANES), F32)
    inv = inv.at[0, 0:half_a].set(inv_a).at[0, half_a:2 * half_a].set(inv_a)
    o = MLA_ROPE_OFF
    inv = inv.at[1, o:o + half_b].set(inv_b).at[1, o + half_b:o + 2 * half_b].set(inv_b)
    sgn = np.zeros((2, LANES), np.float32)
    sgn[0, 0:half_a] = -1.0
    sgn[0, half_a:2 * half_a] = 1.0
    sgn[1, o:o + half_b] = -1.0
    sgn[1, o + half_b:o + 2 * half_b] = 1.0
    posf = jnp.broadcast_to(positions.reshape(n, 1).astype(F32), (n, LANES))
    return pl.pallas_call(
        _rope_table_kernel,
        grid=(n // tm,),
        in_specs=[pl.BlockSpec((tm, LANES), lambda i: (i, 0)),
                  pl.BlockSpec((2, LANES), lambda i: (0, 0)),
                  pl.BlockSpec((2, LANES), lambda i: (0, 0))],
        out_specs=pl.BlockSpec((tm, 4 * LANES), lambda i: (i, 0)),
        out_shape=jax.ShapeDtypeStruct((n, 4 * LANES), F32),
        compiler_params=_cparams(("arbitrary",)),
        name="rope_tables",
    )(posf, inv, jnp.asarray(sgn))


def _rope_slot(a, cos, sin, half, off):
    lane = lax.broadcasted_iota(I32, (1, LANES), 1)
    first = (lane - off) < half
    partner = jnp.where(first, pltpu.roll(a, LANES - half, 1), pltpu.roll(a, half, 1))
    return a * cos + partner * sin


def _normmod_kernel(x_ref, g_ref, sc_ref, sh_ref, o_ref):
    y = _rms(x_ref[...], g_ref[...])
    o_ref[...] = (y * (1.0 + sc_ref[0]) + sh_ref[0]).astype(o_ref.dtype)


def _normmod(x, g, sc, sh, seq, tm=512):
    n, d = x.shape
    bsz = sc.shape[0]
    bspec = pl.BlockSpec((1, 1, d), lambda i: ((i * tm) // seq, 0, 0))
    return pl.pallas_call(
        _normmod_kernel,
        grid=(n // tm,),
        in_specs=[pl.BlockSpec((tm, d), lambda i: (i, 0)),
                  pl.BlockSpec((1, d), lambda i: (0, 0)), bspec, bspec],
        out_specs=pl.BlockSpec((tm, d), lambda i: (i, 0)),
        out_shape=jax.ShapeDtypeStruct((n, d), BF16),
        compiler_params=_cparams(("arbitrary",)),
        name="normmod",
    )(x, g.reshape(1, d), sc.reshape(bsz, 1, d), sh.reshape(bsz, 1, d))


def _norm_res_kernel(x_ref, y_ref, g_ref, ga_ref, o_ref):
    o_ref[...] = x_ref[...] + ga_ref[0] * _rms(y_ref[...], g_ref[...])


def _norm_res(x, y, g, gate, seq, tm=512):
    n, d = x.shape
    bsz = gate.shape[0]
    row = pl.BlockSpec((tm, d), lambda i: (i, 0))
    return pl.pallas_call(
        _norm_res_kernel,
        grid=(n // tm,),
        in_specs=[row, row, pl.BlockSpec((1, d), lambda i: (0, 0)),
                  pl.BlockSpec((1, 1, d), lambda i: ((i * tm) // seq, 0, 0))],
        out_specs=row,
        out_shape=jax.ShapeDtypeStruct((n, d), F32),
        compiler_params=_cparams(("arbitrary",)),
        name="norm_res",
    )(x, y, g.reshape(1, d), gate.reshape(bsz, 1, d))


def _mm(x, w, epi, *, tm, tn, out_dtype, name, extras=(), pro=None, x_spec=None, vmem_mb=None):
    m = x.shape[0]
    k, nc = w.shape
    if x_spec is None:
        x_spec = pl.BlockSpec((tm, k), lambda i, j: (i, 0))
    in_specs = [x_spec, pl.BlockSpec((k, tn), lambda i, j: (0, j))]
    args = [x, w]
    for arr, spec in extras:
        in_specs.append(spec)
        args.append(arr)

    def kern(x_ref, w_ref, *rest):
        o_ref = rest[-1]
        ex = rest[:-1]
        xv = x_ref[...]
        if pro is not None:
            xv = pro(xv, *ex)
        acc = jnp.dot(xv, w_ref[...], preferred_element_type=F32)
        epi(acc, o_ref, *ex)

    return pl.pallas_call(
        kern,
        grid=(m // tm, nc // tn),
        in_specs=in_specs,
        out_specs=pl.BlockSpec((tm, tn), lambda i, j: (i, j)),
        out_shape=jax.ShapeDtypeStruct((m, nc), out_dtype),
        compiler_params=_cparams(("arbitrary", "arbitrary"), vmem_mb),
        name=name,
    )(*args)


def _epi_plain(acc, o_ref, *ex):
    o_ref[...] = acc.astype(o_ref.dtype)


def _epi_sigmoid(acc, o_ref, *ex):
    o_ref[...] = jax.nn.sigmoid(acc).astype(o_ref.dtype)


def _make_epi_rope_a(scale, seq=None, tm=None, onehot=False):
    half = NSA_ROPE_DIM // 2

    def epi(acc, o_ref, tab_ref):
        cos = tab_ref[:, 0:LANES]
        sin = tab_ref[:, LANES:2 * LANES]
        if onehot:
            base = lax.rem(pl.program_id(0) * tm, seq)
            t = base + lax.broadcasted_iota(I32, (acc.shape[0], LANES), 0)
            lane = lax.broadcasted_iota(I32, (acc.shape[0], LANES), 1)
            hot = jnp.where(lane - SEL_BLOCK == jnp.right_shift(t, SEL_SHIFT), 1.0, 0.0)
        for s in range(acc.shape[1] // LANES):
            r = _rope_slot(acc[:, s * LANES:(s + 1) * LANES], cos, sin, half, 0)
            if scale != 1.0:
                r = r * scale
            if onehot:
                r = r + hot
            o_ref[:, s * LANES:(s + 1) * LANES] = r.astype(o_ref.dtype)
    return epi


def _epi_rope_b_scaled(acc, o_ref, g_ref, tab_ref):
    cos = tab_ref[:, 2 * LANES:3 * LANES]
    sin = tab_ref[:, 3 * LANES:4 * LANES]
    scale = (MLA_NOPE_DIM + MLA_ROPE_DIM) ** -0.5 * LOG2E
    for s in range(acc.shape[1] // LANES):
        r = _rope_slot(acc[:, s * LANES:(s + 1) * LANES], cos, sin, MLA_ROPE_DIM // 2, MLA_ROPE_OFF)
        o_ref[:, s * LANES:(s + 1) * LANES] = (r * scale).astype(o_ref.dtype)


def _epi_gate_kr(acc, o_ref, tab_ref):
    o_ref[:, 0:LANES] = jax.nn.sigmoid(acc[:, 0:LANES]).astype(o_ref.dtype)
    cos = tab_ref[:, 2 * LANES:3 * LANES]
    sin = tab_ref[:, 3 * LANES:4 * LANES]
    r = _rope_slot(acc[:, LANES:2 * LANES], cos, sin, MLA_ROPE_DIM // 2, MLA_ROPE_OFF)
    o_ref[:, LANES:2 * LANES] = r.astype(o_ref.dtype)


def _pro_rms(xv, g_ref, *ex):
    return _rms(xv, g_ref[...]).astype(BF16)


def _epi_add_kr(acc, o_ref, g_ref, kr_ref):
    kr = kr_ref[...].astype(F32)
    for s in range(acc.shape[1] // LANES):
        o_ref[:, s * LANES:(s + 1) * LANES] = (acc[:, s * LANES:(s + 1) * LANES] + kr).astype(o_ref.dtype)


def _epi_plain_g(acc, o_ref, g_ref):
    o_ref[...] = acc.astype(o_ref.dtype)


def _compress_kernel(k16_ref, v16_ref, w1_ref, pe_ref, b1_ref, w2_ref, kc_ref, vc_ref):
    half = w1_ref.shape[1] // 2
    for which, (src, dst) in enumerate(((k16_ref, kc_ref), (v16_ref, vc_ref))):
        xb = src[...]
        a = jnp.dot(xb, w1_ref[which, 0:half, :], preferred_element_type=F32)
        b = jnp.dot(xb, w1_ref[which, half:2 * half, :], preferred_element_type=F32)
        c = jnp.dot(pe_ref[which], w1_ref[which], preferred_element_type=F32)[0:1, :] + b1_ref[which]
        rows = a.shape[0]
        pre = a + pltpu.roll(b, rows - 1, 0) + c
        hid = _gelu(pre).astype(BF16)
        dst[...] = jnp.dot(hid, w2_ref[which], preferred_element_type=F32).astype(dst.dtype)


def _compress(k16, v16, w1, pe, b1, w2p):
    bg, rows, feat = k16.shape
    blk = pl.BlockSpec((None, rows, feat), lambda i: (i, 0, 0))
    full = lambda a: pl.BlockSpec(a.shape, lambda i: (0,) * a.ndim)
    out = pl.BlockSpec((None, rows, LANES), lambda i: (i, 0, 0))
    return pl.pallas_call(
        _compress_kernel,
        grid=(bg,),
        in_specs=[blk, blk, full(w1), full(pe), full(b1), full(w2p)],
        out_specs=[out, out],
        out_shape=[jax.ShapeDtypeStruct((bg, rows, LANES), BF16)] * 2,
        compiler_params=_cparams(("arbitrary",)),
        name="nsa_compress",
    )(k16, v16, w1, pe, b1, w2p)


def _cmp_attn_kernel(q_ref, kc_ref, vc_ref, ovt_ref, o_ref, bias_ref, *, tq, n_sel):
    qi = pl.program_id(2)
    ncmp = kc_ref.shape[0]
    t_row = qi * tq + lax.broadcasted_iota(I32, (tq, ncmp), 0)
    n_col = lax.broadcasted_iota(I32, (tq, ncmp), 1)
    cmask = (n_col * CMP_STRIDE + (CMP_BLOCK - 1)) <= t_row
    kc = kc_ref[...]
    vc = vc_ref[...]
    imp_t = jnp.zeros((n_sel, tq), F32)
    for r in range(NSA_R):
        q = q_ref[:, r * LANES:(r + 1) * LANES]
        s = lax.dot_general(q, kc, (((1,), (1,)), ((), ())), preferred_element_type=F32)
        s = jnp.where(cmask, s, NEG)
        e = jnp.exp2(s - jnp.max(s, axis=1, keepdims=True))
        p = e / jnp.sum(e, axis=1, keepdims=True)
        p = jnp.where(cmask, p, 0.0).astype(BF16)
        o_ref[:, r * LANES:(r + 1) * LANES] = jnp.dot(p, vc, preferred_element_type=F32).astype(o_ref.dtype)
        imp_t = imp_t + lax.dot_general(ovt_ref[...], p, (((1,), (1,)), ((), ())),
                                        preferred_element_type=F32)
    blk = lax.broadcasted_iota(I32, (n_sel, tq), 0)
    t = qi * tq + lax.broadcasted_iota(I32, (n_sel, tq), 1)
    tb = jnp.right_shift(t, SEL_SHIFT)
    forced = (blk == 0) | (blk == tb) | (blk == tb - 1)
    valid = blk * SEL_BLOCK <= t
    x = jnp.where(valid, imp_t + jnp.where(forced, FORCE_BONUS, 0.0), NEG)
    sel = jnp.zeros((n_sel, tq), F32)
    for _ in range(min(SEL_TOPK, n_sel)):
        m = jnp.max(x, axis=0, keepdims=True)
        idx = jnp.min(jnp.where(x == m, blk, n_sel), axis=0, keepdims=True)
        hit = blk == idx
        sel = jnp.where(hit, 1.0, sel)
        x = jnp.where(hit, -jnp.inf, x)
    bias_t = jnp.where(sel > 0.5, 0.0, NEG)
    parts = [jnp.zeros((SEL_BLOCK, tq), F32), bias_t]
    if LANES - SEL_BLOCK - n_sel > 0:
        parts.append(jnp.zeros((LANES - SEL_BLOCK - n_sel, tq), F32))
    full_t = jnp.concatenate(parts, axis=0)
    bias_ref[...] = full_t.T.astype(bias_ref.dtype)


def _cmp_attn(qn, kc, vc, ovt, bsz, seq, tq=256):
    n = qn.shape[0]
    nq = seq // tq
    n_sel = seq // SEL_BLOCK
    ncmp = kc.shape[1]
    qspec = pl.BlockSpec((tq, NSA_R * LANES), lambda b, g, qi: (b * nq + qi, g))
    kspec = pl.BlockSpec((None, ncmp, LANES), lambda b, g, qi: (b * NSA_GROUPS + g, 0, 0))
    return pl.pallas_call(
        functools.partial(_cmp_attn_kernel, tq=tq, n_sel=n_sel),
        grid=(bsz, NSA_GROUPS, nq),
        in_specs=[qspec, kspec, kspec, pl.BlockSpec(ovt.shape, lambda b, g, qi: (0, 0))],
        out_specs=[qspec, pl.BlockSpec((tq, LANES), lambda b, g, qi: (b * nq + qi, g))],
        out_shape=[jax.ShapeDtypeStruct((n, NSA_HEADS * LANES), BF16),
                   jax.ShapeDtypeStruct((n, NSA_GROUPS * LANES), BF16)],
        compiler_params=_cparams(("arbitrary",) * 3),
        name="nsa_cmp_attn",
    )(qn, kc, vc, ovt)


def _flash_kernel(*refs, mode, tq, tk, nr, nh, window):
    if mode == "sel":
        q_ref, k_ref, v_ref, bias_ref, o_ref, qs, m_s, l_s, acc_s = refs
    else:
        q_ref, k_ref, v_ref, o_ref, qs, m_s, l_s, acc_s = refs
    qi = pl.program_id(2)
    kk = pl.program_id(3)
    nk = pl.num_programs(3)
    rows = nr * tq

    @pl.when(kk == 0)
    def _init():
        for r in range(nh * nr):
            qr = q_ref[:, r * LANES:(r + 1) * LANES]
            if mode == "sel":
                qr = qr + bias_ref[...]
            qs[r * tq:(r + 1) * tq, :] = qr
        m_s[...] = jnp.full(m_s.shape, -jnp.inf, F32)
        l_s[...] = jnp.zeros(l_s.shape, F32)
        acc_s[...] = jnp.zeros(acc_s.shape, F32)

    if mode == "win":
        kidx = qi * (tq // tk) - window // tk + kk
        needed = kidx >= 0
        full_vis = (kidx * tk + tk - 1 <= qi * tq) & (qi * tq + tq - 1 - kidx * tk < window)
    else:
        kidx = kk
        needed = kk * tk <= qi * tq + tq - 1
        full_vis = kk * tk + tk - 1 <= qi * tq

    nt = tk // LANES

    def update(masked):
        if masked:
            rel = (lax.broadcasted_iota(I32, (rows, tk), 0) & (tq - 1)) - lax.broadcasted_iota(I32, (rows, tk), 1)
            off = kidx * tk - qi * tq
            vis = rel >= off
            if mode == "win":
                vis = vis & (rel < off + window)
        for hh in range(nh):
            rs = slice(hh * rows, (hh + 1) * rows)
            cs = slice(hh * LANES, (hh + 1) * LANES)
            s = lax.dot_general(qs[rs, :], k_ref[:, cs], (((1,), (1,)), ((), ())),
                                preferred_element_type=F32)
            if masked:
                s = jnp.where(vis, s, NEG)
            m_prev = m_s[rs, :]
            m_new = jnp.maximum(m_prev, jnp.max(s, axis=1, keepdims=True))
            p = jnp.exp2(s - jnp.concatenate([m_new] * nt, axis=1))
            alpha = jnp.exp2(m_prev - m_new)
            psum = p[:, 0:LANES]
            for c in range(1, nt):
                psum = psum + p[:, c * LANES:(c + 1) * LANES]
            l_s[rs, :] = alpha * l_s[rs, :] + psum
            acc_s[rs, :] = alpha * acc_s[rs, :] + jnp.dot(p.astype(BF16), v_ref[:, cs],
                                                          preferred_element_type=F32)
            m_s[rs, :] = m_new

    @pl.when(needed & full_vis)
    def _full():
        update(False)

    @pl.when(needed & jnp.logical_not(full_vis))
    def _edge():
        update(True)

    @pl.when(kk == nk - 1)
    def _fin():
        out = acc_s[...] / jnp.sum(l_s[...], axis=1, keepdims=True)
        for r in range(nh * nr):
            o_ref[:, r * LANES:(r + 1) * LANES] = out[r * tq:(r + 1) * tq, :].astype(o_ref.dtype)


def _flash(q, k, v, *, mode, bsz, seq, nslots, nr, tq, tk, nh=1, k_off=0, v_off=0, bias=None,
           name):
    assert tq & (tq - 1) == 0
    n = q.shape[0]
    nq = seq // tq
    nkb = seq // tk
    if mode == "win":
        assert tq % tk == 0 and WINDOW % tk == 0
        steps = WINDOW // tk + tq // tk

        def krow(b, qi, kk):
            return b * nkb + jnp.maximum(qi * (tq // tk) - WINDOW // tk + kk, 0)
    else:
        steps = nkb

        def krow(b, qi, kk):
            return b * nkb + jnp.minimum(kk, (qi * tq + tq - 1) // tk)

    qspec = pl.BlockSpec((tq, nh * nr * LANES), lambda b, g, qi, kk: (b * nq + qi, g))
    in_specs = [qspec,
                pl.BlockSpec((tk, nh * LANES), lambda b, g, qi, kk: (krow(b, qi, kk), k_off + g)),
                pl.BlockSpec((tk, nh * LANES), lambda b, g, qi, kk: (krow(b, qi, kk), v_off + g))]
    args = [q, k, v]
    if mode == "sel":
        assert nh == 1
        in_specs.append(pl.BlockSpec((tq, LANES), lambda b, g, qi, kk: (b * nq + qi, g)))
        args.append(bias)
    rows = nh * nr * tq
    return pl.pallas_call(
        functools.partial(_flash_kernel, mode=mode, tq=tq, tk=tk, nr=nr, nh=nh, window=WINDOW),
        grid=(bsz, nslots, nq, steps),
        in_specs=in_specs,
        out_specs=qspec,
        out_shape=jax.ShapeDtypeStruct((n, nslots * nh * nr * LANES), BF16),
        scratch_shapes=[pltpu.VMEM((rows, LANES), BF16), pltpu.VMEM((rows, LANES), F32),
                        pltpu.VMEM((rows, LANES), F32), pltpu.VMEM((rows, LANES), F32)],
        compiler_params=_cparams(("arbitrary",) * 4, 48),
        name=name,
    )(*args)


def _combine_kernel(oc_ref, os_ref, ow_ref, g_ref, e_ref, o_ref):
    g = g_ref[...]
    acc = None
    for br, src in enumerate((oc_ref, os_ref, ow_ref)):
        ge = jnp.dot(g, e_ref[br], preferred_element_type=F32)
        term = ge * src[...].astype(F32)
        acc = term if acc is None else acc + term
    o_ref[...] = acc.astype(o_ref.dtype)


def _combine(o_cmp, o_sel, o_win, gates, gates_col, tm=512):
    n, w = o_cmp.shape
    e = np.zeros((3, LANES, w), np.float32)
    for h in range(NSA_HEADS):
        for br in range(3):
            e[br, h * 3 + br, h * LANES:h * LANES + NSA_HEAD_DIM] = 1.0
    row = pl.BlockSpec((tm, w), lambda i: (i, 0))
    return pl.pallas_call(
        _combine_kernel,
        grid=(n // tm,),
        in_specs=[row, row, row, pl.BlockSpec((tm, LANES), lambda i: (i, gates_col)),
                  pl.BlockSpec(e.shape, lambda i: (0, 0, 0))],
        out_specs=row,
        out_shape=jax.ShapeDtypeStruct((n, w), BF16),
        compiler_params=_cparams(("arbitrary",)),
        name="nsa_combine",
    )(o_cmp, o_sel, o_win, gates, jnp.asarray(e, BF16))


def _merge_kernel(a_ref, wa_ref, b_ref, wb_ref, g0_ref, g1_ref, o_ref):
    ya = jnp.dot(a_ref[...], wa_ref[...], preferred_element_type=F32)
    yb = jnp.dot(b_ref[...], wb_ref[...], preferred_element_type=F32)
    o_ref[...] = (g0_ref[...].astype(F32) * ya + g1_ref[...].astype(F32) * yb).astype(o_ref.dtype)


def _merge(o_nsa, w_nsa, o_mla, w_mla, gm, tm=512, tn=512):
    n, k = o_nsa.shape
    d = w_nsa.shape[1]
    nj = d // tn
    row = pl.BlockSpec((tm, k), lambda i, j: (i, 0))
    wsp = pl.BlockSpec((k, tn), lambda i, j: (0, j))
    return pl.pallas_call(
        _merge_kernel,
        grid=(n // tm, nj),
        in_specs=[row, wsp, row, wsp,
                  pl.BlockSpec((tm, tn), lambda i, j: (i, j)),
                  pl.BlockSpec((tm, tn), lambda i, j: (i, nj + j))],
        out_specs=pl.BlockSpec((tm, tn), lambda i, j: (i, j)),
        out_shape=jax.ShapeDtypeStruct((n, d), BF16),
        compiler_params=_cparams(("arbitrary", "arbitrary")),
        name="branch_merge",
    )(o_nsa, w_nsa, o_mla, w_mla, gm, gm)


def _extract_ranked(x, nrounds):
    iota = lax.broadcasted_iota(I32, x.shape, 0).astype(F32)
    orow = lax.broadcasted_iota(I32, (nrounds, x.shape[1]), 0)
    vals = jnp.full((nrounds, x.shape[1]), -jnp.inf, F32)
    rank = jnp.full(x.shape, float(nrounds), F32)
    for k in range(nrounds):
        m = jnp.max(x, axis=0, keepdims=True)
        idx = jnp.min(jnp.where(x == m, iota, float(x.shape[0])), axis=0, keepdims=True)
        hit = iota == idx
        x = jnp.where(hit, -jnp.inf, x)
        rank = jnp.where(hit, float(k), rank)
        vals = jnp.where(orow == k, m, vals)
    return vals, rank


def _peer_route_kernel(q_ref, sk_ref, tb_ref, tf_ref):
    nk = PEER_N_KEYS
    sec = PEER_HEADS * nk
    tr = q_ref.shape[0]
    orow = lax.broadcasted_iota(I32, (PEER_TOPK, tr), 0)
    for h in range(PEER_HEADS):
        s = []
        for p in range(2):
            c = (2 * h + p) * nk
            s.append(lax.dot_general(sk_ref[2 * h + p], q_ref[:, c:c + nk], (((1,), (1,)), ((), ())),
                                     preferred_element_type=F32))
        s1, s2 = s
        v1, rank1 = _extract_ranked(s1, PEER_TOPK)
        v2, rank2 = _extract_ranked(s2, PEER_TOPK)
        cands = []
        for i in range(PEER_TOPK):
            cnt = PEER_TOPK // (i + 1)
            rows = PEER_TOPK if i == 0 else SUBLANES
            c = v1[i:i + 1, :] + v2[0:rows, :]
            if cnt < rows:
                c = jnp.where(lax.broadcasted_iota(I32, (rows, tr), 0) < cnt, c, -jnp.inf)
            cands.append(c)
        top, _ = _extract_ranked(jnp.concatenate(cands, axis=0), PEER_TOPK)
        tau = top[PEER_TOPK - 1:PEER_TOPK, :]
        z = jnp.sum(jnp.exp(top - top[0:1, :]), axis=0, keepdims=True)
        cnt_rows = jnp.zeros((PEER_TOPK, tr), F32)
        for i in range(PEER_TOPK):
            ci = jnp.sum(jnp.where((v1[i:i + 1, :] + v2) >= tau, 1.0, 0.0), axis=0, keepdims=True)
            cnt_rows = jnp.where(orow == i, ci, cnt_rows)
        count = jnp.zeros((nk, tr), F32)
        for i in range(PEER_TOPK):
            count = jnp.where(rank1 == float(i), cnt_rows[i:i + 1, :], count)
        tb_ref[h * nk:(h + 1) * nk, :] = rank2.astype(tb_ref.dtype)
        tb_ref[sec + h * nk:sec + (h + 1) * nk, :] = jnp.exp(s2 - v2[0:1, :]).astype(tb_ref.dtype)
        tf_ref[h * nk:(h + 1) * nk, :] = count
        tf_ref[sec + h * nk:sec + (h + 1) * nk, :] = jnp.exp(s1 - v1[0:1, :]) / z


def _peer_route(qry, sk, tr=256):
    n, w = qry.shape
    col = pl.BlockSpec((w, tr), lambda i: (0, i))
    return pl.pallas_call(
        _peer_route_kernel,
        grid=(n // tr,),
        in_specs=[pl.BlockSpec((tr, w), lambda i: (i, 0)),
                  pl.BlockSpec(sk.shape, lambda i: (0, 0, 0))],
        out_specs=[col, col],
        out_shape=[jax.ShapeDtypeStruct((w, n), BF16), jax.ShapeDtypeStruct((w, n), F32)],
        compiler_params=_cparams(("arbitrary",), 40),
        name="peer_route",
    )(qry, sk)


PEER_CHUNK = 32


def _peer_dense_kernel(h_ref, u_ref, vt_ref, tb_ref, tf_ref, o_ref, acc_s, pt_s, *, te):
    j = pl.program_id(1)
    nk = PEER_N_KEYS
    sec = PEER_HEADS * nk
    gdt = pt_s.dtype

    @pl.when(j == 0)
    def _init():
        acc_s[...] = jnp.zeros(acc_s.shape, F32)

    zt = lax.dot_general(u_ref[...], h_ref[...], (((1,), (1,)), ((), ())), preferred_element_type=F32)
    for al in range(te // nk):
        a = j * (te // nk) + al
        cnt = [tf_ref[pl.ds(h * nk + a, 1), :].astype(gdt) for h in range(PEER_HEADS)]
        e1 = [tf_ref[pl.ds(sec + h * nk + a, 1), :].astype(gdt) for h in range(PEER_HEADS)]
        for c in range(nk // PEER_CHUNK):
            lo = c * PEER_CHUNK
            g = None
            for h in range(PEER_HEADS):
                rank2 = tb_ref[h * nk + lo:h * nk + lo + PEER_CHUNK, :]
                e2 = tb_ref[sec + h * nk + lo:sec + h * nk + lo + PEER_CHUNK, :]
                term = jnp.where(rank2 < cnt[h], e1[h] * e2, jnp.zeros((), gdt))
                g = term if g is None else g + term
            r0 = al * nk + lo
            act = _gelu(zt[r0:r0 + PEER_CHUNK, :]).astype(gdt)
            pt_s[r0:r0 + PEER_CHUNK, :] = g * act
    acc_s[...] += jnp.dot(vt_ref[...], pt_s[...], preferred_element_type=F32)

    @pl.when(j == pl.num_programs(1) - 1)
    def _fin():
        o_ref[...] = acc_s[...].T


def _peer_dense(h2, u_bf, vt_bf, tb, tf, tm=512, te=1024):
    n, d = h2.shape
    ne = u_bf.shape[0]
    w = tb.shape[0]
    return pl.pallas_call(
        functools.partial(_peer_dense_kernel, te=te),
        grid=(n // tm, ne // te),
        in_specs=[pl.BlockSpec((tm, d), lambda i, j: (i, 0)),
                  pl.BlockSpec((te, d), lambda i, j: (j, 0)),
                  pl.BlockSpec((d, te), lambda i, j: (0, j)),
                  pl.BlockSpec((w, tm), lambda i, j: (0, i)),
                  pl.BlockSpec((w, tm), lambda i, j: (0, i))],
        out_specs=pl.BlockSpec((tm, d), lambda i, j: (i, 0)),
        out_shape=jax.ShapeDtypeStruct((n, d), F32),
        scratch_shapes=[pltpu.VMEM((d, tm), F32), pltpu.VMEM((te, tm), BF16)],
        compiler_params=_cparams(("arbitrary", "arbitrary"), 56),
        name="peer_dense",
    )(h2, u_bf, vt_bf, tb, tf)


def _slots(w, n, width, off=0):
    k = w.shape[0]
    w = w.reshape(k, n, width)
    w = jnp.pad(w, ((0, 0), (0, 0), (off, LANES - width - off)))
    return w.reshape(k, n * LANES)


def _slot_rows(w, n, width):
    d = w.shape[1]
    w = w.reshape(n, width, d)
    w = jnp.pad(w, ((0, 0), (0, LANES - width), (0, 0)))
    return w.reshape(n * LANES, d)


def _overlap_t(seq):
    n_cmp_rows = seq // CMP_STRIDE
    n_sel = seq // SEL_BLOCK
    cs = np.arange(n_cmp_rows) * CMP_STRIDE
    ce = cs + CMP_BLOCK - 1
    ss = np.arange(n_sel) * SEL_BLOCK
    ov = (cs[None, :] < ss[:, None] + SEL_BLOCK) & (ce[None, :] >= ss[:, None])
    ov[:, n_cmp_rows - 1] = False
    return jnp.asarray(ov.astype(np.float32), BF16)


def _token_mixer(h, positions, bsz, seq, w_in, cmp_pe, cmp_w1, cmp_b1, cmp_w2, mla_q_norm_g,
                 mla_w_q_up, mla_kv_norm_g, mla_w_kv_up, w_branch_nsa, w_branch_mla, w_out):
    d = h.shape[1]
    g_, hd = NSA_GROUPS, NSA_HEAD_DIM
    tab = _rope_tables(positions)
    tab_spec = pl.BlockSpec((512, 4 * LANES), lambda i, j: (i, 0))

    wb = w_in.astype(BF16)
    o1 = NSA_HEADS * hd
    kv = wb[:, o1:o1 + 3 * 2 * g_ * hd].reshape(d, 3, 2, g_ * hd)
    o2 = o1 + 3 * 2 * g_ * hd
    o3 = o2 + 3 * NSA_HEADS
    mq = mla_w_q_up.shape[0]
    mkv = mla_w_kv_up.shape[0]
    o4 = o3 + mq
    o5 = o4 + mkv
    o6 = o5 + MLA_ROPE_DIM
    w_q = _slots(wb[:, :o1], NSA_HEADS, hd)
    w_kk = jnp.concatenate([_slots(kv[:, 0, 0], g_, hd), _slots(kv[:, 2, 0], g_, hd)], axis=1)
    w_ks = _slots(kv[:, 1, 0], g_, hd)
    w_vv = jnp.concatenate([_slots(kv[:, 1, 1], g_, hd), _slots(kv[:, 2, 1], g_, hd),
                            kv[:, 0, 1], jnp.zeros((d, g_ * hd), BF16)], axis=1)
    w_c = wb[:, o3:o5]
    w_gk = jnp.concatenate([jnp.pad(wb[:, o2:o3], ((0, 0), (0, LANES - 3 * NSA_HEADS))),
                            _slots(wb[:, o5:o6], 1, MLA_ROPE_DIM, MLA_ROPE_OFF)], axis=1)
    w_gm = wb[:, o6:]

    tm = 512
    qn = _mm(h, w_q, _make_epi_rope_a(hd ** -0.5 * LOG2E), tm=tm, tn=512, out_dtype=BF16, name="proj_q",
             extras=[(tab, tab_spec)])
    kk = _mm(h, w_kk, _make_epi_rope_a(1.0), tm=tm, tn=512, out_dtype=BF16, name="proj_k",
             extras=[(tab, tab_spec)])
    ks = _mm(h, w_ks, _make_epi_rope_a(1.0, seq, tm, True), tm=tm, tn=512, out_dtype=BF16,
             name="proj_ksel", extras=[(tab, tab_spec)])
    vv = _mm(h, w_vv, _epi_plain, tm=tm, tn=512, out_dtype=BF16, name="proj_v")
    cqkv = _mm(h, w_c, _epi_plain, tm=tm, tn=mq + mkv, out_dtype=F32, name="proj_c")
    gk = _mm(h, w_gk, _epi_gate_kr, tm=tm, tn=2 * LANES, out_dtype=BF16, name="proj_gate_kr",
             extras=[(tab, tab_spec)])
    gm = _mm(h, w_gm, _epi_sigmoid, tm=tm, tn=512, out_dtype=BF16, name="proj_gmerge")

    def to16(a, width):
        a = a.reshape(bsz, seq, g_, width)[..., :hd]
        a = jnp.transpose(a, (0, 2, 1, 3))
        return a.reshape(bsz * g_, seq // CMP_STRIDE, CMP_STRIDE * hd)

    k16 = to16(kk[:, :g_ * LANES], LANES)
    v16 = to16(vv[:, 2 * g_ * LANES:2 * g_ * LANES + g_ * hd], hd)
    pe = jnp.pad(cmp_pe.reshape(2, 1, CMP_BLOCK * hd), ((0, 0), (0, SUBLANES - 1), (0, 0))).astype(BF16)
    w2p = jnp.pad(cmp_w2, ((0, 0), (0, 0), (0, LANES - hd))).astype(BF16)
    kc, vc = _compress(k16, v16, cmp_w1.astype(BF16), pe, cmp_b1.reshape(2, 1, CMP_HIDDEN), w2p)
    o_cmp, bias = _cmp_attn(qn, kc, vc, _overlap_t(seq), bsz, seq)
    o_sel = _flash(qn, ks, vv, mode="sel", bsz=bsz, seq=seq, nslots=g_, nr=NSA_R, tq=512, tk=512,
                   bias=bias, name="nsa_sel_attn")
    o_win = _flash(qn, kk, vv, mode="win", bsz=bsz, seq=seq, nslots=g_, nr=NSA_R, tq=256, tk=256,
                   k_off=g_, v_off=g_, name="nsa_win_attn")
    o_nsa = _combine(o_cmp, o_sel, o_win, gk, 0)

    qd = MLA_NOPE_DIM + MLA_ROPE_DIM
    w_qup = _slots(mla_w_q_up.astype(BF16), MLA_HEADS, qd)
    kvu = mla_w_kv_up.astype(BF16).reshape(mkv, MLA_HEADS, MLA_NOPE_DIM + MLA_V_DIM)
    w_kup = _slots(kvu[:, :, :MLA_NOPE_DIM].reshape(mkv, -1), MLA_HEADS, MLA_NOPE_DIM)
    w_vup = _slots(kvu[:, :, MLA_NOPE_DIM:].reshape(mkv, -1), MLA_HEADS, MLA_V_DIM)
    gq_spec = pl.BlockSpec((1, mq), lambda i, j: (0, 0))
    gkv_spec = pl.BlockSpec((1, mkv), lambda i, j: (0, 0))
    ckv_spec = pl.BlockSpec((tm, mkv), lambda i, j: (i, mq // mkv))
    q_mla = _mm(cqkv, w_qup, _epi_rope_b_scaled, tm=tm, tn=512,
                out_dtype=BF16, name="mla_q_up", pro=_pro_rms,
                x_spec=pl.BlockSpec((tm, mq), lambda i, j: (i, 0)),
                extras=[(mla_q_norm_g.reshape(1, mq), gq_spec), (tab, tab_spec)])
    k_mla = _mm(cqkv, w_kup, _epi_add_kr, tm=tm, tn=512, out_dtype=BF16, name="mla_k_up",
                pro=_pro_rms, x_spec=ckv_spec,
                extras=[(mla_kv_norm_g.reshape(1, mkv), gkv_spec),
                        (gk, pl.BlockSpec((tm, LANES), lambda i, j: (i, 1)))])
    v_mla = _mm(cqkv, w_vup, _epi_plain_g, tm=tm, tn=512, out_dtype=BF16, name="mla_v_up",
                pro=_pro_rms, x_spec=ckv_spec,
                extras=[(mla_kv_norm_g.reshape(1, mkv), gkv_spec)])
    o_mla = _flash(q_mla, k_mla, v_mla, mode="causal", bsz=bsz, seq=seq, nslots=MLA_HEADS // 4, nr=1,
                   nh=4, tq=512, tk=512, name="mla_attn")

    merged = _merge(o_nsa, _slot_rows(w_branch_nsa.astype(BF16), NSA_HEADS, hd),
                    o_mla, _slot_rows(w_branch_mla.astype(BF16), MLA_HEADS, MLA_V_DIM), gm)
    return _mm(merged, w_out.astype(BF16), _epi_plain, tm=tm, tn=512, out_dtype=F32, name="out_proj")


def _peer(h2, peer_w_q, peer_sub_keys, peer_u, peer_v):
    qry = _mm(h2, peer_w_q.astype(BF16), _epi_plain, tm=512, tn=512, out_dtype=BF16, name="peer_q")
    sk = peer_sub_keys.astype(BF16).reshape(2 * PEER_HEADS, PEER_N_KEYS, -1)
    tb, tf = _peer_route(qry, sk)
    return _peer_dense(h2, peer_u.astype(BF16), jnp.transpose(peer_v).astype(BF16), tb, tf)


def _block(x, c, positions, ada_w, ada_b, attn_pre_g, attn_post_g, w_in, cmp_pe, cmp_w1, cmp_b1,
           cmp_w2, mla_q_norm_g, mla_w_q_up, mla_kv_norm_g, mla_w_kv_up, w_branch_nsa, w_branch_mla,
           w_out, ffn_pre_g, ffn_post_g, peer_w_q, peer_sub_keys, peer_u, peer_v):
    bsz, seq, d = x.shape
    x2 = x.reshape(bsz * seq, d)
    mod = _adaln(c, ada_w, ada_b)
    sh_a, sc_a, g_a, sh_f, sc_f, g_f = [mod[:, i * d:(i + 1) * d] for i in range(6)]

    h = _normmod(x2, attn_pre_g, sc_a, sh_a, seq)
    y_attn = _token_mixer(h, positions, bsz, seq, w_in, cmp_pe, cmp_w1, cmp_b1, cmp_w2, mla_q_norm_g,
                          mla_w_q_up, mla_kv_norm_g, mla_w_kv_up, w_branch_nsa, w_branch_mla, w_out)
    x1 = _norm_res(x2, y_attn, attn_post_g, g_a, seq)

    h2 = _normmod(x1, ffn_pre_g, sc_f, sh_f, seq)
    y_ffn = _peer(h2, peer_w_q, peer_sub_keys, peer_u, peer_v)
    out = _norm_res(x1, y_ffn, ffn_post_g, g_f, seq)
    return out.reshape(bsz, seq, d)


def kernel(x, c, positions, ada_w, ada_b, attn_pre_g, attn_post_g, w_in, cmp_pe, cmp_w1, cmp_b1,
           cmp_w2, mla_q_norm_g, mla_w_q_up, mla_kv_norm_g, mla_w_kv_up, w_branch_nsa, w_branch_mla,
           w_out, ffn_pre_g, ffn_post_g, peer_w_q, peer_sub_keys, peer_u, peer_v):
    depth = ada_w.shape[0]
    for l in range(depth):
        x = _block(x, c, positions, ada_w[l], ada_b[l], attn_pre_g[l], attn_post_g[l], w_in[l],
                   cmp_pe[l], cmp_w1[l], cmp_b1[l], cmp_w2[l], mla_q_norm_g[l], mla_w_q_up[l],
                   mla_kv_norm_g[l], mla_w_kv_up[l], w_branch_nsa[l], w_branch_mla[l], w_out[l],
                   ffn_pre_g[l], ffn_post_g[l], peer_w_q[l], peer_sub_keys[l], peer_u[l], peer_v[l])
    return x
```

```python
import functools

import numpy as np
import jax
import jax.numpy as jnp
from jax import lax
from jax.experimental import pallas as pl
from jax.experimental.pallas import tpu as pltpu

F32 = jnp.float32
BF16 = jnp.bfloat16
I32 = jnp.int32

LANES = 128
SUBLANES = 8

NSA_HEADS = 16
NSA_GROUPS = 4
NSA_R = NSA_HEADS // NSA_GROUPS
NSA_HEAD_DIM = 64
NSA_ROPE_DIM = NSA_HEAD_DIM // 4
CMP_BLOCK = 32
CMP_STRIDE = 16
CMP_HIDDEN = 256
SEL_BLOCK = 64
SEL_SHIFT = 6
SEL_TOPK = 16
WINDOW = 512
MLA_HEADS = 16
MLA_NOPE_DIM = 64
MLA_ROPE_DIM = 32
MLA_V_DIM = 64
PEER_HEADS = 8
PEER_N_KEYS = 128
PEER_TOPK = 16
ROPE_THETA = 500000.0
NORM_EPS = 1e-6
NEG = -1e30
FORCE_BONUS = 1e4
LOG2E = 1.4426950408889634
MLA_ROPE_OFF = MLA_NOPE_DIM
PROJ_TM = 1024


def _cparams(sem, vmem_mb=None):
    kw = dict(dimension_semantics=sem)
    if vmem_mb is not None:
        kw["vmem_limit_bytes"] = vmem_mb * 1024 * 1024
    return pltpu.CompilerParams(**kw)


def _rms(x, g):
    return x * lax.rsqrt(jnp.mean(x * x, axis=-1, keepdims=True) + NORM_EPS) * g


def _gelu(x):
    return jax.nn.gelu(x, approximate=True)


def _adaln_kernel(c_ref, w_ref, b_ref, o_ref):
    c = c_ref[...]
    s = (c * jax.nn.sigmoid(c)).astype(BF16)
    o_ref[...] = jnp.dot(s, w_ref[...].astype(BF16), preferred_element_type=F32) + b_ref[...]


def _adaln(c, w, b, tn=1024):
    bsz, d = c.shape
    n = w.shape[1]
    cp = jnp.pad(c, ((0, SUBLANES - bsz), (0, 0)))
    out = pl.pallas_call(
        _adaln_kernel,
        grid=(n // tn,),
        in_specs=[pl.BlockSpec((SUBLANES, d), lambda j: (0, 0)),
                  pl.BlockSpec((d, tn), lambda j: (0, j)),
                  pl.BlockSpec((1, tn), lambda j: (0, j))],
        out_specs=pl.BlockSpec((SUBLANES, tn), lambda j: (0, j)),
        out_shape=jax.ShapeDtypeStruct((SUBLANES, n), F32),
        compiler_params=_cparams(("arbitrary",), 40),
        name="adaln",
    )(cp, w, b.reshape(1, n))
    return out[:bsz]


def _rope_table_kernel(pos_ref, inv_ref, sgn_ref, o_ref):
    pos = pos_ref[...]
    for k in range(2):
        ang = pos * inv_ref[k:k + 1, :]
        o_ref[:, (2 * k) * LANES:(2 * k + 1) * LANES] = jnp.cos(ang)
        o_ref[:, (2 * k + 1) * LANES:(2 * k + 2) * LANES] = jnp.sin(ang) * sgn_ref[k:k + 1, :]


def _rope_tables(positions, tm=512):
    n = positions.size
    half_a = NSA_ROPE_DIM // 2
    half_b = MLA_ROPE_DIM // 2
    inv_a = ROPE_THETA ** (-jnp.arange(half_a, dtype=F32) * (2.0 / NSA_ROPE_DIM))
    inv_b = ROPE_THETA ** (-jnp.arange(half_b, dtype=F32) * (2.0 / MLA_ROPE_DIM))
    inv = jnp.zeros((2, LANES), F32)
    inv = inv.at[0, 0:half_a].set(inv_a).at[0, half_a:2 * half_a].set(inv_a)
    o = MLA_ROPE_OFF
    inv = inv.at[1, o:o + half_b].set(inv_b).at[1, o + half_b:o + 2 * half_b].set(inv_b)
    sgn = np.zeros((2, LANES), np.float32)
    sgn[0, 0:half_a] = -1.0
    sgn[0, half_a:2 * half_a] = 1.0
    sgn[1, o:o + half_b] = -1.0
    sgn[1, o + half_b:o + 2 * half_b] = 1.0
    posf = jnp.broadcast_to(positions.reshape(n, 1).astype(F32), (n, LANES))
    return pl.pallas_call(
        _rope_table_kernel,
        grid=(n // tm,),
        in_specs=[pl.BlockSpec((tm, LANES), lambda i: (i, 0)),
                  pl.BlockSpec((2, LANES), lambda i: (0, 0)),
                  pl.BlockSpec((2, LANES), lambda i: (0, 0))],
        out_specs=pl.BlockSpec((tm, 4 * LANES), lambda i: (i, 0)),
        out_shape=jax.ShapeDtypeStruct((n, 4 * LANES), F32),
        compiler_params=_cparams(("arbitrary",)),
        name="rope_tables",
    )(posf, inv, jnp.asarray(sgn))


def _rope_slot(a, cos, sin, half, off):
    lane = lax.broadcasted_iota(I32, (1, LANES), 1)
    first = (lane - off) < half
    partner = jnp.where(first, pltpu.roll(a, LANES - half, 1), pltpu.roll(a, half, 1))
    return a * cos + partner * sin


def _normmod_kernel(x_ref, g_ref, sc_ref, sh_ref, o_ref):
    y = _rms(x_ref[...], g_ref[...])
    o_ref[...] = (y * (1.0 + sc_ref[0]) + sh_ref[0]).astype(o_ref.dtype)


def _normmod(x, g, sc, sh, seq, tm=512):
    n, d = x.shape
    bsz = sc.shape[0]
    bspec = pl.BlockSpec((1, 1, d), lambda i: ((i * tm) // seq, 0, 0))
    return pl.pallas_call(
        _normmod_kernel,
        grid=(n // tm,),
        in_specs=[pl.BlockSpec((tm, d), lambda i: (i, 0)),
                  pl.BlockSpec((1, d), lambda i: (0, 0)), bspec, bspec],
        out_specs=pl.BlockSpec((tm, d), lambda i: (i, 0)),
        out_shape=jax.ShapeDtypeStruct((n, d), BF16),
        compiler_params=_cparams(("arbitrary",)),
        name="normmod",
    )(x, g.reshape(1, d), sc.reshape(bsz, 1, d), sh.reshape(bsz, 1, d))


def _norm_res_kernel(x_ref, y_ref, g_ref, ga_ref, o_ref):
    o_ref[...] = x_ref[...] + ga_ref[0] * _rms(y_ref[...], g_ref[...])


def _norm_res(x, y, g, gate, seq, tm=512):
    n, d = x.shape
    bsz = gate.shape[0]
    row = pl.BlockSpec((tm, d), lambda i: (i, 0))
    return pl.pallas_call(
        _norm_res_kernel,
        grid=(n // tm,),
        in_specs=[row, row, pl.BlockSpec((1, d), lambda i: (0, 0)),
                  pl.BlockSpec((1, 1, d), lambda i: ((i * tm) // seq, 0, 0))],
        out_specs=row,
        out_shape=jax.ShapeDtypeStruct((n, d), F32),
        compiler_params=_cparams(("arbitrary",)),
        name="norm_res",
    )(x, y, g.reshape(1, d), gate.reshape(bsz, 1, d))


def _mm(x, w, epi, *, tm, tn, out_dtype, name, extras=(), pro=None, x_spec=None, vmem_mb=None):
    m = x.shape[0]
    k, nc = w.shape
    if x_spec is None:
        x_spec = pl.BlockSpec((tm, k), lambda i, j: (i, 0))
    in_specs = [x_spec, pl.BlockSpec((k, tn), lambda i, j: (0, j))]
    args = [x, w]
    for arr, spec in extras:
        in_specs.append(spec)
        args.append(arr)

    def kern(x_ref, w_ref, *rest):
        o_ref = rest[-1]
        ex = rest[:-1]
        xv = x_ref[...]
        if pro is not None:
            xv = pro(xv, *ex)
        acc = jnp.dot(xv, w_ref[...], preferred_element_type=F32)
        epi(acc, o_ref, *ex)

    return pl.pallas_call(
        kern,
        grid=(m // tm, nc // tn),
        in_specs=in_specs,
        out_specs=pl.BlockSpec((tm, tn), lambda i, j: (i, j)),
        out_shape=jax.ShapeDtypeStruct((m, nc), out_dtype),
        compiler_params=_cparams(("arbitrary", "arbitrary"), vmem_mb),
        name=name,
    )(*args)


def _epi_plain(acc, o_ref, *ex):
    o_ref[...] = acc.astype(o_ref.dtype)


def _epi_sigmoid(acc, o_ref, *ex):
    o_ref[...] = jax.nn.sigmoid(acc).astype(o_ref.dtype)


def _make_epi_rope_a(scale, seq=None, tm=None, onehot=False):
    half = NSA_ROPE_DIM // 2

    def epi(acc, o_ref, tab_ref):
        cos = tab_ref[:, 0:LANES]
        sin = tab_ref[:, LANES:2 * LANES]
        if onehot:
            base = lax.rem(pl.program_id(0) * tm, seq)
            t = base + lax.broadcasted_iota(I32, (acc.shape[0], LANES), 0)
            lane = lax.broadcasted_iota(I32, (acc.shape[0], LANES), 1)
            hot = jnp.where(lane - SEL_BLOCK == jnp.right_shift(t, SEL_SHIFT), 1.0, 0.0)
        for s in range(acc.shape[1] // LANES):
            r = _rope_slot(acc[:, s * LANES:(s + 1) * LANES], cos, sin, half, 0)
            if scale != 1.0:
                r = r * scale
            if onehot:
                r = r + hot
            o_ref[:, s * LANES:(s + 1) * LANES] = r.astype(o_ref.dtype)
    return epi


def _epi_rope_b_scaled(acc, o_ref, g_ref, tab_ref):
    cos = tab_ref[:, 2 * LANES:3 * LANES]
    sin = tab_ref[:, 3 * LANES:4 * LANES]
    scale = (MLA_NOPE_DIM + MLA_ROPE_DIM) ** -0.5 * LOG2E
    for s in range(acc.shape[1] // LANES):
        r = _rope_slot(acc[:, s * LANES:(s + 1) * LANES], cos, sin, MLA_ROPE_DIM // 2, MLA_ROPE_OFF)
        o_ref[:, s * LANES:(s + 1) * LANES] = (r * scale).astype(o_ref.dtype)


def _epi_gate_kr(acc, o_ref, tab_ref):
    o_ref[:, 0:LANES] = jax.nn.sigmoid(acc[:, 0:LANES]).astype(o_ref.dtype)
    cos = tab_ref[:, 2 * LANES:3 * LANES]
    sin = tab_ref[:, 3 * LANES:4 * LANES]
    r = _rope_slot(acc[:, LANES:2 * LANES], cos, sin, MLA_ROPE_DIM // 2, MLA_ROPE_OFF)
    o_ref[:, LANES:2 * LANES] = r.astype(o_ref.dtype)


def _pro_rms(xv, g_ref, *ex):
    return _rms(xv, g_ref[...]).astype(BF16)


def _epi_add_kr(acc, o_ref, g_ref, kr_ref):
    kr = kr_ref[...].astype(F32)
    for s in range(acc.shape[1] // LANES):
        o_ref[:, s * LANES:(s + 1) * LANES] = (acc[:, s * LANES:(s + 1) * LANES] + kr).astype(o_ref.dtype)


def _epi_plain_g(acc, o_ref, g_ref):
    o_ref[...] = acc.astype(o_ref.dtype)


def _compress_kernel(k16_ref, v16_ref, w1_ref, pe_ref, b1_ref, w2_ref, kc_ref, vc_ref):
    half = w1_ref.shape[1] // 2
    for which, (src, dst) in enumerate(((k16_ref, kc_ref), (v16_ref, vc_ref))):
        xb = src[...]
        a = jnp.dot(xb, w1_ref[which, 0:half, :], preferred_element_type=F32)
        b = jnp.dot(xb, w1_ref[which, half:2 * half, :], preferred_element_type=F32)
        c = jnp.dot(pe_ref[which], w1_ref[which], preferred_element_type=F32)[0:1, :] + b1_ref[which]
        rows = a.shape[0]
        pre = a + pltpu.roll(b, rows - 1, 0) + c
        hid = _gelu(pre).astype(BF16)
        dst[...] = jnp.dot(hid, w2_ref[which], preferred_element_type=F32).astype(dst.dtype)


def _compress(k16, v16, w1, pe, b1, w2p):
    bg, rows, feat = k16.shape
    blk = pl.BlockSpec((None, rows, feat), lambda i: (i, 0, 0))
    full = lambda a: pl.BlockSpec(a.shape, lambda i: (0,) * a.ndim)
    out = pl.BlockSpec((None, rows, LANES), lambda i: (i, 0, 0))
    return pl.pallas_call(
        _compress_kernel,
        grid=(bg,),
        in_specs=[blk, blk, full(w1), full(pe), full(b1), full(w2p)],
        out_specs=[out, out],
        out_shape=[jax.ShapeDtypeStruct((bg, rows, LANES), BF16)] * 2,
        compiler_params=_cparams(("arbitrary",)),
        name="nsa_compress",
    )(k16, v16, w1, pe, b1, w2p)


def _row_bcast(col, width):
    if width % LANES:
        return jnp.broadcast_to(col, (col.shape[0], width))
    tile = jnp.broadcast_to(col, (col.shape[0], LANES))
    return jnp.concatenate([tile] * (width // LANES), axis=1)


def _cmp_attn_kernel(q_ref, kc_ref, vc_ref, ovt_ref, o_ref, bias_ref, *, tq, n_sel):
    qi = pl.program_id(2)
    ncmp = kc_ref.shape[0]
    t_row = qi * tq + lax.broadcasted_iota(I32, (tq, ncmp), 0)
    n_col = lax.broadcasted_iota(I32, (tq, ncmp), 1)
    cmask = (n_col * CMP_STRIDE + (CMP_BLOCK - 1)) <= t_row
    kc = kc_ref[...]
    vc = vc_ref[...]
    imp_t = jnp.zeros((n_sel, tq), F32)
    outs = []
    for r in range(NSA_R):
        q = q_ref[:, r * LANES:(r + 1) * LANES]
        s = lax.dot_general(q, kc, (((1,), (1,)), ((), ())), preferred_element_type=F32)
        s = jnp.where(cmask, s, NEG)
        e = jnp.exp2(s - _row_bcast(jnp.max(s, axis=1, keepdims=True), ncmp))
        p = e / _row_bcast(jnp.sum(e, axis=1, keepdims=True), ncmp)
        p = jnp.where(cmask, p, 0.0).astype(BF16)
        outs.append(jnp.dot(p, vc, preferred_element_type=F32))
        imp_t = imp_t + lax.dot_general(ovt_ref[...], p, (((1,), (1,)), ((), ())),
                                        preferred_element_type=F32)
    _store_heads_compact(o_ref, outs)
    blk = lax.broadcasted_iota(I32, (n_sel, tq), 0)
    t = qi * tq + lax.broadcasted_iota(I32, (n_sel, tq), 1)
    tb = jnp.right_shift(t, SEL_SHIFT)
    forced = (blk == 0) | (blk == tb) | (blk == tb - 1)
    valid = blk * SEL_BLOCK <= t
    x = jnp.where(valid, imp_t + jnp.where(forced, FORCE_BONUS, 0.0), NEG)
    sel = jnp.zeros((n_sel, tq), F32)
    for _ in range(min(SEL_TOPK, n_sel)):
        m = jnp.max(x, axis=0, keepdims=True)
        idx = jnp.min(jnp.where(x == m, blk, n_sel), axis=0, keepdims=True)
        hit = blk == idx
        sel = jnp.where(hit, 1.0, sel)
        x = jnp.where(hit, -jnp.inf, x)
    bias_t = jnp.where(sel > 0.5, 0.0, NEG)
    parts = [jnp.zeros((SEL_BLOCK, tq), F32), bias_t]
    if LANES - SEL_BLOCK - n_sel > 0:
        parts.append(jnp.zeros((LANES - SEL_BLOCK - n_sel, tq), F32))
    full_t = jnp.concatenate(parts, axis=0)
    bias_ref[...] = full_t.T.astype(bias_ref.dtype)


def _cmp_attn(qn, kc, vc, ovt, bsz, seq, tq=256):
    n = qn.shape[0]
    nq = seq // tq
    n_sel = seq // SEL_BLOCK
    ncmp = kc.shape[1]
    qspec = pl.BlockSpec((tq, NSA_R * LANES), lambda b, g, qi: (b * nq + qi, g))
    kspec = pl.BlockSpec((None, ncmp, LANES), lambda b, g, qi: (b * NSA_GROUPS + g, 0, 0))
    return pl.pallas_call(
        functools.partial(_cmp_attn_kernel, tq=tq, n_sel=n_sel),
        grid=(bsz, NSA_GROUPS, nq),
        in_specs=[qspec, kspec, kspec, pl.BlockSpec(ovt.shape, lambda b, g, qi: (0, 0))],
        out_specs=[pl.BlockSpec((tq, NSA_R * NSA_HEAD_DIM), lambda b, g, qi: (b * nq + qi, g)),
                   pl.BlockSpec((tq, LANES), lambda b, g, qi: (b * nq + qi, g))],
        out_shape=[jax.ShapeDtypeStruct((n, NSA_HEADS * NSA_HEAD_DIM), BF16),
                   jax.ShapeDtypeStruct((n, NSA_GROUPS * LANES), BF16)],
        compiler_params=_cparams(("arbitrary",) * 3),
        name="nsa_cmp_attn",
    )(qn, kc, vc, ovt)


def _store_heads_compact(o_ref, heads):
    hd = NSA_HEAD_DIM
    for p in range(len(heads) // 2):
        pair = jnp.concatenate([heads[2 * p][:, 0:hd], heads[2 * p + 1][:, 0:hd]], axis=1)
        o_ref[:, p * LANES:(p + 1) * LANES] = pair.astype(o_ref.dtype)


def _flash_kernel(*refs, mode, tq, tk, nr, nh, window):
    if mode == "sel":
        q_ref, k_ref, v_ref, bias_ref, o_ref, qs, m_s, l_s, acc_s = refs
    else:
        q_ref, k_ref, v_ref, o_ref, qs, m_s, l_s, acc_s = refs
    qi = pl.program_id(2)
    kk = pl.program_id(3)
    nk = pl.num_programs(3)
    rows = nr * tq

    @pl.when(kk == 0)
    def _init():
        for r in range(nh * nr):
            qr = q_ref[:, r * LANES:(r + 1) * LANES]
            if mode == "sel":
                qr = qr + bias_ref[...]
            qs[r * tq:(r + 1) * tq, :] = qr
        m_s[...] = jnp.full(m_s.shape, -jnp.inf, F32)
        l_s[...] = jnp.zeros(l_s.shape, F32)
        acc_s[...] = jnp.zeros(acc_s.shape, F32)

    if mode == "win":
        kidx = qi * (tq // tk) - window // tk + kk
        needed = kidx >= 0
        full_vis = (kidx * tk + tk - 1 <= qi * tq) & (qi * tq + tq - 1 - kidx * tk < window)
    else:
        kidx = kk
        needed = kk * tk <= qi * tq + tq - 1
        full_vis = kk * tk + tk - 1 <= qi * tq

    nt = tk // LANES

    def update(masked):
        if masked:
            rel = (lax.broadcasted_iota(I32, (rows, tk), 0) & (tq - 1)) - lax.broadcasted_iota(I32, (rows, tk), 1)
            off = kidx * tk - qi * tq
            vis = rel >= off
            if mode == "win":
                vis = vis & (rel < off + window)
        for hh in range(nh):
            rs = slice(hh * rows, (hh + 1) * rows)
            cs = slice(hh * LANES, (hh + 1) * LANES)
            s = lax.dot_general(qs[rs, :], k_ref[:, cs], (((1,), (1,)), ((), ())),
                                preferred_element_type=F32)
            if masked:
                s = jnp.where(vis, s, NEG)
            m_prev = m_s[rs, :]
            m_new = jnp.maximum(m_prev, jnp.max(s, axis=1, keepdims=True))
            p = jnp.exp2(s - jnp.concatenate([m_new] * nt, axis=1))
            alpha = jnp.exp2(m_prev - m_new)
            psum = p[:, 0:LANES]
            for c in range(1, nt):
                psum = psum + p[:, c * LANES:(c + 1) * LANES]
            l_s[rs, :] = alpha * l_s[rs, :] + psum
            acc_s[rs, :] = alpha * acc_s[rs, :] + jnp.dot(p.astype(BF16), v_ref[:, cs],
                                                          preferred_element_type=F32)
            m_s[rs, :] = m_new

    @pl.when(needed & full_vis)
    def _full():
        update(False)

    @pl.when(needed & jnp.logical_not(full_vis))
    def _edge():
        update(True)

    @pl.when(kk == nk - 1)
    def _fin():
        out = acc_s[...] / jnp.sum(l_s[...], axis=1, keepdims=True)
        _store_heads_compact(o_ref, [out[r * tq:(r + 1) * tq, :] for r in range(nh * nr)])


def _flash(q, k, v, *, mode, bsz, seq, nslots, nr, tq, tk, nh=1, k_off=0, v_off=0, bias=None,
           name):
    assert tq & (tq - 1) == 0
    n = q.shape[0]
    nq = seq // tq
    nkb = seq // tk
    if mode == "win":
        assert tq % tk == 0 and WINDOW % tk == 0
        steps = WINDOW // tk + tq // tk

        def krow(b, qi, kk):
            return b * nkb + jnp.maximum(qi * (tq // tk) - WINDOW // tk + kk, 0)
    else:
        steps = nkb

        def krow(b, qi, kk):
            return b * nkb + jnp.minimum(kk, (qi * tq + tq - 1) // tk)

    qspec = pl.BlockSpec((tq, nh * nr * LANES), lambda b, g, qi, kk: (b * nq + qi, g))
    in_specs = [qspec,
                pl.BlockSpec((tk, nh * LANES), lambda b, g, qi, kk: (krow(b, qi, kk), k_off + g)),
                pl.BlockSpec((tk, nh * LANES), lambda b, g, qi, kk: (krow(b, qi, kk), v_off + g))]
    args = [q, k, v]
    if mode == "sel":
        assert nh == 1
        in_specs.append(pl.BlockSpec((tq, LANES), lambda b, g, qi, kk: (b * nq + qi, g)))
        args.append(bias)
    rows = nh * nr * tq
    ow = nh * nr * NSA_HEAD_DIM
    return pl.pallas_call(
        functools.partial(_flash_kernel, mode=mode, tq=tq, tk=tk, nr=nr, nh=nh, window=WINDOW),
        grid=(bsz, nslots, nq, steps),
        in_specs=in_specs,
        out_specs=pl.BlockSpec((tq, ow), lambda b, g, qi, kk: (b * nq + qi, g)),
        out_shape=jax.ShapeDtypeStruct((n, nslots * ow), BF16),
        scratch_shapes=[pltpu.VMEM((rows, LANES), BF16), pltpu.VMEM((rows, LANES), F32),
                        pltpu.VMEM((rows, LANES), F32), pltpu.VMEM((rows, LANES), F32)],
        compiler_params=_cparams(("arbitrary",) * 4, 48),
        name=name,
    )(*args)


def _combine_kernel(oc_ref, os_ref, ow_ref, g_ref, e_ref, o_ref):
    g = g_ref[...]
    acc = None
    for br, src in enumerate((oc_ref, os_ref, ow_ref)):
        ge = jnp.dot(g, e_ref[br], preferred_element_type=F32)
        term = ge * src[...].astype(F32)
        acc = term if acc is None else acc + term
    o_ref[...] = acc.astype(o_ref.dtype)


def _combine(o_cmp, o_sel, o_win, gates, gates_col, tm=512):
    n, w = o_cmp.shape
    e = np.zeros((3, LANES, w), np.float32)
    for h in range(NSA_HEADS):
        for br in range(3):
            e[br, h * 3 + br, h * NSA_HEAD_DIM:(h + 1) * NSA_HEAD_DIM] = 1.0
    row = pl.BlockSpec((tm, w), lambda i: (i, 0))
    return pl.pallas_call(
        _combine_kernel,
        grid=(n // tm,),
        in_specs=[row, row, row, pl.BlockSpec((tm, LANES), lambda i: (i, gates_col)),
                  pl.BlockSpec(e.shape, lambda i: (0, 0, 0))],
        out_specs=row,
        out_shape=jax.ShapeDtypeStruct((n, w), BF16),
        compiler_params=_cparams(("arbitrary",)),
        name="nsa_combine",
    )(o_cmp, o_sel, o_win, gates, jnp.asarray(e, BF16))


def _merge_kernel(a_ref, wa_ref, b_ref, wb_ref, g0_ref, g1_ref, o_ref):
    ya = jnp.dot(a_ref[...], wa_ref[...], preferred_element_type=F32)
    yb = jnp.dot(b_ref[...], wb_ref[...], preferred_element_type=F32)
    o_ref[...] = (g0_ref[...].astype(F32) * ya + g1_ref[...].astype(F32) * yb).astype(o_ref.dtype)


def _merge(o_nsa, w_nsa, o_mla, w_mla, gm, tm=512, tn=512):
    n, k = o_nsa.shape
    d = w_nsa.shape[1]
    nj = d // tn
    row = pl.BlockSpec((tm, k), lambda i, j: (i, 0))
    wsp = pl.BlockSpec((k, tn), lambda i, j: (0, j))
    return pl.pallas_call(
        _merge_kernel,
        grid=(n // tm, nj),
        in_specs=[row, wsp, row, wsp,
                  pl.BlockSpec((tm, tn), lambda i, j: (i, j)),
                  pl.BlockSpec((tm, tn), lambda i, j: (i, nj + j))],
        out_specs=pl.BlockSpec((tm, tn), lambda i, j: (i, j)),
        out_shape=jax.ShapeDtypeStruct((n, d), BF16),
        compiler_params=_cparams(("arbitrary", "arbitrary")),
        name="branch_merge",
    )(o_nsa, w_nsa, o_mla, w_mla, gm, gm)


def _extract_ranked(x, nrounds):
    iota = lax.broadcasted_iota(I32, x.shape, 0).astype(F32)
    orow = lax.broadcasted_iota(I32, (nrounds, x.shape[1]), 0)
    vals = jnp.full((nrounds, x.shape[1]), -jnp.inf, F32)
    rank = jnp.full(x.shape, float(nrounds), F32)
    for k in range(nrounds):
        m = jnp.max(x, axis=0, keepdims=True)
        idx = jnp.min(jnp.where(x == m, iota, float(x.shape[0])), axis=0, keepdims=True)
        hit = iota == idx
        x = jnp.where(hit, -jnp.inf, x)
        rank = jnp.where(hit, float(k), rank)
        vals = jnp.where(orow == k, m, vals)
    return vals, rank


def _pack_rows(x, dtype):
    if dtype != jnp.uint32:
        return x.astype(dtype)
    bits = pltpu.bitcast(x.astype(jnp.bfloat16).astype(F32), jnp.uint32)
    return bits | jnp.right_shift(bits, jnp.uint32(16))


def _peer_route_kernel(q_ref, sk_ref, tb_ref, tf_ref):
    nk = PEER_N_KEYS
    sec = PEER_HEADS * nk
    tr = q_ref.shape[0]
    orow = lax.broadcasted_iota(I32, (PEER_TOPK, tr), 0)
    for h in range(PEER_HEADS):
        s = []
        for p in range(2):
            c = (2 * h + p) * nk
            s.append(lax.dot_general(sk_ref[2 * h + p], q_ref[:, c:c + nk], (((1,), (1,)), ((), ())),
                                     preferred_element_type=F32))
        s1, s2 = s
        v1, rank1 = _extract_ranked(s1, PEER_TOPK)
        v2, rank2 = _extract_ranked(s2, PEER_TOPK)
        cands = [v1[0:1, :] + v2]
        for i in range(1, SUBLANES):
            c = v1[i:i + 1, :] + v2[0:SUBLANES, :]
            cnt = PEER_TOPK // (i + 1)
            if cnt < SUBLANES:
                c = jnp.where(lax.broadcasted_iota(I32, (SUBLANES, tr), 0) < cnt, c, -jnp.inf)
            cands.append(c)
        cands.append(v1[SUBLANES:PEER_TOPK, :] + v2[0:1, :])
        top, _ = _extract_ranked(jnp.concatenate(cands, axis=0), PEER_TOPK)
        tau = top[PEER_TOPK - 1:PEER_TOPK, :]
        z = jnp.sum(jnp.exp(top - top[0:1, :]), axis=0, keepdims=True)
        cnt_rows = jnp.zeros((PEER_TOPK, tr), F32)
        for i in range(PEER_TOPK):
            ci = jnp.sum(jnp.where((v1[i:i + 1, :] + v2) >= tau, 1.0, 0.0), axis=0, keepdims=True)
            cnt_rows = jnp.where(orow == i, ci, cnt_rows)
        count = jnp.zeros((nk, tr), F32)
        for i in range(PEER_TOPK):
            count = jnp.where(rank1 == float(i), cnt_rows[i:i + 1, :], count)
        tb_ref[h * nk:(h + 1) * nk, :] = rank2.astype(tb_ref.dtype)
        tb_ref[sec + h * nk:sec + (h + 1) * nk, :] = jnp.exp(s2 - v2[0:1, :]).astype(tb_ref.dtype)
        tf_ref[h * nk:(h + 1) * nk, :] = _pack_rows(count, tf_ref.dtype)
        tf_ref[sec + h * nk:sec + (h + 1) * nk, :] = _pack_rows(jnp.exp(s1 - v1[0:1, :]) / z, tf_ref.dtype)


def _peer_route(qry, sk, tr=256):
    n, w = qry.shape
    col = pl.BlockSpec((w, tr), lambda i: (0, i))
    return pl.pallas_call(
        _peer_route_kernel,
        grid=(n // tr,),
        in_specs=[pl.BlockSpec((tr, w), lambda i: (i, 0)),
                  pl.BlockSpec(sk.shape, lambda i: (0, 0, 0))],
        out_specs=[col, col],
        out_shape=[jax.ShapeDtypeStruct((w, n), BF16),
                   jax.ShapeDtypeStruct((w, n), jnp.uint32 if BF16 == jnp.bfloat16 else F32)],
        compiler_params=_cparams(("arbitrary",), 40),
        name="peer_route",
    )(qry, sk)


PEER_CHUNK = 32


def _peer_dense_kernel(h_ref, u_ref, vt_ref, tb_ref, tf_ref, o_ref, acc_s, pt_s, *, te):
    j = pl.program_id(1)
    nk = PEER_N_KEYS
    sec = PEER_HEADS * nk
    gdt = pt_s.dtype

    @pl.when(j == 0)
    def _init():
        acc_s[...] = jnp.zeros(acc_s.shape, F32)

    def row_tile(idx):
        r = tf_ref[pl.ds(idx, 1), :]
        if tf_ref.dtype == jnp.uint32:
            t = pltpu.bitcast(jnp.broadcast_to(r, (SUBLANES, r.shape[1])), gdt)
            return jnp.concatenate([t] * (PEER_CHUNK // t.shape[0]), axis=0)
        return jnp.broadcast_to(r.astype(gdt), (PEER_CHUNK, r.shape[1]))

    for al in range(te // nk):
        a = j * (te // nk) + al
        cnt = [row_tile(h * nk + a) for h in range(PEER_HEADS)]
        e1 = [row_tile(sec + h * nk + a) for h in range(PEER_HEADS)]
        for c in range(nk // PEER_CHUNK):
            lo = c * PEER_CHUNK
            g = None
            for h in range(PEER_HEADS):
                rank2 = tb_ref[h * nk + lo:h * nk + lo + PEER_CHUNK, :]
                e2 = tb_ref[sec + h * nk + lo:sec + h * nk + lo + PEER_CHUNK, :]
                term = jnp.where(rank2 < cnt[h], e1[h] * e2, jnp.zeros((), gdt))
                g = term if g is None else g + term
            pt_s[al * nk + lo:al * nk + lo + PEER_CHUNK, :] = g
    zt = lax.dot_general(u_ref[...], h_ref[...], (((1,), (1,)), ((), ())), preferred_element_type=F32)
    acc = None
    ew = 2 * nk
    for ec in range(te // ew):
        rows = slice(ec * ew, (ec + 1) * ew)
        act = _gelu(zt[rows, :]).astype(gdt)
        part = jnp.dot(vt_ref[:, rows], pt_s[rows, :] * act, preferred_element_type=F32)
        acc = part if acc is None else acc + part
    acc_s[...] += acc

    @pl.when(j == pl.num_programs(1) - 1)
    def _fin():
        o_ref[...] = acc_s[...].T


def _peer_dense(h2, u_bf, vt_bf, tb, tf, tm=512, te=1024):
    n, d = h2.shape
    ne = u_bf.shape[0]
    w = tb.shape[0]
    return pl.pallas_call(
        functools.partial(_peer_dense_kernel, te=te),
        grid=(n // tm, ne // te),
        in_specs=[pl.BlockSpec((tm, d), lambda i, j: (i, 0)),
                  pl.BlockSpec((te, d), lambda i, j: (j, 0)),
                  pl.BlockSpec((d, te), lambda i, j: (0, j)),
                  pl.BlockSpec((w, tm), lambda i, j: (0, i)),
                  pl.BlockSpec((w, tm), lambda i, j: (0, i))],
        out_specs=pl.BlockSpec((tm, d), lambda i, j: (i, 0)),
        out_shape=jax.ShapeDtypeStruct((n, d), F32),
        scratch_shapes=[pltpu.VMEM((d, tm), F32), pltpu.VMEM((te, tm), BF16)],
        compiler_params=_cparams(("arbitrary", "arbitrary"), 56),
        name="peer_dense",
    )(h2, u_bf, vt_bf, tb, tf)


def _slots(w, n, width, off=0):
    k = w.shape[0]
    w = w.reshape(k, n, width)
    w = jnp.pad(w, ((0, 0), (0, 0), (off, LANES - width - off)))
    return w.reshape(k, n * LANES)


def _overlap_t(seq):
    n_cmp_rows = seq // CMP_STRIDE
    n_sel = seq // SEL_BLOCK
    cs = np.arange(n_cmp_rows) * CMP_STRIDE
    ce = cs + CMP_BLOCK - 1
    ss = np.arange(n_sel) * SEL_BLOCK
    ov = (cs[None, :] < ss[:, None] + SEL_BLOCK) & (ce[None, :] >= ss[:, None])
    ov[:, n_cmp_rows - 1] = False
    return jnp.asarray(ov.astype(np.float32), BF16)


def _token_mixer(h, positions, bsz, seq, w_in, cmp_pe, cmp_w1, cmp_b1, cmp_w2, mla_q_norm_g,
                 mla_w_q_up, mla_kv_norm_g, mla_w_kv_up, w_branch_nsa, w_branch_mla, w_out):
    d = h.shape[1]
    g_, hd = NSA_GROUPS, NSA_HEAD_DIM
    tab = _rope_tables(positions)
    tm = min(PROJ_TM, seq)
    tab_spec = pl.BlockSpec((tm, 4 * LANES), lambda i, j: (i, 0))

    wb = w_in.astype(BF16)
    o1 = NSA_HEADS * hd
    kv = wb[:, o1:o1 + 3 * 2 * g_ * hd].reshape(d, 3, 2, g_ * hd)
    o2 = o1 + 3 * 2 * g_ * hd
    o3 = o2 + 3 * NSA_HEADS
    mq = mla_w_q_up.shape[0]
    mkv = mla_w_kv_up.shape[0]
    o4 = o3 + mq
    o5 = o4 + mkv
    o6 = o5 + MLA_ROPE_DIM
    w_q = _slots(wb[:, :o1], NSA_HEADS, hd)
    w_kk = jnp.concatenate([_slots(kv[:, 0, 0], g_, hd), _slots(kv[:, 2, 0], g_, hd)], axis=1)
    w_ks = _slots(kv[:, 1, 0], g_, hd)
    w_vv = jnp.concatenate([_slots(kv[:, 1, 1], g_, hd), _slots(kv[:, 2, 1], g_, hd),
                            kv[:, 0, 1], jnp.zeros((d, g_ * hd), BF16)], axis=1)
    w_c = wb[:, o3:o5]
    w_gk = jnp.concatenate([jnp.pad(wb[:, o2:o3], ((0, 0), (0, LANES - 3 * NSA_HEADS))),
                            _slots(wb[:, o5:o6], 1, MLA_ROPE_DIM, MLA_ROPE_OFF)], axis=1)
    w_gm = wb[:, o6:]

    qn = _mm(h, w_q, _make_epi_rope_a(hd ** -0.5 * LOG2E), tm=tm, tn=512, out_dtype=BF16, name="proj_q",
             extras=[(tab, tab_spec)])
    kk = _mm(h, w_kk, _make_epi_rope_a(1.0), tm=tm, tn=512, out_dtype=BF16, name="proj_k",
             extras=[(tab, tab_spec)])
    ks = _mm(h, w_ks, _make_epi_rope_a(1.0, seq, tm, True), tm=tm, tn=512, out_dtype=BF16,
             name="proj_ksel", extras=[(tab, tab_spec)])
    vv = _mm(h, w_vv, _epi_plain, tm=tm, tn=512, out_dtype=BF16, name="proj_v")
    cqkv = _mm(h, w_c, _epi_plain, tm=tm, tn=mq + mkv, out_dtype=F32, name="proj_c")
    gk = _mm(h, w_gk, _epi_gate_kr, tm=tm, tn=2 * LANES, out_dtype=BF16, name="proj_gate_kr",
             extras=[(tab, tab_spec)])
    gm = _mm(h, w_gm, _epi_sigmoid, tm=tm, tn=512, out_dtype=BF16, name="proj_gmerge")

    def to16(a, width):
        a = a.reshape(bsz, seq, g_, width)[..., :hd]
        a = jnp.transpose(a, (0, 2, 1, 3))
        return a.reshape(bsz * g_, seq // CMP_STRIDE, CMP_STRIDE * hd)

    k16 = to16(kk[:, :g_ * LANES], LANES)
    v16 = to16(vv[:, 2 * g_ * LANES:2 * g_ * LANES + g_ * hd], hd)
    pe = jnp.pad(cmp_pe.reshape(2, 1, CMP_BLOCK * hd), ((0, 0), (0, SUBLANES - 1), (0, 0))).astype(BF16)
    w2p = jnp.pad(cmp_w2, ((0, 0), (0, 0), (0, LANES - hd))).astype(BF16)
    kc, vc = _compress(k16, v16, cmp_w1.astype(BF16), pe, cmp_b1.reshape(2, 1, CMP_HIDDEN), w2p)
    o_cmp, bias = _cmp_attn(qn, kc, vc, _overlap_t(seq), bsz, seq)
    o_sel = _flash(qn, ks, vv, mode="sel", bsz=bsz, seq=seq, nslots=g_, nr=NSA_R, tq=512, tk=512,
                   bias=bias, name="nsa_sel_attn")
    o_win = _flash(qn, kk, vv, mode="win", bsz=bsz, seq=seq, nslots=g_, nr=NSA_R, tq=256, tk=256,
                   k_off=g_, v_off=g_, name="nsa_win_attn")
    o_nsa = _combine(o_cmp, o_sel, o_win, gk, 0)

    qd = MLA_NOPE_DIM + MLA_ROPE_DIM
    w_qup = _slots(mla_w_q_up.astype(BF16), MLA_HEADS, qd)
    kvu = mla_w_kv_up.astype(BF16).reshape(mkv, MLA_HEADS, MLA_NOPE_DIM + MLA_V_DIM)
    w_kup = _slots(kvu[:, :, :MLA_NOPE_DIM].reshape(mkv, -1), MLA_HEADS, MLA_NOPE_DIM)
    w_vup = _slots(kvu[:, :, MLA_NOPE_DIM:].reshape(mkv, -1), MLA_HEADS, MLA_V_DIM)
    gq_spec = pl.BlockSpec((1, mq), lambda i, j: (0, 0))
    gkv_spec = pl.BlockSpec((1, mkv), lambda i, j: (0, 0))
    ckv_spec = pl.BlockSpec((tm, mkv), lambda i, j: (i, mq // mkv))
    q_mla = _mm(cqkv, w_qup, _epi_rope_b_scaled, tm=tm, tn=512,
                out_dtype=BF16, name="mla_q_up", pro=_pro_rms,
                x_spec=pl.BlockSpec((tm, mq), lambda i, j: (i, 0)),
                extras=[(mla_q_norm_g.reshape(1, mq), gq_spec), (tab, tab_spec)])
    k_mla = _mm(cqkv, w_kup, _epi_add_kr, tm=tm, tn=512, out_dtype=BF16, name="mla_k_up",
                pro=_pro_rms, x_spec=ckv_spec,
                extras=[(mla_kv_norm_g.reshape(1, mkv), gkv_spec),
                        (gk, pl.BlockSpec((tm, LANES), lambda i, j: (i, 1)))])
    v_mla = _mm(cqkv, w_vup, _epi_plain_g, tm=tm, tn=512, out_dtype=BF16, name="mla_v_up",
                pro=_pro_rms, x_spec=ckv_spec,
                extras=[(mla_kv_norm_g.reshape(1, mkv), gkv_spec)])
    o_mla = _flash(q_mla, k_mla, v_mla, mode="causal", bsz=bsz, seq=seq, nslots=MLA_HEADS // 4, nr=1,
                   nh=4, tq=512, tk=512, name="mla_attn")

    merged = _merge(o_nsa, w_branch_nsa.astype(BF16), o_mla, w_branch_mla.astype(BF16), gm)
    return _mm(merged, w_out.astype(BF16), _epi_plain, tm=tm, tn=512, out_dtype=F32, name="out_proj")


def _peer(h2, peer_w_q, peer_sub_keys, peer_u, peer_v):
    qry = _mm(h2, peer_w_q.astype(BF16), _epi_plain, tm=min(PROJ_TM, h2.shape[0]), tn=512,
              out_dtype=BF16, name="peer_q")
    sk = peer_sub_keys.astype(BF16).reshape(2 * PEER_HEADS, PEER_N_KEYS, -1)
    tb, tf = _peer_route(qry, sk)
    return _peer_dense(h2, peer_u.astype(BF16), jnp.transpose(peer_v).astype(BF16), tb, tf)


def _block(x, c, positions, ada_w, ada_b, attn_pre_g, attn_post_g, w_in, cmp_pe, cmp_w1, cmp_b1,
           cmp_w2, mla_q_norm_g, mla_w_q_up, mla_kv_norm_g, mla_w_kv_up, w_branch_nsa, w_branch_mla,
           w_out, ffn_pre_g, ffn_post_g, peer_w_q, peer_sub_keys, peer_u, peer_v):
    bsz, seq, d = x.shape
    x2 = x.reshape(bsz * seq, d)
    mod = _adaln(c, ada_w, ada_b)
    sh_a, sc_a, g_a, sh_f, sc_f, g_f = [mod[:, i * d:(i + 1) * d] for i in range(6)]

    h = _normmod(x2, attn_pre_g, sc_a, sh_a, seq)
    y_attn = _token_mixer(h, positions, bsz, seq, w_in, cmp_pe, cmp_w1, cmp_b1, cmp_w2, mla_q_norm_g,
                          mla_w_q_up, mla_kv_norm_g, mla_w_kv_up, w_branch_nsa, w_branch_mla, w_out)
    x1 = _norm_res(x2, y_attn, attn_post_g, g_a, seq)

    h2 = _normmod(x1, ffn_pre_g, sc_f, sh_f, seq)
    y_ffn = _peer(h2, peer_w_q, peer_sub_keys, peer_u, peer_v)
    out = _norm_res(x1, y_ffn, ffn_post_g, g_f, seq)
    return out.reshape(bsz, seq, d)


def kernel(x, c, positions, ada_w, ada_b, attn_pre_g, attn_post_g, w_in, cmp_pe, cmp_w1, cmp_b1,
           cmp_w2, mla_q_norm_g, mla_w_q_up, mla_kv_norm_g, mla_w_kv_up, w_branch_nsa, w_branch_mla,
           w_out, ffn_pre_g, ffn_post_g, peer_w_q, peer_sub_keys, peer_u, peer_v):
    depth = ada_w.shape[0]
    for l in range(depth):
        x = _block(x, c, positions, ada_w[l], ada_b[l], attn_pre_g[l], attn_post_g[l], w_in[l],
                   cmp_pe[l], cmp_w1[l], cmp_b1[l], cmp_w2[l], mla_q_norm_g[l], mla_w_q_up[l],
                   mla_kv_norm_g[l], mla_w_kv_up[l], w_branch_nsa[l], w_branch_mla[l], w_out[l],
                   ffn_pre_g[l], ffn_post_g[l], peer_w_q[l], peer_sub_keys[l], peer_u[l], peer_v[l])
    return x
```

```python
import functools

import numpy as np
import jax
import jax.numpy as jnp
from jax import lax
from jax.experimental import pallas as pl
from jax.experimental.pallas import tpu as pltpu

F32 = jnp.float32
BF16 = jnp.bfloat16
I32 = jnp.int32

LANES = 128
SUBLANES = 8

NSA_HEADS = 16
NSA_GROUPS = 4
NSA_R = NSA_HEADS // NSA_GROUPS
NSA_HEAD_DIM = 64
NSA_ROPE_DIM = NSA_HEAD_DIM // 4
CMP_BLOCK = 32
CMP_STRIDE = 16
CMP_HIDDEN = 256
SEL_BLOCK = 64
SEL_SHIFT = 6
SEL_TOPK = 16
WINDOW = 512
MLA_HEADS = 16
MLA_NOPE_DIM = 64
MLA_ROPE_DIM = 32
MLA_V_DIM = 64
PEER_HEADS = 8
PEER_N_KEYS = 128
PEER_TOPK = 16
ROPE_THETA = 500000.0
NORM_EPS = 1e-6
NEG = -1e30
FORCE_BONUS = 1e4
LOG2E = 1.4426950408889634
MLA_ROPE_OFF = MLA_NOPE_DIM
PROJ_TM = 1024


def _cparams(sem, vmem_mb=None):
    kw = dict(dimension_semantics=sem)
    if vmem_mb is not None:
        kw["vmem_limit_bytes"] = vmem_mb * 1024 * 1024
    return pltpu.CompilerParams(**kw)


def _rms(x, g):
    return x * lax.rsqrt(jnp.mean(x * x, axis=-1, keepdims=True) + NORM_EPS) * g


def _gelu(x):
    return jax.nn.gelu(x, approximate=True)


def _adaln_kernel(c_ref, w_ref, b_ref, o_ref):
    c = c_ref[...]
    s = (c * jax.nn.sigmoid(c)).astype(BF16)
    o_ref[...] = jnp.dot(s, w_ref[...].astype(BF16), preferred_element_type=F32) + b_ref[...]


def _adaln(c, w, b, tn=1024):
    bsz, d = c.shape
    n = w.shape[1]
    cp = jnp.pad(c, ((0, SUBLANES - bsz), (0, 0)))
    out = pl.pallas_call(
        _adaln_kernel,
        grid=(n // tn,),
        in_specs=[pl.BlockSpec((SUBLANES, d), lambda j: (0, 0)),
                  pl.BlockSpec((d, tn), lambda j: (0, j)),
                  pl.BlockSpec((1, tn), lambda j: (0, j))],
        out_specs=pl.BlockSpec((SUBLANES, tn), lambda j: (0, j)),
        out_shape=jax.ShapeDtypeStruct((SUBLANES, n), F32),
        compiler_params=_cparams(("arbitrary",), 40),
        name="adaln",
    )(cp, w, b.reshape(1, n))
    return out[:bsz]


def _rope_table_kernel(pos_ref, inv_ref, sgn_ref, o_ref):
    pos = pos_ref[...]
    for k in range(2):
        ang = pos * inv_ref[k:k + 1, :]
        o_ref[:, (2 * k) * LANES:(2 * k + 1) * LANES] = jnp.cos(ang)
        o_ref[:, (2 * k + 1) * LANES:(2 * k + 2) * LANES] = jnp.sin(ang) * sgn_ref[k:k + 1, :]


def _rope_tables(positions, tm=512):
    n = positions.size
    half_a = NSA_ROPE_DIM // 2
    half_b = MLA_ROPE_DIM // 2
    inv_a = ROPE_THETA ** (-jnp.arange(half_a, dtype=F32) * (2.0 / NSA_ROPE_DIM))
    inv_b = ROPE_THETA ** (-jnp.arange(half_b, dtype=F32) * (2.0 / MLA_ROPE_DIM))
    inv = jnp.zeros((2, LANES), F32)
    inv = inv.at[0, 0:half_a].set(inv_a).at[0, half_a:2 * half_a].set(inv_a)
    o = MLA_ROPE_OFF
    inv = inv.at[1, o:o + half_b].set(inv_b).at[1, o + half_b:o + 2 * half_b].set(inv_b)
    sgn = np.zeros((2, LANES), np.float32)
    sgn[0, 0:half_a] = -1.0
    sgn[0, half_a:2 * half_a] = 1.0
    sgn[1, o:o + half_b] = -1.0
    sgn[1, o + half_b:o + 2 * half_b] = 1.0
    posf = jnp.broadcast_to(positions.reshape(n, 1).astype(F32), (n, LANES))
    return pl.pallas_call(
        _rope_table_kernel,
        grid=(n // tm,),
        in_specs=[pl.BlockSpec((tm, LANES), lambda i: (i, 0)),
                  pl.BlockSpec((2, LANES), lambda i: (0, 0)),
                  pl.BlockSpec((2, LANES), lambda i: (0, 0))],
        out_specs=pl.BlockSpec((tm, 4 * LANES), lambda i: (i, 0)),
        out_shape=jax.ShapeDtypeStruct((n, 4 * LANES), F32),
        compiler_params=_cparams(("arbitrary",)),
        name="rope_tables",
    )(posf, inv, jnp.asarray(sgn))


def _rope_slot(a, cos, sin, half, off):
    lane = lax.broadcasted_iota(I32, (1, LANES), 1)
    first = (lane - off) < half
    partner = jnp.where(first, pltpu.roll(a, LANES - half, 1), pltpu.roll(a, half, 1))
    return a * cos + partner * sin


def _normmod_kernel(x_ref, g_ref, sc_ref, sh_ref, o_ref):
    y = _rms(x_ref[...], g_ref[...])
    o_ref[...] = (y * (1.0 + sc_ref[0]) + sh_ref[0]).astype(o_ref.dtype)


def _normmod(x, g, sc, sh, seq, tm=512):
    n, d = x.shape
    bsz = sc.shape[0]
    bspec = pl.BlockSpec((1, 1, d), lambda i: ((i * tm) // seq, 0, 0))
    return pl.pallas_call(
        _normmod_kernel,
        grid=(n // tm,),
        in_specs=[pl.BlockSpec((tm, d), lambda i: (i, 0)),
                  pl.BlockSpec((1, d), lambda i: (0, 0)), bspec, bspec],
        out_specs=pl.BlockSpec((tm, d), lambda i: (i, 0)),
        out_shape=jax.ShapeDtypeStruct((n, d), BF16),
        compiler_params=_cparams(("arbitrary",)),
        name="normmod",
    )(x, g.reshape(1, d), sc.reshape(bsz, 1, d), sh.reshape(bsz, 1, d))


def _norm_res_kernel(x_ref, y_ref, g_ref, ga_ref, o_ref):
    o_ref[...] = x_ref[...] + ga_ref[0] * _rms(y_ref[...], g_ref[...])


def _norm_res(x, y, g, gate, seq, tm=512):
    n, d = x.shape
    bsz = gate.shape[0]
    row = pl.BlockSpec((tm, d), lambda i: (i, 0))
    return pl.pallas_call(
        _norm_res_kernel,
        grid=(n // tm,),
        in_specs=[row, row, pl.BlockSpec((1, d), lambda i: (0, 0)),
                  pl.BlockSpec((1, 1, d), lambda i: ((i * tm) // seq, 0, 0))],
        out_specs=row,
        out_shape=jax.ShapeDtypeStruct((n, d), F32),
        compiler_params=_cparams(("arbitrary",)),
        name="norm_res",
    )(x, y, g.reshape(1, d), gate.reshape(bsz, 1, d))


def _mm(x, w, epi, *, tm, tn, out_dtype, name, extras=(), pro=None, x_spec=None, vmem_mb=None):
    m = x.shape[0]
    k, nc = w.shape
    if x_spec is None:
        x_spec = pl.BlockSpec((tm, k), lambda i, j: (i, 0))
    in_specs = [x_spec, pl.BlockSpec((k, tn), lambda i, j: (0, j))]
    args = [x, w]
    for arr, spec in extras:
        in_specs.append(spec)
        args.append(arr)

    def kern(x_ref, w_ref, *rest):
        o_ref = rest[-1]
        ex = rest[:-1]
        xv = x_ref[...]
        if pro is not None:
            xv = pro(xv, *ex)
        acc = jnp.dot(xv, w_ref[...], preferred_element_type=F32)
        epi(acc, o_ref, *ex)

    return pl.pallas_call(
        kern,
        grid=(m // tm, nc // tn),
        in_specs=in_specs,
        out_specs=pl.BlockSpec((tm, tn), lambda i, j: (i, j)),
        out_shape=jax.ShapeDtypeStruct((m, nc), out_dtype),
        compiler_params=_cparams(("arbitrary", "arbitrary"), vmem_mb),
        name=name,
    )(*args)


def _epi_plain(acc, o_ref, *ex):
    o_ref[...] = acc.astype(o_ref.dtype)


def _epi_sigmoid(acc, o_ref, *ex):
    o_ref[...] = jax.nn.sigmoid(acc).astype(o_ref.dtype)


def _make_epi_rope_a(scale, seq=None, tm=None, onehot=False):
    half = NSA_ROPE_DIM // 2

    def epi(acc, o_ref, tab_ref):
        cos = tab_ref[:, 0:LANES]
        sin = tab_ref[:, LANES:2 * LANES]
        if onehot:
            base = lax.rem(pl.program_id(0) * tm, seq)
            t = base + lax.broadcasted_iota(I32, (acc.shape[0], LANES), 0)
            lane = lax.broadcasted_iota(I32, (acc.shape[0], LANES), 1)
            hot = jnp.where(lane - SEL_BLOCK == jnp.right_shift(t, SEL_SHIFT), 1.0, 0.0)
        for s in range(acc.shape[1] // LANES):
            r = _rope_slot(acc[:, s * LANES:(s + 1) * LANES], cos, sin, half, 0)
            if scale != 1.0:
                r = r * scale
            if onehot:
                r = r + hot
            o_ref[:, s * LANES:(s + 1) * LANES] = r.astype(o_ref.dtype)
    return epi


def _epi_rope_b_scaled(acc, o_ref, g_ref, tab_ref):
    cos = tab_ref[:, 2 * LANES:3 * LANES]
    sin = tab_ref[:, 3 * LANES:4 * LANES]
    scale = (MLA_NOPE_DIM + MLA_ROPE_DIM) ** -0.5 * LOG2E
    for s in range(acc.shape[1] // LANES):
        r = _rope_slot(acc[:, s * LANES:(s + 1) * LANES], cos, sin, MLA_ROPE_DIM // 2, MLA_ROPE_OFF)
        o_ref[:, s * LANES:(s + 1) * LANES] = (r * scale).astype(o_ref.dtype)


def _epi_gate_kr(acc, o_ref, tab_ref):
    o_ref[:, 0:LANES] = jax.nn.sigmoid(acc[:, 0:LANES]).astype(o_ref.dtype)
    cos = tab_ref[:, 2 * LANES:3 * LANES]
    sin = tab_ref[:, 3 * LANES:4 * LANES]
    r = _rope_slot(acc[:, LANES:2 * LANES], cos, sin, MLA_ROPE_DIM // 2, MLA_ROPE_OFF)
    o_ref[:, LANES:2 * LANES] = r.astype(o_ref.dtype)


def _pro_rms(xv, g_ref, *ex):
    return _rms(xv, g_ref[...]).astype(BF16)


def _epi_add_kr(acc, o_ref, g_ref, kr_ref):
    kr = kr_ref[...].astype(F32)
    for s in range(acc.shape[1] // LANES):
        o_ref[:, s * LANES:(s + 1) * LANES] = (acc[:, s * LANES:(s + 1) * LANES] + kr).astype(o_ref.dtype)


def _epi_plain_g(acc, o_ref, g_ref):
    o_ref[...] = acc.astype(o_ref.dtype)


def _compress_kernel(k16_ref, v16_ref, w1_ref, pe_ref, b1_ref, w2_ref, kc_ref, vc_ref):
    half = w1_ref.shape[1] // 2
    for which, (src, dst) in enumerate(((k16_ref, kc_ref), (v16_ref, vc_ref))):
        xb = src[...]
        a = jnp.dot(xb, w1_ref[which, 0:half, :], preferred_element_type=F32)
        b = jnp.dot(xb, w1_ref[which, half:2 * half, :], preferred_element_type=F32)
        c = jnp.dot(pe_ref[which], w1_ref[which], preferred_element_type=F32)[0:1, :] + b1_ref[which]
        rows = a.shape[0]
        pre = a + pltpu.roll(b, rows - 1, 0) + c
        hid = _gelu(pre).astype(BF16)
        dst[...] = jnp.dot(hid, w2_ref[which], preferred_element_type=F32).astype(dst.dtype)


def _compress(k16, v16, w1, pe, b1, w2p):
    bg, rows, feat = k16.shape
    blk = pl.BlockSpec((None, rows, feat), lambda i: (i, 0, 0))
    full = lambda a: pl.BlockSpec(a.shape, lambda i: (0,) * a.ndim)
    out = pl.BlockSpec((None, rows, LANES), lambda i: (i, 0, 0))
    return pl.pallas_call(
        _compress_kernel,
        grid=(bg,),
        in_specs=[blk, blk, full(w1), full(pe), full(b1), full(w2p)],
        out_specs=[out, out],
        out_shape=[jax.ShapeDtypeStruct((bg, rows, LANES), BF16)] * 2,
        compiler_params=_cparams(("arbitrary",)),
        name="nsa_compress",
    )(k16, v16, w1, pe, b1, w2p)


def _row_bcast(col, width):
    if width % LANES:
        return jnp.broadcast_to(col, (col.shape[0], width))
    tile = jnp.broadcast_to(col, (col.shape[0], LANES))
    return jnp.concatenate([tile] * (width // LANES), axis=1)


def _cmp_attn_kernel(q_ref, kc_ref, vc_ref, ovt_ref, o_ref, bias_ref, *, tq, n_sel):
    qi = pl.program_id(2)
    ncmp = kc_ref.shape[0]
    t_row = qi * tq + lax.broadcasted_iota(I32, (tq, ncmp), 0)
    n_col = lax.broadcasted_iota(I32, (tq, ncmp), 1)
    cmask = (n_col * CMP_STRIDE + (CMP_BLOCK - 1)) <= t_row
    kc = kc_ref[...]
    vc = vc_ref[...]
    imp_t = jnp.zeros((n_sel, tq), F32)
    outs = []
    for r in range(NSA_R):
        q = q_ref[:, r * LANES:(r + 1) * LANES]
        s = lax.dot_general(q, kc, (((1,), (1,)), ((), ())), preferred_element_type=F32)
        s = jnp.where(cmask, s, NEG)
        e = jnp.exp2(s - _row_bcast(jnp.max(s, axis=1, keepdims=True), ncmp))
        p = e / _row_bcast(jnp.sum(e, axis=1, keepdims=True), ncmp)
        p = jnp.where(cmask, p, 0.0).astype(BF16)
        outs.append(jnp.dot(p, vc, preferred_element_type=F32))
        imp_t = imp_t + lax.dot_general(ovt_ref[...], p, (((1,), (1,)), ((), ())),
                                        preferred_element_type=F32)
    _store_heads_compact(o_ref, outs)
    blk = lax.broadcasted_iota(I32, (n_sel, tq), 0)
    t = qi * tq + lax.broadcasted_iota(I32, (n_sel, tq), 1)
    tb = jnp.right_shift(t, SEL_SHIFT)
    forced = (blk == 0) | (blk == tb) | (blk == tb - 1)
    valid = blk * SEL_BLOCK <= t
    x = jnp.where(valid, imp_t + jnp.where(forced, FORCE_BONUS, 0.0), NEG)
    sel = jnp.zeros((n_sel, tq), F32)
    for _ in range(min(SEL_TOPK, n_sel)):
        m = jnp.max(x, axis=0, keepdims=True)
        idx = jnp.min(jnp.where(x == m, blk, n_sel), axis=0, keepdims=True)
        hit = blk == idx
        sel = jnp.where(hit, 1.0, sel)
        x = jnp.where(hit, -jnp.inf, x)
    bias_t = jnp.where(sel > 0.5, 0.0, NEG)
    parts = [jnp.zeros((SEL_BLOCK, tq), F32), bias_t]
    if LANES - SEL_BLOCK - n_sel > 0:
        parts.append(jnp.zeros((LANES - SEL_BLOCK - n_sel, tq), F32))
    full_t = jnp.concatenate(parts, axis=0)
    bias_ref[...] = full_t.T.astype(bias_ref.dtype)


def _cmp_attn(qn, kc, vc, ovt, bsz, seq, tq=256):
    n = qn.shape[0]
    nq = seq // tq
    n_sel = seq // SEL_BLOCK
    ncmp = kc.shape[1]
    qspec = pl.BlockSpec((tq, NSA_R * LANES), lambda b, g, qi: (b * nq + qi, g))
    kspec = pl.BlockSpec((None, ncmp, LANES), lambda b, g, qi: (b * NSA_GROUPS + g, 0, 0))
    return pl.pallas_call(
        functools.partial(_cmp_attn_kernel, tq=tq, n_sel=n_sel),
        grid=(bsz, NSA_GROUPS, nq),
        in_specs=[qspec, kspec, kspec, pl.BlockSpec(ovt.shape, lambda b, g, qi: (0, 0))],
        out_specs=[pl.BlockSpec((tq, NSA_R * NSA_HEAD_DIM), lambda b, g, qi: (b * nq + qi, g)),
                   pl.BlockSpec((tq, LANES), lambda b, g, qi: (b * nq + qi, g))],
        out_shape=[jax.ShapeDtypeStruct((n, NSA_HEADS * NSA_HEAD_DIM), BF16),
                   jax.ShapeDtypeStruct((n, NSA_GROUPS * LANES), BF16)],
        compiler_params=_cparams(("arbitrary",) * 3),
        name="nsa_cmp_attn",
    )(qn, kc, vc, ovt)


def _store_heads_compact(o_ref, heads):
    hd = NSA_HEAD_DIM
    for p in range(len(heads) // 2):
        pair = jnp.concatenate([heads[2 * p][:, 0:hd], heads[2 * p + 1][:, 0:hd]], axis=1)
        o_ref[:, p * LANES:(p + 1) * LANES] = pair.astype(o_ref.dtype)


def _flash_kernel(*refs, mode, tq, tk, nr, nh, window):
    if mode == "sel":
        q_ref, k_ref, v_ref, bias_ref, o_ref, qs, m_s, l_s, acc_s = refs
    else:
        q_ref, k_ref, v_ref, o_ref, qs, m_s, l_s, acc_s = refs
    qi = pl.program_id(2)
    kk = pl.program_id(3)
    nk = pl.num_programs(3)
    rows = nr * tq

    @pl.when(kk == 0)
    def _init():
        for r in range(nh * nr):
            qr = q_ref[:, r * LANES:(r + 1) * LANES]
            if mode == "sel":
                qr = qr + bias_ref[:, (r // nr) * LANES:(r // nr + 1) * LANES]
            qs[r * tq:(r + 1) * tq, :] = qr
        m_s[...] = jnp.full(m_s.shape, -jnp.inf, F32)
        l_s[...] = jnp.zeros(l_s.shape, F32)
        acc_s[...] = jnp.zeros(acc_s.shape, F32)

    if mode == "win":
        kidx = qi * (tq // tk) - window // tk + kk
        needed = kidx >= 0
        full_vis = (kidx * tk + tk - 1 <= qi * tq) & (qi * tq + tq - 1 - kidx * tk < window)
    else:
        kidx = kk
        needed = kk * tk <= qi * tq + tq - 1
        full_vis = kk * tk + tk - 1 <= qi * tq

    nt = tk // LANES

    def update(masked):
        if masked:
            rel = (lax.broadcasted_iota(I32, (rows, tk), 0) & (tq - 1)) - lax.broadcasted_iota(I32, (rows, tk), 1)
            off = kidx * tk - qi * tq
            vis = rel >= off
            if mode == "win":
                vis = vis & (rel < off + window)
        for hh in range(nh):
            rs = slice(hh * rows, (hh + 1) * rows)
            cs = slice(hh * LANES, (hh + 1) * LANES)
            s = lax.dot_general(qs[rs, :], k_ref[:, cs], (((1,), (1,)), ((), ())),
                                preferred_element_type=F32)
            if masked:
                s = jnp.where(vis, s, NEG)
            m_prev = m_s[rs, :]
            m_new = jnp.maximum(m_prev, jnp.max(s, axis=1, keepdims=True))
            p = jnp.exp2(s - jnp.concatenate([m_new] * nt, axis=1))
            alpha = jnp.exp2(m_prev - m_new)
            psum = p[:, 0:LANES]
            for c in range(1, nt):
                psum = psum + p[:, c * LANES:(c + 1) * LANES]
            l_s[rs, :] = alpha * l_s[rs, :] + psum
            acc_s[rs, :] = alpha * acc_s[rs, :] + jnp.dot(p.astype(BF16), v_ref[:, cs],
                                                          preferred_element_type=F32)
            m_s[rs, :] = m_new

    @pl.when(needed & full_vis)
    def _full():
        update(False)

    @pl.when(needed & jnp.logical_not(full_vis))
    def _edge():
        update(True)

    @pl.when(kk == nk - 1)
    def _fin():
        out = acc_s[...] / jnp.sum(l_s[...], axis=1, keepdims=True)
        _store_heads_compact(o_ref, [out[r * tq:(r + 1) * tq, :] for r in range(nh * nr)])


def _flash(q, k, v, *, mode, bsz, seq, nslots, nr, tq, tk, nh=1, k_off=0, v_off=0, bias=None,
           name):
    assert tq & (tq - 1) == 0
    n = q.shape[0]
    nq = seq // tq
    nkb = seq // tk
    if mode == "win":
        assert tq % tk == 0 and WINDOW % tk == 0
        steps = WINDOW // tk + tq // tk

        def krow(b, qi, kk):
            return b * nkb + jnp.maximum(qi * (tq // tk) - WINDOW // tk + kk, 0)
    else:
        steps = nkb

        def krow(b, qi, kk):
            return b * nkb + jnp.minimum(kk, (qi * tq + tq - 1) // tk)

    qspec = pl.BlockSpec((tq, nh * nr * LANES), lambda b, g, qi, kk: (b * nq + qi, g))
    in_specs = [qspec,
                pl.BlockSpec((tk, nh * LANES), lambda b, g, qi, kk: (krow(b, qi, kk), k_off // nh + g)),
                pl.BlockSpec((tk, nh * LANES), lambda b, g, qi, kk: (krow(b, qi, kk), v_off // nh + g))]
    args = [q, k, v]
    if mode == "sel":
        in_specs.append(pl.BlockSpec((tq, nh * LANES), lambda b, g, qi, kk: (b * nq + qi, g)))
        args.append(bias)
    rows = nh * nr * tq
    ow = nh * nr * NSA_HEAD_DIM
    return pl.pallas_call(
        functools.partial(_flash_kernel, mode=mode, tq=tq, tk=tk, nr=nr, nh=nh, window=WINDOW),
        grid=(bsz, nslots, nq, steps),
        in_specs=in_specs,
        out_specs=pl.BlockSpec((tq, ow), lambda b, g, qi, kk: (b * nq + qi, g)),
        out_shape=jax.ShapeDtypeStruct((n, nslots * ow), BF16),
        scratch_shapes=[pltpu.VMEM((rows, LANES), BF16), pltpu.VMEM((rows, LANES), F32),
                        pltpu.VMEM((rows, LANES), F32), pltpu.VMEM((rows, LANES), F32)],
        compiler_params=_cparams(("arbitrary",) * 4, 48),
        name=name,
    )(*args)


def _combine_kernel(oc_ref, os_ref, ow_ref, g_ref, e_ref, o_ref):
    g = g_ref[...]
    acc = None
    for br, src in enumerate((oc_ref, os_ref, ow_ref)):
        ge = jnp.dot(g, e_ref[br], preferred_element_type=F32)
        term = ge * src[...].astype(F32)
        acc = term if acc is None else acc + term
    o_ref[...] = acc.astype(o_ref.dtype)


def _combine(o_cmp, o_sel, o_win, gates, gates_col, tm=512):
    n, w = o_cmp.shape
    e = np.zeros((3, LANES, w), np.float32)
    for h in range(NSA_HEADS):
        for br in range(3):
            e[br, h * 3 + br, h * NSA_HEAD_DIM:(h + 1) * NSA_HEAD_DIM] = 1.0
    row = pl.BlockSpec((tm, w), lambda i: (i, 0))
    return pl.pallas_call(
        _combine_kernel,
        grid=(n // tm,),
        in_specs=[row, row, row, pl.BlockSpec((tm, LANES), lambda i: (i, gates_col)),
                  pl.BlockSpec(e.shape, lambda i: (0, 0, 0))],
        out_specs=row,
        out_shape=jax.ShapeDtypeStruct((n, w), BF16),
        compiler_params=_cparams(("arbitrary",)),
        name="nsa_combine",
    )(o_cmp, o_sel, o_win, gates, jnp.asarray(e, BF16))


def _merge_kernel(a_ref, wa_ref, b_ref, wb_ref, g0_ref, g1_ref, o_ref):
    ya = jnp.dot(a_ref[...], wa_ref[...], preferred_element_type=F32)
    yb = jnp.dot(b_ref[...], wb_ref[...], preferred_element_type=F32)
    o_ref[...] = (g0_ref[...].astype(F32) * ya + g1_ref[...].astype(F32) * yb).astype(o_ref.dtype)


def _merge(o_nsa, w_nsa, o_mla, w_mla, gm, tm=512, tn=512):
    n, k = o_nsa.shape
    d = w_nsa.shape[1]
    nj = d // tn
    row = pl.BlockSpec((tm, k), lambda i, j: (i, 0))
    wsp = pl.BlockSpec((k, tn), lambda i, j: (0, j))
    return pl.pallas_call(
        _merge_kernel,
        grid=(n // tm, nj),
        in_specs=[row, wsp, row, wsp,
                  pl.BlockSpec((tm, tn), lambda i, j: (i, j)),
                  pl.BlockSpec((tm, tn), lambda i, j: (i, nj + j))],
        out_specs=pl.BlockSpec((tm, tn), lambda i, j: (i, j)),
        out_shape=jax.ShapeDtypeStruct((n, d), BF16),
        compiler_params=_cparams(("arbitrary", "arbitrary")),
        name="branch_merge",
    )(o_nsa, w_nsa, o_mla, w_mla, gm, gm)


def _extract_ranked(x, nrounds):
    iota = lax.broadcasted_iota(I32, x.shape, 0).astype(F32)
    orow = lax.broadcasted_iota(I32, (nrounds, x.shape[1]), 0)
    vals = jnp.full((nrounds, x.shape[1]), -jnp.inf, F32)
    rank = jnp.full(x.shape, float(nrounds), F32)
    for k in range(nrounds):
        m = jnp.max(x, axis=0, keepdims=True)
        idx = jnp.min(jnp.where(x == m, iota, float(x.shape[0])), axis=0, keepdims=True)
        hit = iota == idx
        x = jnp.where(hit, -jnp.inf, x)
        rank = jnp.where(hit, float(k), rank)
        vals = jnp.where(orow == k, m, vals)
    return vals, rank


def _pack_rows(x, dtype):
    if dtype != jnp.uint32:
        return x.astype(dtype)
    bits = pltpu.bitcast(x.astype(jnp.bfloat16).astype(F32), jnp.uint32)
    return bits | jnp.right_shift(bits, jnp.uint32(16))


def _peer_route_kernel(q_ref, sk_ref, tb_ref, tf_ref):
    nk = PEER_N_KEYS
    sec = PEER_HEADS * nk
    tr = q_ref.shape[0]
    orow = lax.broadcasted_iota(I32, (PEER_TOPK, tr), 0)
    for h in range(PEER_HEADS):
        s = []
        for p in range(2):
            c = (2 * h + p) * nk
            s.append(lax.dot_general(sk_ref[2 * h + p], q_ref[:, c:c + nk], (((1,), (1,)), ((), ())),
                                     preferred_element_type=F32))
        s1, s2 = s
        v1, rank1 = _extract_ranked(s1, PEER_TOPK)
        v2, rank2 = _extract_ranked(s2, PEER_TOPK)
        cands = [v1[0:1, :] + v2]
        for i in range(1, SUBLANES):
            c = v1[i:i + 1, :] + v2[0:SUBLANES, :]
            cnt = PEER_TOPK // (i + 1)
            if cnt < SUBLANES:
                c = jnp.where(lax.broadcasted_iota(I32, (SUBLANES, tr), 0) < cnt, c, -jnp.inf)
            cands.append(c)
        cands.append(v1[SUBLANES:PEER_TOPK, :] + v2[0:1, :])
        top, _ = _extract_ranked(jnp.concatenate(cands, axis=0), PEER_TOPK)
        tau = top[PEER_TOPK - 1:PEER_TOPK, :]
        z = jnp.sum(jnp.exp(top - top[0:1, :]), axis=0, keepdims=True)
        cnt_rows = jnp.zeros((PEER_TOPK, tr), F32)
        for i in range(PEER_TOPK):
            ci = jnp.sum(jnp.where((v1[i:i + 1, :] + v2) >= tau, 1.0, 0.0), axis=0, keepdims=True)
            cnt_rows = jnp.where(orow == i, ci, cnt_rows)
        count = jnp.zeros((nk, tr), F32)
        for i in range(PEER_TOPK):
            count = jnp.where(rank1 == float(i), cnt_rows[i:i + 1, :], count)
        tb_ref[h * nk:(h + 1) * nk, :] = rank2.astype(tb_ref.dtype)
        tb_ref[sec + h * nk:sec + (h + 1) * nk, :] = jnp.exp(s2 - v2[0:1, :]).astype(tb_ref.dtype)
        tf_ref[h * nk:(h + 1) * nk, :] = _pack_rows(count, tf_ref.dtype)
        tf_ref[sec + h * nk:sec + (h + 1) * nk, :] = _pack_rows(jnp.exp(s1 - v1[0:1, :]) / z, tf_ref.dtype)


def _peer_route(qry, sk, tr=256):
    n, w = qry.shape
    col = pl.BlockSpec((w, tr), lambda i: (0, i))
    return pl.pallas_call(
        _peer_route_kernel,
        grid=(n // tr,),
        in_specs=[pl.BlockSpec((tr, w), lambda i: (i, 0)),
                  pl.BlockSpec(sk.shape, lambda i: (0, 0, 0))],
        out_specs=[col, col],
        out_shape=[jax.ShapeDtypeStruct((w, n), BF16),
                   jax.ShapeDtypeStruct((w, n), jnp.uint32 if BF16 == jnp.bfloat16 else F32)],
        compiler_params=_cparams(("arbitrary",), 40),
        name="peer_route",
    )(qry, sk)


PEER_CHUNK = 32
PEER_TOKEN_GROUPS = 1


def _peer_dense_kernel(h_ref, u_ref, vt_ref, tb_ref, tf_ref, o_ref, acc_s, pt_s, *, te):
    j = pl.program_id(1)
    nk = PEER_N_KEYS
    sec = PEER_HEADS * nk
    gdt = pt_s.dtype

    @pl.when(j == 0)
    def _init():
        acc_s[...] = jnp.zeros(acc_s.shape, F32)

    def row_tile(idx, cols):
        r = tf_ref[pl.ds(idx, 1), cols]
        if tf_ref.dtype == jnp.uint32:
            t = pltpu.bitcast(jnp.broadcast_to(r, (SUBLANES, r.shape[1])), gdt)
            return jnp.concatenate([t] * (PEER_CHUNK // t.shape[0]), axis=0)
        return jnp.broadcast_to(r.astype(gdt), (PEER_CHUNK, r.shape[1]))

    tm = h_ref.shape[0]
    tg = tm // PEER_TOKEN_GROUPS
    for grp in range(PEER_TOKEN_GROUPS):
        cols = slice(grp * tg, (grp + 1) * tg)
        zt = lax.dot_general(u_ref[...], h_ref[cols, :], (((1,), (1,)), ((), ())),
                             preferred_element_type=F32)
        for al in range(te // nk):
            a = j * (te // nk) + al
            cnt = [row_tile(h * nk + a, cols) for h in range(PEER_HEADS)]
            e1 = [row_tile(sec + h * nk + a, cols) for h in range(PEER_HEADS)]
            for c in range(nk // PEER_CHUNK):
                lo = c * PEER_CHUNK
                g = None
                for h in range(PEER_HEADS):
                    rank2 = tb_ref[h * nk + lo:h * nk + lo + PEER_CHUNK, cols]
                    e2 = tb_ref[sec + h * nk + lo:sec + h * nk + lo + PEER_CHUNK, cols]
                    term = jnp.where(rank2 < cnt[h], e1[h] * e2, jnp.zeros((), gdt))
                    g = term if g is None else g + term
                r0 = al * nk + lo
                pt_s[r0:r0 + PEER_CHUNK, cols] = g * _gelu(zt[r0:r0 + PEER_CHUNK, :]).astype(gdt)
        acc_s[:, cols] += jnp.dot(vt_ref[...], pt_s[:, cols], preferred_element_type=F32)

    @pl.when(j == pl.num_programs(1) - 1)
    def _fin():
        o_ref[...] = acc_s[...].T


def _peer_dense(h2, u_bf, vt_bf, tb, tf, tm=512, te=1024):
    n, d = h2.shape
    ne = u_bf.shape[0]
    w = tb.shape[0]
    return pl.pallas_call(
        functools.partial(_peer_dense_kernel, te=te),
        grid=(n // tm, ne // te),
        in_specs=[pl.BlockSpec((tm, d), lambda i, j: (i, 0)),
                  pl.BlockSpec((te, d), lambda i, j: (j, 0)),
                  pl.BlockSpec((d, te), lambda i, j: (0, j)),
                  pl.BlockSpec((w, tm), lambda i, j: (0, i)),
                  pl.BlockSpec((w, tm), lambda i, j: (0, i))],
        out_specs=pl.BlockSpec((tm, d), lambda i, j: (i, 0)),
        out_shape=jax.ShapeDtypeStruct((n, d), F32),
        scratch_shapes=[pltpu.VMEM((d, tm), F32), pltpu.VMEM((te, tm), BF16)],
        compiler_params=_cparams(("arbitrary", "arbitrary"), 56),
        name="peer_dense",
    )(h2, u_bf, vt_bf, tb, tf)


def _slots(w, n, width, off=0):
    k = w.shape[0]
    w = w.reshape(k, n, width)
    w = jnp.pad(w, ((0, 0), (0, 0), (off, LANES - width - off)))
    return w.reshape(k, n * LANES)


def _overlap_t(seq):
    n_cmp_rows = seq // CMP_STRIDE
    n_sel = seq // SEL_BLOCK
    cs = np.arange(n_cmp_rows) * CMP_STRIDE
    ce = cs + CMP_BLOCK - 1
    ss = np.arange(n_sel) * SEL_BLOCK
    ov = (cs[None, :] < ss[:, None] + SEL_BLOCK) & (ce[None, :] >= ss[:, None])
    ov[:, n_cmp_rows - 1] = False
    return jnp.asarray(ov.astype(np.float32), BF16)


def _token_mixer(h, positions, bsz, seq, w_in, cmp_pe, cmp_w1, cmp_b1, cmp_w2, mla_q_norm_g,
                 mla_w_q_up, mla_kv_norm_g, mla_w_kv_up, w_branch_nsa, w_branch_mla, w_out):
    d = h.shape[1]
    g_, hd = NSA_GROUPS, NSA_HEAD_DIM
    tab = _rope_tables(positions)
    tm = min(PROJ_TM, seq)
    tab_spec = pl.BlockSpec((tm, 4 * LANES), lambda i, j: (i, 0))

    wb = w_in.astype(BF16)
    o1 = NSA_HEADS * hd
    kv = wb[:, o1:o1 + 3 * 2 * g_ * hd].reshape(d, 3, 2, g_ * hd)
    o2 = o1 + 3 * 2 * g_ * hd
    o3 = o2 + 3 * NSA_HEADS
    mq = mla_w_q_up.shape[0]
    mkv = mla_w_kv_up.shape[0]
    o4 = o3 + mq
    o5 = o4 + mkv
    o6 = o5 + MLA_ROPE_DIM
    w_q = _slots(wb[:, :o1], NSA_HEADS, hd)
    w_kk = jnp.concatenate([_slots(kv[:, 0, 0], g_, hd), _slots(kv[:, 2, 0], g_, hd)], axis=1)
    w_ks = _slots(kv[:, 1, 0], g_, hd)
    w_vv = jnp.concatenate([_slots(kv[:, 1, 1], g_, hd), _slots(kv[:, 2, 1], g_, hd),
                            kv[:, 0, 1], jnp.zeros((d, g_ * hd), BF16)], axis=1)
    w_c = wb[:, o3:o5]
    w_gk = jnp.concatenate([jnp.pad(wb[:, o2:o3], ((0, 0), (0, LANES - 3 * NSA_HEADS))),
                            _slots(wb[:, o5:o6], 1, MLA_ROPE_DIM, MLA_ROPE_OFF)], axis=1)
    w_gm = wb[:, o6:]

    qn = _mm(h, w_q, _make_epi_rope_a(hd ** -0.5 * LOG2E), tm=tm, tn=512, out_dtype=BF16, name="proj_q",
             extras=[(tab, tab_spec)])
    kk = _mm(h, w_kk, _make_epi_rope_a(1.0), tm=tm, tn=512, out_dtype=BF16, name="proj_k",
             extras=[(tab, tab_spec)])
    ks = _mm(h, w_ks, _make_epi_rope_a(1.0, seq, tm, True), tm=tm, tn=512, out_dtype=BF16,
             name="proj_ksel", extras=[(tab, tab_spec)])
    vv = _mm(h, w_vv, _epi_plain, tm=tm, tn=512, out_dtype=BF16, name="proj_v")
    cqkv = _mm(h, w_c, _epi_plain, tm=tm, tn=mq + mkv, out_dtype=F32, name="proj_c")
    gk = _mm(h, w_gk, _epi_gate_kr, tm=tm, tn=2 * LANES, out_dtype=BF16, name="proj_gate_kr",
             extras=[(tab, tab_spec)])
    gm = _mm(h, w_gm, _epi_sigmoid, tm=tm, tn=512, out_dtype=BF16, name="proj_gmerge")

    def to16(a, width):
        a = a.reshape(bsz, seq, g_, width)[..., :hd]
        a = jnp.transpose(a, (0, 2, 1, 3))
        return a.reshape(bsz * g_, seq // CMP_STRIDE, CMP_STRIDE * hd)

    k16 = to16(kk[:, :g_ * LANES], LANES)
    v16 = to16(vv[:, 2 * g_ * LANES:2 * g_ * LANES + g_ * hd], hd)
    pe = jnp.pad(cmp_pe.reshape(2, 1, CMP_BLOCK * hd), ((0, 0), (0, SUBLANES - 1), (0, 0))).astype(BF16)
    w2p = jnp.pad(cmp_w2, ((0, 0), (0, 0), (0, LANES - hd))).astype(BF16)
    kc, vc = _compress(k16, v16, cmp_w1.astype(BF16), pe, cmp_b1.reshape(2, 1, CMP_HIDDEN), w2p)
    o_cmp, bias = _cmp_attn(qn, kc, vc, _overlap_t(seq), bsz, seq, tq=min(512, seq))
    o_sel = _flash(qn, ks, vv, mode="sel", bsz=bsz, seq=seq, nslots=g_ // 2, nr=NSA_R, nh=2,
                   tq=512, tk=512, bias=bias, name="nsa_sel_attn")
    o_win = _flash(qn, kk, vv, mode="win", bsz=bsz, seq=seq, nslots=g_ // 2, nr=NSA_R, nh=2,
                   tq=256, tk=256, k_off=g_, v_off=g_, name="nsa_win_attn")
    o_nsa = _combine(o_cmp, o_sel, o_win, gk, 0)

    qd = MLA_NOPE_DIM + MLA_ROPE_DIM
    w_qup = _slots(mla_w_q_up.astype(BF16), MLA_HEADS, qd)
    kvu = mla_w_kv_up.astype(BF16).reshape(mkv, MLA_HEADS, MLA_NOPE_DIM + MLA_V_DIM)
    w_kup = _slots(kvu[:, :, :MLA_NOPE_DIM].reshape(mkv, -1), MLA_HEADS, MLA_NOPE_DIM)
    w_vup = _slots(kvu[:, :, MLA_NOPE_DIM:].reshape(mkv, -1), MLA_HEADS, MLA_V_DIM)
    gq_spec = pl.BlockSpec((1, mq), lambda i, j: (0, 0))
    gkv_spec = pl.BlockSpec((1, mkv), lambda i, j: (0, 0))
    ckv_spec = pl.BlockSpec((tm, mkv), lambda i, j: (i, mq // mkv))
    q_mla = _mm(cqkv, w_qup, _epi_rope_b_scaled, tm=tm, tn=512,
                out_dtype=BF16, name="mla_q_up", pro=_pro_rms,
                x_spec=pl.BlockSpec((tm, mq), lambda i, j: (i, 0)),
                extras=[(mla_q_norm_g.reshape(1, mq), gq_spec), (tab, tab_spec)])
    k_mla = _mm(cqkv, w_kup, _epi_add_kr, tm=tm, tn=512, out_dtype=BF16, name="mla_k_up",
                pro=_pro_rms, x_spec=ckv_spec,
                extras=[(mla_kv_norm_g.reshape(1, mkv), gkv_spec),
                        (gk, pl.BlockSpec((tm, LANES), lambda i, j: (i, 1)))])
    v_mla = _mm(cqkv, w_vup, _epi_plain_g, tm=tm, tn=512, out_dtype=BF16, name="mla_v_up",
                pro=_pro_rms, x_spec=ckv_spec,
                extras=[(mla_kv_norm_g.reshape(1, mkv), gkv_spec)])
    o_mla = _flash(q_mla, k_mla, v_mla, mode="causal", bsz=bsz, seq=seq, nslots=MLA_HEADS // 8, nr=1,
                   nh=8, tq=512, tk=512, name="mla_attn")

    merged = _merge(o_nsa, w_branch_nsa.astype(BF16), o_mla, w_branch_mla.astype(BF16), gm)
    return _mm(merged, w_out.astype(BF16), _epi_plain, tm=tm, tn=512, out_dtype=F32, name="out_proj")


def _peer(h2, peer_w_q, peer_sub_keys, peer_u, peer_v):
    qry = _mm(h2, peer_w_q.astype(BF16), _epi_plain, tm=min(PROJ_TM, h2.shape[0]), tn=512,
              out_dtype=BF16, name="peer_q")
    sk = peer_sub_keys.astype(BF16).reshape(2 * PEER_HEADS, PEER_N_KEYS, -1)
    tb, tf = _peer_route(qry, sk)
    return _peer_dense(h2, peer_u.astype(BF16), jnp.transpose(peer_v).astype(BF16), tb, tf)


def _block(x, c, positions, ada_w, ada_b, attn_pre_g, attn_post_g, w_in, cmp_pe, cmp_w1, cmp_b1,
           cmp_w2, mla_q_norm_g, mla_w_q_up, mla_kv_norm_g, mla_w_kv_up, w_branch_nsa, w_branch_mla,
           w_out, ffn_pre_g, ffn_post_g, peer_w_q, peer_sub_keys, peer_u, peer_v):
    bsz, seq, d = x.shape
    x2 = x.reshape(bsz * seq, d)
    mod = _adaln(c, ada_w, ada_b)
    sh_a, sc_a, g_a, sh_f, sc_f, g_f = [mod[:, i * d:(i + 1) * d] for i in range(6)]

    h = _normmod(x2, attn_pre_g, sc_a, sh_a, seq)
    y_attn = _token_mixer(h, positions, bsz, seq, w_in, cmp_pe, cmp_w1, cmp_b1, cmp_w2, mla_q_norm_g,
                          mla_w_q_up, mla_kv_norm_g, mla_w_kv_up, w_branch_nsa, w_branch_mla, w_out)
    x1 = _norm_res(x2, y_attn, attn_post_g, g_a, seq)

    h2 = _normmod(x1, ffn_pre_g, sc_f, sh_f, seq)
    y_ffn = _peer(h2, peer_w_q, peer_sub_keys, peer_u, peer_v)
    out = _norm_res(x1, y_ffn, ffn_post_g, g_f, seq)
    return out.reshape(bsz, seq, d)


def kernel(x, c, positions, ada_w, ada_b, attn_pre_g, attn_post_g, w_in, cmp_pe, cmp_w1, cmp_b1,
           cmp_w2, mla_q_norm_g, mla_w_q_up, mla_kv_norm_g, mla_w_kv_up, w_branch_nsa, w_branch_mla,
           w_out, ffn_pre_g, ffn_post_g, peer_w_q, peer_sub_keys, peer_u, peer_v):
    depth = ada_w.shape[0]
    for l in range(depth):
        x = _block(x, c, positions, ada_w[l], ada_b[l], attn_pre_g[l], attn_post_g[l], w_in[l],
                   cmp_pe[l], cmp_w1[l], cmp_b1[l], cmp_w2[l], mla_q_norm_g[l], mla_w_q_up[l],
                   mla_kv_norm_g[l], mla_w_kv_up[l], w_branch_nsa[l], w_branch_mla[l], w_out[l],
                   ffn_pre_g[l], ffn_post_g[l], peer_w_q[l], peer_sub_keys[l], peer_u[l], peer_v[l])
    return x
```

```python
import functools

import numpy as np
import jax
import jax.numpy as jnp
from jax import lax
from jax.experimental import pallas as pl
from jax.experimental.pallas import tpu as pltpu

F32 = jnp.float32
BF16 = jnp.bfloat16
I32 = jnp.int32

LANES = 128
SUBLANES = 8

NSA_HEADS = 16
NSA_GROUPS = 4
NSA_R = NSA_HEADS // NSA_GROUPS
NSA_HEAD_DIM = 64
NSA_ROPE_DIM = NSA_HEAD_DIM // 4
CMP_BLOCK = 32
CMP_STRIDE = 16
CMP_HIDDEN = 256
SEL_BLOCK = 64
SEL_SHIFT = 6
SEL_TOPK = 16
WINDOW = 512
MLA_HEADS = 16
MLA_NOPE_DIM = 64
MLA_ROPE_DIM = 32
MLA_V_DIM = 64
PEER_HEADS = 8
PEER_N_KEYS = 128
PEER_TOPK = 16
ROPE_THETA = 500000.0
NORM_EPS = 1e-6
NEG = -1e30
FORCE_BONUS = 1e4
LOG2E = 1.4426950408889634
MLA_ROPE_OFF = MLA_NOPE_DIM
PROJ_TM = 1024


def _cparams(sem, vmem_mb=None):
    kw = dict(dimension_semantics=sem)
    if vmem_mb is not None:
        kw["vmem_limit_bytes"] = vmem_mb * 1024 * 1024
    return pltpu.CompilerParams(**kw)


def _rms(x, g):
    return x * lax.rsqrt(jnp.mean(x * x, axis=-1, keepdims=True) + NORM_EPS) * g


def _gelu(x):
    return jax.nn.gelu(x, approximate=True)


def _adaln_kernel(c_ref, w_ref, b_ref, o_ref):
    c = c_ref[...]
    s = (c * jax.nn.sigmoid(c)).astype(BF16)
    o_ref[...] = jnp.dot(s, w_ref[...].astype(BF16), preferred_element_type=F32) + b_ref[...]


def _adaln(c, w, b, tn=1024):
    bsz, d = c.shape
    n = w.shape[1]
    cp = jnp.pad(c, ((0, SUBLANES - bsz), (0, 0)))
    out = pl.pallas_call(
        _adaln_kernel,
        grid=(n // tn,),
        in_specs=[pl.BlockSpec((SUBLANES, d), lambda j: (0, 0)),
                  pl.BlockSpec((d, tn), lambda j: (0, j)),
                  pl.BlockSpec((1, tn), lambda j: (0, j))],
        out_specs=pl.BlockSpec((SUBLANES, tn), lambda j: (0, j)),
        out_shape=jax.ShapeDtypeStruct((SUBLANES, n), F32),
        compiler_params=_cparams(("arbitrary",), 40),
        name="adaln",
    )(cp, w, b.reshape(1, n))
    return out[:bsz]


def _rope_table_kernel(pos_ref, inv_ref, sgn_ref, o_ref):
    pos = pos_ref[...]
    for k in range(2):
        ang = pos * inv_ref[k:k + 1, :]
        o_ref[:, (2 * k) * LANES:(2 * k + 1) * LANES] = jnp.cos(ang)
        o_ref[:, (2 * k + 1) * LANES:(2 * k + 2) * LANES] = jnp.sin(ang) * sgn_ref[k:k + 1, :]


def _rope_tables(positions, tm=512):
    n = positions.size
    half_a = NSA_ROPE_DIM // 2
    half_b = MLA_ROPE_DIM // 2
    inv_a = ROPE_THETA ** (-jnp.arange(half_a, dtype=F32) * (2.0 / NSA_ROPE_DIM))
    inv_b = ROPE_THETA ** (-jnp.arange(half_b, dtype=F32) * (2.0 / MLA_ROPE_DIM))
    inv = jnp.zeros((2, LANES), F32)
    inv = inv.at[0, 0:half_a].set(inv_a).at[0, half_a:2 * half_a].set(inv_a)
    o = MLA_ROPE_OFF
    inv = inv.at[1, o:o + half_b].set(inv_b).at[1, o + half_b:o + 2 * half_b].set(inv_b)
    sgn = np.zeros((2, LANES), np.float32)
    sgn[0, 0:half_a] = -1.0
    sgn[0, half_a:2 * half_a] = 1.0
    sgn[1, o:o + half_b] = -1.0
    sgn[1, o + half_b:o + 2 * half_b] = 1.0
    posf = jnp.broadcast_to(positions.reshape(n, 1).astype(F32), (n, LANES))
    return pl.pallas_call(
        _rope_table_kernel,
        grid=(n // tm,),
        in_specs=[pl.BlockSpec((tm, LANES), lambda i: (i, 0)),
                  pl.BlockSpec((2, LANES), lambda i: (0, 0)),
                  pl.BlockSpec((2, LANES), lambda i: (0, 0))],
        out_specs=pl.BlockSpec((tm, 4 * LANES), lambda i: (i, 0)),
        out_shape=jax.ShapeDtypeStruct((n, 4 * LANES), F32),
        compiler_params=_cparams(("arbitrary",)),
        name="rope_tables",
    )(posf, inv, jnp.asarray(sgn))


def _rope_slot(a, cos, sin, half, off):
    lane = lax.broadcasted_iota(I32, (1, LANES), 1)
    first = (lane - off) < half
    partner = jnp.where(first, pltpu.roll(a, LANES - half, 1), pltpu.roll(a, half, 1))
    return a * cos + partner * sin


def _normmod_kernel(x_ref, g_ref, sc_ref, sh_ref, o_ref):
    y = _rms(x_ref[...], g_ref[...])
    o_ref[...] = (y * (1.0 + sc_ref[0]) + sh_ref[0]).astype(o_ref.dtype)


def _normmod(x, g, sc, sh, seq, tm=512):
    n, d = x.shape
    bsz = sc.shape[0]
    bspec = pl.BlockSpec((1, 1, d), lambda i: ((i * tm) // seq, 0, 0))
    return pl.pallas_call(
        _normmod_kernel,
        grid=(n // tm,),
        in_specs=[pl.BlockSpec((tm, d), lambda i: (i, 0)),
                  pl.BlockSpec((1, d), lambda i: (0, 0)), bspec, bspec],
        out_specs=pl.BlockSpec((tm, d), lambda i: (i, 0)),
        out_shape=jax.ShapeDtypeStruct((n, d), BF16),
        compiler_params=_cparams(("arbitrary",)),
        name="normmod",
    )(x, g.reshape(1, d), sc.reshape(bsz, 1, d), sh.reshape(bsz, 1, d))


def _norm_res_kernel(x_ref, y_ref, g_ref, ga_ref, o_ref):
    o_ref[...] = x_ref[...] + ga_ref[0] * _rms(y_ref[...], g_ref[...])


def _norm_res(x, y, g, gate, seq, tm=512):
    n, d = x.shape
    bsz = gate.shape[0]
    row = pl.BlockSpec((tm, d), lambda i: (i, 0))
    return pl.pallas_call(
        _norm_res_kernel,
        grid=(n // tm,),
        in_specs=[row, row, pl.BlockSpec((1, d), lambda i: (0, 0)),
                  pl.BlockSpec((1, 1, d), lambda i: ((i * tm) // seq, 0, 0))],
        out_specs=row,
        out_shape=jax.ShapeDtypeStruct((n, d), F32),
        compiler_params=_cparams(("arbitrary",)),
        name="norm_res",
    )(x, y, g.reshape(1, d), gate.reshape(bsz, 1, d))


def _out_proj_kernel(m_ref, w_ref, x_ref, pg_ref, ga_ref, fg_ref, sc_ref, sh_ref, x1_ref, h2_ref):
    y = jnp.dot(m_ref[...], w_ref[...], preferred_element_type=F32)
    x1 = x_ref[...] + ga_ref[0] * _rms(y, pg_ref[...])
    x1_ref[...] = x1
    h2_ref[...] = (_rms(x1, fg_ref[...]) * (1.0 + sc_ref[0]) + sh_ref[0]).astype(h2_ref.dtype)


def _out_proj(merged, w, x, post_g, gate, pre_g, sc, sh, seq, tm=512):
    n, d = x.shape
    bsz = gate.shape[0]
    row = lambda width: pl.BlockSpec((tm, width), lambda i: (i, 0))
    vec = pl.BlockSpec((1, d), lambda i: (0, 0))
    bvec = pl.BlockSpec((1, 1, d), lambda i: ((i * tm) // seq, 0, 0))
    return pl.pallas_call(
        _out_proj_kernel,
        grid=(n // tm,),
        in_specs=[row(merged.shape[1]), pl.BlockSpec(w.shape, lambda i: (0, 0)), row(d), vec, bvec,
                  vec, bvec, bvec],
        out_specs=[row(d), row(d)],
        out_shape=[jax.ShapeDtypeStruct((n, d), F32), jax.ShapeDtypeStruct((n, d), BF16)],
        compiler_params=_cparams(("arbitrary",), 56),
        name="out_proj",
    )(merged, w, x, post_g.reshape(1, d), gate.reshape(bsz, 1, d), pre_g.reshape(1, d),
      sc.reshape(bsz, 1, d), sh.reshape(bsz, 1, d))


def _mm(x, w, epi, *, tm, tn, out_dtype, name, extras=(), pro=None, x_spec=None, vmem_mb=None):
    m = x.shape[0]
    k, nc = w.shape
    if x_spec is None:
        x_spec = pl.BlockSpec((tm, k), lambda i, j: (i, 0))
    in_specs = [x_spec, pl.BlockSpec((k, tn), lambda i, j: (0, j))]
    args = [x, w]
    for arr, spec in extras:
        in_specs.append(spec)
        args.append(arr)

    def kern(x_ref, w_ref, *rest):
        o_ref = rest[-1]
        ex = rest[:-1]
        xv = x_ref[...]
        if pro is not None:
            xv = pro(xv, *ex)
        acc = jnp.dot(xv, w_ref[...], preferred_element_type=F32)
        epi(acc, o_ref, *ex)

    return pl.pallas_call(
        kern,
        grid=(m // tm, nc // tn),
        in_specs=in_specs,
        out_specs=pl.BlockSpec((tm, tn), lambda i, j: (i, j)),
        out_shape=jax.ShapeDtypeStruct((m, nc), out_dtype),
        compiler_params=_cparams(("arbitrary", "arbitrary"), vmem_mb),
        name=name,
    )(*args)


def _epi_plain(acc, o_ref, *ex):
    o_ref[...] = acc.astype(o_ref.dtype)


def _epi_sigmoid(acc, o_ref, *ex):
    o_ref[...] = jax.nn.sigmoid(acc).astype(o_ref.dtype)


def _make_epi_rope_a(scale, seq=None, tm=None, onehot=False):
    half = NSA_ROPE_DIM // 2

    def epi(acc, o_ref, tab_ref):
        cos = tab_ref[:, 0:LANES]
        sin = tab_ref[:, LANES:2 * LANES]
        if onehot:
            base = lax.rem(pl.program_id(0) * tm, seq)
            t = base + lax.broadcasted_iota(I32, (acc.shape[0], LANES), 0)
            lane = lax.broadcasted_iota(I32, (acc.shape[0], LANES), 1)
            hot = jnp.where(lane - SEL_BLOCK == jnp.right_shift(t, SEL_SHIFT), 1.0, 0.0)
        for s in range(acc.shape[1] // LANES):
            r = _rope_slot(acc[:, s * LANES:(s + 1) * LANES], cos, sin, half, 0)
            if scale != 1.0:
                r = r * scale
            if onehot:
                r = r + hot
            o_ref[:, s * LANES:(s + 1) * LANES] = r.astype(o_ref.dtype)
    return epi


def _epi_rope_b_scaled(acc, o_ref, g_ref, tab_ref):
    cos = tab_ref[:, 2 * LANES:3 * LANES]
    sin = tab_ref[:, 3 * LANES:4 * LANES]
    scale = (MLA_NOPE_DIM + MLA_ROPE_DIM) ** -0.5 * LOG2E
    for s in range(acc.shape[1] // LANES):
        r = _rope_slot(acc[:, s * LANES:(s + 1) * LANES], cos, sin, MLA_ROPE_DIM // 2, MLA_ROPE_OFF)
        o_ref[:, s * LANES:(s + 1) * LANES] = (r * scale).astype(o_ref.dtype)


def _epi_gate_kr(acc, o_ref, tab_ref):
    o_ref[:, 0:LANES] = jax.nn.sigmoid(acc[:, 0:LANES]).astype(o_ref.dtype)
    cos = tab_ref[:, 2 * LANES:3 * LANES]
    sin = tab_ref[:, 3 * LANES:4 * LANES]
    r = _rope_slot(acc[:, LANES:2 * LANES], cos, sin, MLA_ROPE_DIM // 2, MLA_ROPE_OFF)
    o_ref[:, LANES:2 * LANES] = r.astype(o_ref.dtype)


def _pro_rms(xv, g_ref, *ex):
    return _rms(xv, g_ref[...]).astype(BF16)


def _epi_add_kr(acc, o_ref, g_ref, kr_ref):
    kr = kr_ref[...].astype(F32)
    for s in range(acc.shape[1] // LANES):
        o_ref[:, s * LANES:(s + 1) * LANES] = (acc[:, s * LANES:(s + 1) * LANES] + kr).astype(o_ref.dtype)


def _epi_plain_g(acc, o_ref, g_ref):
    o_ref[...] = acc.astype(o_ref.dtype)


def _compress_kernel(k16_ref, v16_ref, w1_ref, pe_ref, b1_ref, w2_ref, kc_ref, vc_ref):
    half = w1_ref.shape[1] // 2
    for which, (src, dst) in enumerate(((k16_ref, kc_ref), (v16_ref, vc_ref))):
        xb = src[...]
        a = jnp.dot(xb, w1_ref[which, 0:half, :], preferred_element_type=F32)
        b = jnp.dot(xb, w1_ref[which, half:2 * half, :], preferred_element_type=F32)
        c = jnp.dot(pe_ref[which], w1_ref[which], preferred_element_type=F32)[0:1, :] + b1_ref[which]
        rows = a.shape[0]
        pre = a + pltpu.roll(b, rows - 1, 0) + c
        hid = _gelu(pre).astype(BF16)
        dst[...] = jnp.dot(hid, w2_ref[which], preferred_element_type=F32).astype(dst.dtype)


def _compress(k16, v16, w1, pe, b1, w2p):
    bg, rows, feat = k16.shape
    blk = pl.BlockSpec((None, rows, feat), lambda i: (i, 0, 0))
    full = lambda a: pl.BlockSpec(a.shape, lambda i: (0,) * a.ndim)
    out = pl.BlockSpec((None, rows, LANES), lambda i: (i, 0, 0))
    return pl.pallas_call(
        _compress_kernel,
        grid=(bg,),
        in_specs=[blk, blk, full(w1), full(pe), full(b1), full(w2p)],
        out_specs=[out, out],
        out_shape=[jax.ShapeDtypeStruct((bg, rows, LANES), BF16)] * 2,
        compiler_params=_cparams(("arbitrary",)),
        name="nsa_compress",
    )(k16, v16, w1, pe, b1, w2p)


def _row_bcast(col, width):
    if width % LANES:
        return jnp.broadcast_to(col, (col.shape[0], width))
    tile = jnp.broadcast_to(col, (col.shape[0], LANES))
    return jnp.concatenate([tile] * (width // LANES), axis=1)


def _cmp_attn_kernel(q_ref, kc_ref, vc_ref, ovt_ref, o_ref, bias_ref, *, tq, n_sel):
    qi = pl.program_id(2)
    ncmp = kc_ref.shape[0]
    t_row = qi * tq + lax.broadcasted_iota(I32, (tq, ncmp), 0)
    n_col = lax.broadcasted_iota(I32, (tq, ncmp), 1)
    cmask = (n_col * CMP_STRIDE + (CMP_BLOCK - 1)) <= t_row
    kc = kc_ref[...]
    vc = vc_ref[...]
    imp_t = jnp.zeros((n_sel, tq), F32)
    outs = []
    for r in range(NSA_R):
        q = q_ref[:, r * LANES:(r + 1) * LANES]
        s = lax.dot_general(q, kc, (((1,), (1,)), ((), ())), preferred_element_type=F32)
        s = jnp.where(cmask, s, NEG)
        e = jnp.exp2(s - _row_bcast(jnp.max(s, axis=1, keepdims=True), ncmp))
        p = e / _row_bcast(jnp.sum(e, axis=1, keepdims=True), ncmp)
        p = jnp.where(cmask, p, 0.0).astype(BF16)
        outs.append(jnp.dot(p, vc, preferred_element_type=F32))
        imp_t = imp_t + lax.dot_general(ovt_ref[...], p, (((1,), (1,)), ((), ())),
                                        preferred_element_type=F32)
    _store_heads_compact(o_ref, outs)
    blk = lax.broadcasted_iota(I32, (n_sel, tq), 0)
    t = qi * tq + lax.broadcasted_iota(I32, (n_sel, tq), 1)
    tb = jnp.right_shift(t, SEL_SHIFT)
    forced = (blk == 0) | (blk == tb) | (blk == tb - 1)
    valid = blk * SEL_BLOCK <= t
    x = jnp.where(valid, imp_t + jnp.where(forced, FORCE_BONUS, 0.0), NEG)
    sel = jnp.zeros((n_sel, tq), F32)
    for _ in range(min(SEL_TOPK, n_sel)):
        m = jnp.max(x, axis=0, keepdims=True)
        idx = jnp.min(jnp.where(x == m, blk, n_sel), axis=0, keepdims=True)
        hit = blk == idx
        sel = jnp.where(hit, 1.0, sel)
        x = jnp.where(hit, -jnp.inf, x)
    bias_t = jnp.where(sel > 0.5, 0.0, NEG)
    parts = [jnp.zeros((SEL_BLOCK, tq), F32), bias_t]
    if LANES - SEL_BLOCK - n_sel > 0:
        parts.append(jnp.zeros((LANES - SEL_BLOCK - n_sel, tq), F32))
    full_t = jnp.concatenate(parts, axis=0)
    bias_ref[...] = full_t.T.astype(bias_ref.dtype)


def _cmp_attn(qn, kc, vc, ovt, bsz, seq, tq=256):
    n = qn.shape[0]
    nq = seq // tq
    n_sel = seq // SEL_BLOCK
    ncmp = kc.shape[1]
    qspec = pl.BlockSpec((tq, NSA_R * LANES), lambda b, g, qi: (b * nq + qi, g))
    kspec = pl.BlockSpec((None, ncmp, LANES), lambda b, g, qi: (b * NSA_GROUPS + g, 0, 0))
    return pl.pallas_call(
        functools.partial(_cmp_attn_kernel, tq=tq, n_sel=n_sel),
        grid=(bsz, NSA_GROUPS, nq),
        in_specs=[qspec, kspec, kspec, pl.BlockSpec(ovt.shape, lambda b, g, qi: (0, 0))],
        out_specs=[pl.BlockSpec((tq, NSA_R * NSA_HEAD_DIM), lambda b, g, qi: (b * nq + qi, g)),
                   pl.BlockSpec((tq, LANES), lambda b, g, qi: (b * nq + qi, g))],
        out_shape=[jax.ShapeDtypeStruct((n, NSA_HEADS * NSA_HEAD_DIM), BF16),
                   jax.ShapeDtypeStruct((n, NSA_GROUPS * LANES), BF16)],
        compiler_params=_cparams(("arbitrary",) * 3),
        name="nsa_cmp_attn",
    )(qn, kc, vc, ovt)


def _store_heads_compact(o_ref, heads):
    hd = NSA_HEAD_DIM
    for p in range(len(heads) // 2):
        pair = jnp.concatenate([heads[2 * p][:, 0:hd], heads[2 * p + 1][:, 0:hd]], axis=1)
        o_ref[:, p * LANES:(p + 1) * LANES] = pair.astype(o_ref.dtype)


def _flash_kernel(*refs, mode, tq, tk, nr, nh, window):
    if mode == "sel":
        q_ref, k_ref, v_ref, bias_ref, o_ref, qs, m_s, l_s, acc_s = refs
    else:
        q_ref, k_ref, v_ref, o_ref, qs, m_s, l_s, acc_s = refs
    qi = pl.program_id(2)
    kk = pl.program_id(3)
    nk = pl.num_programs(3)
    rows = nr * tq

    @pl.when(kk == 0)
    def _init():
        for r in range(nh * nr):
            qr = q_ref[:, r * LANES:(r + 1) * LANES]
            if mode == "sel":
                qr = qr + bias_ref[:, (r // nr) * LANES:(r // nr + 1) * LANES]
            qs[r * tq:(r + 1) * tq, :] = qr
        m_s[...] = jnp.full(m_s.shape, -jnp.inf, F32)
        l_s[...] = jnp.zeros(l_s.shape, F32)
        acc_s[...] = jnp.zeros(acc_s.shape, F32)

    if mode == "win":
        kidx = qi * (tq // tk) - window // tk + kk
        needed = kidx >= 0
        full_vis = (kidx * tk + tk - 1 <= qi * tq) & (qi * tq + tq - 1 - kidx * tk < window)
    else:
        kidx = kk
        needed = kk * tk <= qi * tq + tq - 1
        full_vis = kk * tk + tk - 1 <= qi * tq

    nt = tk // LANES

    def update(masked):
        if masked:
            rel = (lax.broadcasted_iota(I32, (rows, tk), 0) & (tq - 1)) - lax.broadcasted_iota(I32, (rows, tk), 1)
            off = kidx * tk - qi * tq
            vis = rel >= off
            if mode == "win":
                vis = vis & (rel < off + window)
        for hh in range(nh):
            rs = slice(hh * rows, (hh + 1) * rows)
            cs = slice(hh * LANES, (hh + 1) * LANES)
            s = lax.dot_general(qs[rs, :], k_ref[:, cs], (((1,), (1,)), ((), ())),
                                preferred_element_type=F32)
            if masked:
                s = jnp.where(vis, s, NEG)
            m_prev = m_s[rs, :]
            m_new = jnp.maximum(m_prev, jnp.max(s, axis=1, keepdims=True))
            p = jnp.exp2(s - jnp.concatenate([m_new] * nt, axis=1))
            alpha = jnp.exp2(m_prev - m_new)
            psum = p[:, 0:LANES]
            for c in range(1, nt):
                psum = psum + p[:, c * LANES:(c + 1) * LANES]
            l_s[rs, :] = alpha * l_s[rs, :] + psum
            acc_s[rs, :] = alpha * acc_s[rs, :] + jnp.dot(p.astype(BF16), v_ref[:, cs],
                                                          preferred_element_type=F32)
            m_s[rs, :] = m_new

    @pl.when(needed & full_vis)
    def _full():
        update(False)

    @pl.when(needed & jnp.logical_not(full_vis))
    def _edge():
        update(True)

    @pl.when(kk == nk - 1)
    def _fin():
        out = acc_s[...] / jnp.sum(l_s[...], axis=1, keepdims=True)
        _store_heads_compact(o_ref, [out[r * tq:(r + 1) * tq, :] for r in range(nh * nr)])


def _flash(q, k, v, *, mode, bsz, seq, nslots, nr, tq, tk, nh=1, k_off=0, v_off=0, bias=None,
           name):
    assert tq & (tq - 1) == 0
    n = q.shape[0]
    nq = seq // tq
    nkb = seq // tk
    if mode == "win":
        assert tq % tk == 0 and WINDOW % tk == 0
        steps = WINDOW // tk + tq // tk

        def krow(b, qi, kk):
            return b * nkb + jnp.maximum(qi * (tq // tk) - WINDOW // tk + kk, 0)
    else:
        steps = nkb

        def krow(b, qi, kk):
            return b * nkb + jnp.minimum(kk, (qi * tq + tq - 1) // tk)

    qspec = pl.BlockSpec((tq, nh * nr * LANES), lambda b, g, qi, kk: (b * nq + qi, g))
    in_specs = [qspec,
                pl.BlockSpec((tk, nh * LANES), lambda b, g, qi, kk: (krow(b, qi, kk), k_off // nh + g)),
                pl.BlockSpec((tk, nh * LANES), lambda b, g, qi, kk: (krow(b, qi, kk), v_off // nh + g))]
    args = [q, k, v]
    if mode == "sel":
        in_specs.append(pl.BlockSpec((tq, nh * LANES), lambda b, g, qi, kk: (b * nq + qi, g)))
        args.append(bias)
    rows = nh * nr * tq
    ow = nh * nr * NSA_HEAD_DIM
    return pl.pallas_call(
        functools.partial(_flash_kernel, mode=mode, tq=tq, tk=tk, nr=nr, nh=nh, window=WINDOW),
        grid=(bsz, nslots, nq, steps),
        in_specs=in_specs,
        out_specs=pl.BlockSpec((tq, ow), lambda b, g, qi, kk: (b * nq + qi, g)),
        out_shape=jax.ShapeDtypeStruct((n, nslots * ow), BF16),
        scratch_shapes=[pltpu.VMEM((rows, LANES), BF16), pltpu.VMEM((rows, LANES), F32),
                        pltpu.VMEM((rows, LANES), F32), pltpu.VMEM((rows, LANES), F32)],
        compiler_params=_cparams(("arbitrary",) * 4, 48),
        name=name,
    )(*args)


def _combine_kernel(oc_ref, os_ref, ow_ref, g_ref, e_ref, o_ref):
    g = g_ref[...]
    acc = None
    for br, src in enumerate((oc_ref, os_ref, ow_ref)):
        ge = jnp.dot(g, e_ref[br], preferred_element_type=F32)
        term = ge * src[...].astype(F32)
        acc = term if acc is None else acc + term
    o_ref[...] = acc.astype(o_ref.dtype)


def _combine(o_cmp, o_sel, o_win, gates, gates_col, tm=512):
    n, w = o_cmp.shape
    e = np.zeros((3, LANES, w), np.float32)
    for h in range(NSA_HEADS):
        for br in range(3):
            e[br, h * 3 + br, h * NSA_HEAD_DIM:(h + 1) * NSA_HEAD_DIM] = 1.0
    row = pl.BlockSpec((tm, w), lambda i: (i, 0))
    return pl.pallas_call(
        _combine_kernel,
        grid=(n // tm,),
        in_specs=[row, row, row, pl.BlockSpec((tm, LANES), lambda i: (i, gates_col)),
                  pl.BlockSpec(e.shape, lambda i: (0, 0, 0))],
        out_specs=row,
        out_shape=jax.ShapeDtypeStruct((n, w), BF16),
        compiler_params=_cparams(("arbitrary",)),
        name="nsa_combine",
    )(o_cmp, o_sel, o_win, gates, jnp.asarray(e, BF16))


def _merge_kernel(a_ref, wa_ref, b_ref, wb_ref, g0_ref, g1_ref, o_ref):
    ya = jnp.dot(a_ref[...], wa_ref[...], preferred_element_type=F32)
    yb = jnp.dot(b_ref[...], wb_ref[...], preferred_element_type=F32)
    o_ref[...] = (g0_ref[...].astype(F32) * ya + g1_ref[...].astype(F32) * yb).astype(o_ref.dtype)


def _merge(o_nsa, w_nsa, o_mla, w_mla, gm, tm=512, tn=512):
    n, k = o_nsa.shape
    d = w_nsa.shape[1]
    nj = d // tn
    row = pl.BlockSpec((tm, k), lambda i, j: (i, 0))
    wsp = pl.BlockSpec((k, tn), lambda i, j: (0, j))
    return pl.pallas_call(
        _merge_kernel,
        grid=(n // tm, nj),
        in_specs=[row, wsp, row, wsp,
                  pl.BlockSpec((tm, tn), lambda i, j: (i, j)),
                  pl.BlockSpec((tm, tn), lambda i, j: (i, nj + j))],
        out_specs=pl.BlockSpec((tm, tn), lambda i, j: (i, j)),
        out_shape=jax.ShapeDtypeStruct((n, d), BF16),
        compiler_params=_cparams(("arbitrary", "arbitrary")),
        name="branch_merge",
    )(o_nsa, w_nsa, o_mla, w_mla, gm, gm)


def _merge_desc(xs):
    xs = list(xs)
    n = len(xs)
    stride = n // 2
    while stride >= 1:
        for i in range(n):
            p = i ^ stride
            if p > i:
                xs[i], xs[p] = jnp.maximum(xs[i], xs[p]), jnp.minimum(xs[i], xs[p])
        stride //= 2
    return xs


def _top16_sorted(xs):
    xs = list(xs)
    n = len(xs)
    size = 2
    while size <= n:
        stride = size // 2
        while stride >= 1:
            for i in range(n):
                p = i ^ stride
                if p > i:
                    hi, lo = jnp.maximum(xs[i], xs[p]), jnp.minimum(xs[i], xs[p])
                    xs[i], xs[p] = (hi, lo) if (i & size) == 0 else (lo, hi)
            stride //= 2
        size *= 2
    shift = SUBLANES // 2
    while shift >= 1:
        ys = [pltpu.roll(x, shift, 0) for x in xs]
        xs = _merge_desc([jnp.maximum(xs[k], ys[n - 1 - k]) for k in range(n)])
        shift //= 2
    return xs


def _prefix_len(pred, vs):
    p8 = pred(vs[7])
    p4 = pred(jnp.where(p8, vs[11], vs[3]))
    p2 = pred(jnp.where(p8, jnp.where(p4, vs[13], vs[9]), jnp.where(p4, vs[5], vs[1])))
    q = [jnp.where(p2, vs[4 * i + 2], vs[4 * i]) for i in range(4)]
    p1 = pred(jnp.where(p8, jnp.where(p4, q[3], q[2]), jnp.where(p4, q[1], q[0])))
    n = (jnp.where(p8, 8.0, 0.0) + jnp.where(p4, 4.0, 0.0)) + (jnp.where(p2, 2.0, 0.0) + jnp.where(p1, 1.0, 0.0))
    return jnp.where(pred(vs[15]), 16.0, n)


def _rows8(x):
    return [x[k * SUBLANES:(k + 1) * SUBLANES, :] for k in range(x.shape[0] // SUBLANES)]


def _pack_rows(x, dtype):
    if dtype != jnp.uint32:
        return x.astype(dtype)
    bits = pltpu.bitcast(x.astype(jnp.bfloat16).astype(F32), jnp.uint32)
    return bits | jnp.right_shift(bits, jnp.uint32(16))


def _peer_route_kernel(q_ref, sk_ref, tb_ref, tf_ref):
    nk = PEER_N_KEYS
    sec = PEER_HEADS * nk
    tr = q_ref.shape[0]
    sub = lax.broadcasted_iota(I32, (SUBLANES, tr), 0)
    ninf = jnp.full((SUBLANES, tr), -jnp.inf, F32)

    def spread(vs):
        out = vs[SUBLANES - 1]
        for j in range(SUBLANES - 2, -1, -1):
            out = jnp.where(sub == j, vs[j], out)
        return out

    for h in range(PEER_HEADS):
        s = []
        for p in range(2):
            c = (2 * h + p) * nk
            s.append(lax.dot_general(sk_ref[2 * h + p], q_ref[:, c:c + nk], (((1,), (1,)), ((), ())),
                                     preferred_element_type=F32))
        x1, x2 = _rows8(s[0]), _rows8(s[1])
        v1, v2 = _top16_sorted(x1), _top16_sorted(x2)
        v2lo, v2hi = spread(v2[:SUBLANES]), spread(v2[SUBLANES:])
        cands = [v1[0] + v2lo, v1[0] + v2hi]
        for i in range(1, SUBLANES):
            c = v1[i] + v2lo
            cnt = PEER_TOPK // (i + 1)
            cands.append(jnp.where(sub < cnt, c, -jnp.inf) if cnt < SUBLANES else c)
        cands.append(spread(v1[SUBLANES:]) + v2[0])
        top = _top16_sorted(cands + [ninf] * (PEER_TOPK - len(cands)))
        tau = top[PEER_TOPK - 1]
        z = None
        for k in range(PEER_TOPK):
            ek = jnp.exp(top[k] - top[0])
            z = ek if z is None else z + ek
        rank2 = jnp.concatenate([_prefix_len(lambda v, x=x: v > x, v2) for x in x2], axis=0)
        count = jnp.concatenate([_prefix_len(lambda v, x=x: (x + v) >= tau, v2) for x in x1], axis=0)
        tb_ref[h * nk:(h + 1) * nk, :] = rank2.astype(tb_ref.dtype)
        tb_ref[sec + h * nk:sec + (h + 1) * nk, :] = jnp.exp(s[1] - v2[0][0:1, :]).astype(tb_ref.dtype)
        tf_ref[h * nk:(h + 1) * nk, :] = _pack_rows(count, tf_ref.dtype)
        tf_ref[sec + h * nk:sec + (h + 1) * nk, :] = _pack_rows(
            jnp.exp(s[0] - v1[0][0:1, :]) / z[0:1, :], tf_ref.dtype)


def _peer_route(qry, sk, tr=256):
    n, w = qry.shape
    col = pl.BlockSpec((w, tr), lambda i: (0, i))
    return pl.pallas_call(
        _peer_route_kernel,
        grid=(n // tr,),
        in_specs=[pl.BlockSpec((tr, w), lambda i: (i, 0)),
                  pl.BlockSpec(sk.shape, lambda i: (0, 0, 0))],
        out_specs=[col, col],
        out_shape=[jax.ShapeDtypeStruct((w, n), BF16),
                   jax.ShapeDtypeStruct((w, n), jnp.uint32 if BF16 == jnp.bfloat16 else F32)],
        compiler_params=_cparams(("arbitrary",), 40),
        name="peer_route",
    )(qry, sk)


PEER_CHUNK = 32
PEER_TOKEN_GROUPS = 1


def _peer_dense_kernel(h_ref, u_ref, vt_ref, tb_ref, tf_ref, o_ref, acc_s, pt_s, *, te):
    j = pl.program_id(1)
    nk = PEER_N_KEYS
    sec = PEER_HEADS * nk
    gdt = pt_s.dtype

    @pl.when(j == 0)
    def _init():
        acc_s[...] = jnp.zeros(acc_s.shape, F32)

    def row_tile(idx, cols):
        r = tf_ref[pl.ds(idx, 1), cols]
        if tf_ref.dtype == jnp.uint32:
            t = pltpu.bitcast(jnp.broadcast_to(r, (SUBLANES, r.shape[1])), gdt)
            return jnp.concatenate([t] * (PEER_CHUNK // t.shape[0]), axis=0)
        return jnp.broadcast_to(r.astype(gdt), (PEER_CHUNK, r.shape[1]))

    tm = h_ref.shape[0]
    tg = tm // PEER_TOKEN_GROUPS
    for grp in range(PEER_TOKEN_GROUPS):
        cols = slice(grp * tg, (grp + 1) * tg)
        zt = lax.dot_general(u_ref[...], h_ref[cols, :], (((1,), (1,)), ((), ())),
                             preferred_element_type=F32)
        for al in range(te // nk):
            a = j * (te // nk) + al
            cnt = [row_tile(h * nk + a, cols) for h in range(PEER_HEADS)]
            e1 = [row_tile(sec + h * nk + a, cols) for h in range(PEER_HEADS)]
            for c in range(nk // PEER_CHUNK):
                lo = c * PEER_CHUNK
                g = None
                for h in range(PEER_HEADS):
                    rank2 = tb_ref[h * nk + lo:h * nk + lo + PEER_CHUNK, cols]
                    e2 = tb_ref[sec + h * nk + lo:sec + h * nk + lo + PEER_CHUNK, cols]
                    term = jnp.where(rank2 < cnt[h], e1[h] * e2, jnp.zeros((), gdt))
                    g = term if g is None else g + term
                r0 = al * nk + lo
                pt_s[r0:r0 + PEER_CHUNK, cols] = g * _gelu(zt[r0:r0 + PEER_CHUNK, :]).astype(gdt)
        acc_s[:, cols] += jnp.dot(vt_ref[...], pt_s[:, cols], preferred_element_type=F32)

    @pl.when(j == pl.num_programs(1) - 1)
    def _fin():
        o_ref[...] = acc_s[...].T


def _peer_dense(h2, u_bf, vt_bf, tb, tf, tm=512, te=1024):
    n, d = h2.shape
    ne = u_bf.shape[0]
    w = tb.shape[0]
    return pl.pallas_call(
        functools.partial(_peer_dense_kernel, te=te),
        grid=(n // tm, ne // te),
        in_specs=[pl.BlockSpec((tm, d), lambda i, j: (i, 0)),
                  pl.BlockSpec((te, d), lambda i, j: (j, 0)),
                  pl.BlockSpec((d, te), lambda i, j: (0, j)),
                  pl.BlockSpec((w, tm), lambda i, j: (0, i)),
                  pl.BlockSpec((w, tm), lambda i, j: (0, i))],
        out_specs=pl.BlockSpec((tm, d), lambda i, j: (i, 0)),
        out_shape=jax.ShapeDtypeStruct((n, d), F32),
        scratch_shapes=[pltpu.VMEM((d, tm), F32), pltpu.VMEM((te, tm), BF16)],
        compiler_params=_cparams(("arbitrary", "arbitrary"), 56),
        name="peer_dense",
    )(h2, u_bf, vt_bf, tb, tf)


def _slots(w, n, width, off=0):
    k = w.shape[0]
    w = w.reshape(k, n, width)
    w = jnp.pad(w, ((0, 0), (0, 0), (off, LANES - width - off)))
    return w.reshape(k, n * LANES)


def _overlap_t(seq):
    n_cmp_rows = seq // CMP_STRIDE
    n_sel = seq // SEL_BLOCK
    cs = np.arange(n_cmp_rows) * CMP_STRIDE
    ce = cs + CMP_BLOCK - 1
    ss = np.arange(n_sel) * SEL_BLOCK
    ov = (cs[None, :] < ss[:, None] + SEL_BLOCK) & (ce[None, :] >= ss[:, None])
    ov[:, n_cmp_rows - 1] = False
    return jnp.asarray(ov.astype(np.float32), BF16)


def _token_mixer(h, positions, bsz, seq, w_in, cmp_pe, cmp_w1, cmp_b1, cmp_w2, mla_q_norm_g,
                 mla_w_q_up, mla_kv_norm_g, mla_w_kv_up, w_branch_nsa, w_branch_mla):
    d = h.shape[1]
    g_, hd = NSA_GROUPS, NSA_HEAD_DIM
    tab = _rope_tables(positions)
    tm = min(PROJ_TM, seq)
    tab_spec = pl.BlockSpec((tm, 4 * LANES), lambda i, j: (i, 0))

    wb = w_in.astype(BF16)
    o1 = NSA_HEADS * hd
    kv = wb[:, o1:o1 + 3 * 2 * g_ * hd].reshape(d, 3, 2, g_ * hd)
    o2 = o1 + 3 * 2 * g_ * hd
    o3 = o2 + 3 * NSA_HEADS
    mq = mla_w_q_up.shape[0]
    mkv = mla_w_kv_up.shape[0]
    o4 = o3 + mq
    o5 = o4 + mkv
    o6 = o5 + MLA_ROPE_DIM
    w_q = _slots(wb[:, :o1], NSA_HEADS, hd)
    w_kk = jnp.concatenate([_slots(kv[:, 0, 0], g_, hd), _slots(kv[:, 2, 0], g_, hd)], axis=1)
    w_ks = _slots(kv[:, 1, 0], g_, hd)
    w_vv = jnp.concatenate([_slots(kv[:, 1, 1], g_, hd), _slots(kv[:, 2, 1], g_, hd),
                            kv[:, 0, 1], jnp.zeros((d, g_ * hd), BF16)], axis=1)
    w_c = wb[:, o3:o5]
    w_gk = jnp.concatenate([jnp.pad(wb[:, o2:o3], ((0, 0), (0, LANES - 3 * NSA_HEADS))),
                            _slots(wb[:, o5:o6], 1, MLA_ROPE_DIM, MLA_ROPE_OFF)], axis=1)
    w_gm = wb[:, o6:]

    qn = _mm(h, w_q, _make_epi_rope_a(hd ** -0.5 * LOG2E), tm=tm, tn=512, out_dtype=BF16, name="proj_q",
             extras=[(tab, tab_spec)])
    kk = _mm(h, w_kk, _make_epi_rope_a(1.0), tm=tm, tn=512, out_dtype=BF16, name="proj_k",
             extras=[(tab, tab_spec)])
    ks = _mm(h, w_ks, _make_epi_rope_a(1.0, seq, tm, True), tm=tm, tn=512, out_dtype=BF16,
             name="proj_ksel", extras=[(tab, tab_spec)])
    vv = _mm(h, w_vv, _epi_plain, tm=tm, tn=512, out_dtype=BF16, name="proj_v")
    cqkv = _mm(h, w_c, _epi_plain, tm=tm, tn=mq + mkv, out_dtype=F32, name="proj_c")
    gk = _mm(h, w_gk, _epi_gate_kr, tm=tm, tn=2 * LANES, out_dtype=BF16, name="proj_gate_kr",
             extras=[(tab, tab_spec)])
    gm = _mm(h, w_gm, _epi_sigmoid, tm=tm, tn=512, out_dtype=BF16, name="proj_gmerge")

    def to16(a, width):
        a = a.reshape(bsz, seq, g_, width)[..., :hd]
        a = jnp.transpose(a, (0, 2, 1, 3))
        return a.reshape(bsz * g_, seq // CMP_STRIDE, CMP_STRIDE * hd)

    k16 = to16(kk[:, :g_ * LANES], LANES)
    v16 = to16(vv[:, 2 * g_ * LANES:2 * g_ * LANES + g_ * hd], hd)
    pe = jnp.pad(cmp_pe.reshape(2, 1, CMP_BLOCK * hd), ((0, 0), (0, SUBLANES - 1), (0, 0))).astype(BF16)
    w2p = jnp.pad(cmp_w2, ((0, 0), (0, 0), (0, LANES - hd))).astype(BF16)
    kc, vc = _compress(k16, v16, cmp_w1.astype(BF16), pe, cmp_b1.reshape(2, 1, CMP_HIDDEN), w2p)
    o_cmp, bias = _cmp_attn(qn, kc, vc, _overlap_t(seq), bsz, seq, tq=min(512, seq))
    o_sel = _flash(qn, ks, vv, mode="sel", bsz=bsz, seq=seq, nslots=g_ // 2, nr=NSA_R, nh=2,
                   tq=512, tk=512, bias=bias, name="nsa_sel_attn")
    o_win = _flash(qn, kk, vv, mode="win", bsz=bsz, seq=seq, nslots=g_ // 2, nr=NSA_R, nh=2,
                   tq=256, tk=256, k_off=g_, v_off=g_, name="nsa_win_attn")
    o_nsa = _combine(o_cmp, o_sel, o_win, gk, 0)

    qd = MLA_NOPE_DIM + MLA_ROPE_DIM
    w_qup = _slots(mla_w_q_up.astype(BF16), MLA_HEADS, qd)
    kvu = mla_w_kv_up.astype(BF16).reshape(mkv, MLA_HEADS, MLA_NOPE_DIM + MLA_V_DIM)
    w_kup = _slots(kvu[:, :, :MLA_NOPE_DIM].reshape(mkv, -1), MLA_HEADS, MLA_NOPE_DIM)
    w_vup = _slots(kvu[:, :, MLA_NOPE_DIM:].reshape(mkv, -1), MLA_HEADS, MLA_V_DIM)
    gq_spec = pl.BlockSpec((1, mq), lambda i, j: (0, 0))
    gkv_spec = pl.BlockSpec((1, mkv), lambda i, j: (0, 0))
    ckv_spec = pl.BlockSpec((tm, mkv), lambda i, j: (i, mq // mkv))
    q_mla = _mm(cqkv, w_qup, _epi_rope_b_scaled, tm=tm, tn=512,
                out_dtype=BF16, name="mla_q_up", pro=_pro_rms,
                x_spec=pl.BlockSpec((tm, mq), lambda i, j: (i, 0)),
                extras=[(mla_q_norm_g.reshape(1, mq), gq_spec), (tab, tab_spec)])
    k_mla = _mm(cqkv, w_kup, _epi_add_kr, tm=tm, tn=512, out_dtype=BF16, name="mla_k_up",
                pro=_pro_rms, x_spec=ckv_spec,
                extras=[(mla_kv_norm_g.reshape(1, mkv), gkv_spec),
                        (gk, pl.BlockSpec((tm, LANES), lambda i, j: (i, 1)))])
    v_mla = _mm(cqkv, w_vup, _epi_plain_g, tm=tm, tn=512, out_dtype=BF16, name="mla_v_up",
                pro=_pro_rms, x_spec=ckv_spec,
                extras=[(mla_kv_norm_g.reshape(1, mkv), gkv_spec)])
    o_mla = _flash(q_mla, k_mla, v_mla, mode="causal", bsz=bsz, seq=seq, nslots=MLA_HEADS // 8, nr=1,
                   nh=8, tq=512, tk=512, name="mla_attn")

    return _merge(o_nsa, w_branch_nsa.astype(BF16), o_mla, w_branch_mla.astype(BF16), gm)


def _peer(h2, peer_w_q, peer_sub_keys, peer_u, peer_v):
    qry = _mm(h2, peer_w_q.astype(BF16), _epi_plain, tm=min(PROJ_TM, h2.shape[0]), tn=512,
              out_dtype=BF16, name="peer_q")
    sk = peer_sub_keys.astype(BF16).reshape(2 * PEER_HEADS, PEER_N_KEYS, -1)
    tb, tf = _peer_route(qry, sk)
    return _peer_dense(h2, peer_u.astype(BF16), jnp.transpose(peer_v).astype(BF16), tb, tf)


def _block(x, c, positions, ada_w, ada_b, attn_pre_g, attn_post_g, w_in, cmp_pe, cmp_w1, cmp_b1,
           cmp_w2, mla_q_norm_g, mla_w_q_up, mla_kv_norm_g, mla_w_kv_up, w_branch_nsa, w_branch_mla,
           w_out, ffn_pre_g, ffn_post_g, peer_w_q, peer_sub_keys, peer_u, peer_v):
    bsz, seq, d = x.shape
    x2 = x.reshape(bsz * seq, d)
    mod = _adaln(c, ada_w, ada_b)
    sh_a, sc_a, g_a, sh_f, sc_f, g_f = [mod[:, i * d:(i + 1) * d] for i in range(6)]

    h = _normmod(x2, attn_pre_g, sc_a, sh_a, seq)
    merged = _token_mixer(h, positions, bsz, seq, w_in, cmp_pe, cmp_w1, cmp_b1, cmp_w2, mla_q_norm_g,
                          mla_w_q_up, mla_kv_norm_g, mla_w_kv_up, w_branch_nsa, w_branch_mla)
    x1, h2 = _out_proj(merged, w_out.astype(BF16), x2, attn_post_g, g_a, ffn_pre_g, sc_f, sh_f, seq)

    y_ffn = _peer(h2, peer_w_q, peer_sub_keys, peer_u, peer_v)
    out = _norm_res(x1, y_ffn, ffn_post_g, g_f, seq)
    return out.reshape(bsz, seq, d)


def kernel(x, c, positions, ada_w, ada_b, attn_pre_g, attn_post_g, w_in, cmp_pe, cmp_w1, cmp_b1,
           cmp_w2, mla_q_norm_g, mla_w_q_up, mla_kv_norm_g, mla_w_kv_up, w_branch_nsa, w_branch_mla,
           w_out, ffn_pre_g, ffn_post_g, peer_w_q, peer_sub_keys, peer_u, peer_v):
    depth = ada_w.shape[0]
    for l in range(depth):
        x = _block(x, c, positions, ada_w[l], ada_b[l], attn_pre_g[l], attn_post_g[l], w_in[l],
                   cmp_pe[l], cmp_w1[l], cmp_b1[l], cmp_w2[l], mla_q_norm_g[l], mla_w_q_up[l],
                   mla_kv_norm_g[l], mla_w_kv_up[l], w_branch_nsa[l], w_branch_mla[l], w_out[l],
                   ffn_pre_g[l], ffn_post_g[l], peer_w_q[l], peer_sub_keys[l], peer_u[l], peer_v[l])
    return x
```

```python
import functools

import numpy as np
import jax
import jax.numpy as jnp
from jax import lax
from jax.experimental import pallas as pl
from jax.experimental.pallas import tpu as pltpu

F32 = jnp.float32
BF16 = jnp.bfloat16
I32 = jnp.int32

LANES = 128
SUBLANES = 8

NSA_HEADS = 16
NSA_GROUPS = 4
NSA_R = NSA_HEADS // NSA_GROUPS
NSA_HEAD_DIM = 64
NSA_ROPE_DIM = NSA_HEAD_DIM // 4
CMP_BLOCK = 32
CMP_STRIDE = 16
CMP_HIDDEN = 256
SEL_BLOCK = 64
SEL_SHIFT = 6
SEL_TOPK = 16
WINDOW = 512
MLA_HEADS = 16
MLA_NOPE_DIM = 64
MLA_ROPE_DIM = 32
MLA_V_DIM = 64
PEER_HEADS = 8
PEER_N_KEYS = 128
PEER_TOPK = 16
ROPE_THETA = 500000.0
NORM_EPS = 1e-6
NEG = -1e30
FORCE_BONUS = 1e4
LOG2E = 1.4426950408889634
MLA_ROPE_OFF = MLA_NOPE_DIM
PROJ_TM = 1024


def _cparams(sem, vmem_mb=None):
    kw = dict(dimension_semantics=sem)
    if vmem_mb is not None:
        kw["vmem_limit_bytes"] = vmem_mb * 1024 * 1024
    return pltpu.CompilerParams(**kw)


def _rms(x, g):
    return x * lax.rsqrt(jnp.mean(x * x, axis=-1, keepdims=True) + NORM_EPS) * g


def _gelu(x):
    return jax.nn.gelu(x, approximate=True)


def _adaln_kernel(c_ref, w_ref, b_ref, o_ref):
    c = c_ref[...]
    s = (c * jax.nn.sigmoid(c)).astype(BF16)
    o_ref[...] = jnp.dot(s, w_ref[...].astype(BF16), preferred_element_type=F32) + b_ref[...]


def _adaln(c, w, b, tn=1024):
    bsz, d = c.shape
    n = w.shape[1]
    cp = jnp.pad(c, ((0, SUBLANES - bsz), (0, 0)))
    out = pl.pallas_call(
        _adaln_kernel,
        grid=(n // tn,),
        in_specs=[pl.BlockSpec((SUBLANES, d), lambda j: (0, 0)),
                  pl.BlockSpec((d, tn), lambda j: (0, j)),
                  pl.BlockSpec((1, tn), lambda j: (0, j))],
        out_specs=pl.BlockSpec((SUBLANES, tn), lambda j: (0, j)),
        out_shape=jax.ShapeDtypeStruct((SUBLANES, n), F32),
        compiler_params=_cparams(("arbitrary",), 40),
        name="adaln",
    )(cp, w, b.reshape(1, n))
    return out[:bsz]


def _rope_table_kernel(pos_ref, inv_ref, sgn_ref, o_ref):
    pos = pos_ref[...]
    for k in range(2):
        ang = pos * inv_ref[k:k + 1, :]
        o_ref[:, (2 * k) * LANES:(2 * k + 1) * LANES] = jnp.cos(ang)
        o_ref[:, (2 * k + 1) * LANES:(2 * k + 2) * LANES] = jnp.sin(ang) * sgn_ref[k:k + 1, :]


def _rope_tables(positions, tm=512):
    n = positions.size
    half_a = NSA_ROPE_DIM // 2
    half_b = MLA_ROPE_DIM // 2
    inv_a = ROPE_THETA ** (-jnp.arange(half_a, dtype=F32) * (2.0 / NSA_ROPE_DIM))
    inv_b = ROPE_THETA ** (-jnp.arange(half_b, dtype=F32) * (2.0 / MLA_ROPE_DIM))
    inv = jnp.zeros((2, LANES), F32)
    inv = inv.at[0, 0:half_a].set(inv_a).at[0, half_a:2 * half_a].set(inv_a)
    o = MLA_ROPE_OFF
    inv = inv.at[1, o:o + half_b].set(inv_b).at[1, o + half_b:o + 2 * half_b].set(inv_b)
    sgn = np.zeros((2, LANES), np.float32)
    sgn[0, 0:half_a] = -1.0
    sgn[0, half_a:2 * half_a] = 1.0
    sgn[1, o:o + half_b] = -1.0
    sgn[1, o + half_b:o + 2 * half_b] = 1.0
    posf = jnp.broadcast_to(positions.reshape(n, 1).astype(F32), (n, LANES))
    return pl.pallas_call(
        _rope_table_kernel,
        grid=(n // tm,),
        in_specs=[pl.BlockSpec((tm, LANES), lambda i: (i, 0)),
                  pl.BlockSpec((2, LANES), lambda i: (0, 0)),
                  pl.BlockSpec((2, LANES), lambda i: (0, 0))],
        out_specs=pl.BlockSpec((tm, 4 * LANES), lambda i: (i, 0)),
        out_shape=jax.ShapeDtypeStruct((n, 4 * LANES), F32),
        compiler_params=_cparams(("arbitrary",)),
        name="rope_tables",
    )(posf, inv, jnp.asarray(sgn))


def _rope_slot(a, cos, sin, half, off):
    lane = lax.broadcasted_iota(I32, (1, LANES), 1)
    first = (lane - off) < half
    partner = jnp.where(first, pltpu.roll(a, LANES - half, 1), pltpu.roll(a, half, 1))
    return a * cos + partner * sin


def _normmod_kernel(x_ref, g_ref, sc_ref, sh_ref, o_ref):
    y = _rms(x_ref[...], g_ref[...])
    o_ref[...] = (y * (1.0 + sc_ref[0]) + sh_ref[0]).astype(o_ref.dtype)


def _normmod(x, g, sc, sh, seq, tm=512):
    n, d = x.shape
    bsz = sc.shape[0]
    bspec = pl.BlockSpec((1, 1, d), lambda i: ((i * tm) // seq, 0, 0))
    return pl.pallas_call(
        _normmod_kernel,
        grid=(n // tm,),
        in_specs=[pl.BlockSpec((tm, d), lambda i: (i, 0)),
                  pl.BlockSpec((1, d), lambda i: (0, 0)), bspec, bspec],
        out_specs=pl.BlockSpec((tm, d), lambda i: (i, 0)),
        out_shape=jax.ShapeDtypeStruct((n, d), BF16),
        compiler_params=_cparams(("arbitrary",)),
        name="normmod",
    )(x, g.reshape(1, d), sc.reshape(bsz, 1, d), sh.reshape(bsz, 1, d))


def _norm_res_kernel(x_ref, y_ref, g_ref, ga_ref, o_ref):
    o_ref[...] = x_ref[...] + ga_ref[0] * _rms(y_ref[...], g_ref[...])


def _norm_res(x, y, g, gate, seq, tm=512):
    n, d = x.shape
    bsz = gate.shape[0]
    row = pl.BlockSpec((tm, d), lambda i: (i, 0))
    return pl.pallas_call(
        _norm_res_kernel,
        grid=(n // tm,),
        in_specs=[row, row, pl.BlockSpec((1, d), lambda i: (0, 0)),
                  pl.BlockSpec((1, 1, d), lambda i: ((i * tm) // seq, 0, 0))],
        out_specs=row,
        out_shape=jax.ShapeDtypeStruct((n, d), F32),
        compiler_params=_cparams(("arbitrary",)),
        name="norm_res",
    )(x, y, g.reshape(1, d), gate.reshape(bsz, 1, d))


def _out_proj_kernel(m_ref, w_ref, x_ref, pg_ref, ga_ref, fg_ref, sc_ref, sh_ref, x1_ref, h2_ref):
    y = jnp.dot(m_ref[...], w_ref[...], preferred_element_type=F32)
    x1 = x_ref[...] + ga_ref[0] * _rms(y, pg_ref[...])
    x1_ref[...] = x1
    h2_ref[...] = (_rms(x1, fg_ref[...]) * (1.0 + sc_ref[0]) + sh_ref[0]).astype(h2_ref.dtype)


def _out_proj(merged, w, x, post_g, gate, pre_g, sc, sh, seq, tm=512):
    n, d = x.shape
    bsz = gate.shape[0]
    row = lambda width: pl.BlockSpec((tm, width), lambda i: (i, 0))
    vec = pl.BlockSpec((1, d), lambda i: (0, 0))
    bvec = pl.BlockSpec((1, 1, d), lambda i: ((i * tm) // seq, 0, 0))
    return pl.pallas_call(
        _out_proj_kernel,
        grid=(n // tm,),
        in_specs=[row(merged.shape[1]), pl.BlockSpec(w.shape, lambda i: (0, 0)), row(d), vec, bvec,
                  vec, bvec, bvec],
        out_specs=[row(d), row(d)],
        out_shape=[jax.ShapeDtypeStruct((n, d), F32), jax.ShapeDtypeStruct((n, d), BF16)],
        compiler_params=_cparams(("arbitrary",), 56),
        name="out_proj",
    )(merged, w, x, post_g.reshape(1, d), gate.reshape(bsz, 1, d), pre_g.reshape(1, d),
      sc.reshape(bsz, 1, d), sh.reshape(bsz, 1, d))


def _mm(x, w, epi, *, tm, tn, out_dtype, name, extras=(), pro=None, x_spec=None, vmem_mb=None):
    m = x.shape[0]
    k, nc = w.shape
    if x_spec is None:
        x_spec = pl.BlockSpec((tm, k), lambda i, j: (i, 0))
    in_specs = [x_spec, pl.BlockSpec((k, tn), lambda i, j: (0, j))]
    args = [x, w]
    for arr, spec in extras:
        in_specs.append(spec)
        args.append(arr)

    def kern(x_ref, w_ref, *rest):
        o_ref = rest[-1]
        ex = rest[:-1]
        xv = x_ref[...]
        if pro is not None:
            xv = pro(xv, *ex)
        acc = jnp.dot(xv, w_ref[...], preferred_element_type=F32)
        epi(acc, o_ref, *ex)

    return pl.pallas_call(
        kern,
        grid=(m // tm, nc // tn),
        in_specs=in_specs,
        out_specs=pl.BlockSpec((tm, tn), lambda i, j: (i, j)),
        out_shape=jax.ShapeDtypeStruct((m, nc), out_dtype),
        compiler_params=_cparams(("arbitrary", "arbitrary"), vmem_mb),
        name=name,
    )(*args)


def _epi_plain(acc, o_ref, *ex):
    o_ref[...] = acc.astype(o_ref.dtype)


def _make_epi_rope_a(scale, seq=None, tm=None, onehot=False):
    half = NSA_ROPE_DIM // 2

    def epi(acc, o_ref, tab_ref):
        cos = tab_ref[:, 0:LANES]
        sin = tab_ref[:, LANES:2 * LANES]
        if onehot:
            base = lax.rem(pl.program_id(0) * tm, seq)
            t = base + lax.broadcasted_iota(I32, (acc.shape[0], LANES), 0)
            lane = lax.broadcasted_iota(I32, (acc.shape[0], LANES), 1)
            hot = jnp.where(lane - SEL_BLOCK == jnp.right_shift(t, SEL_SHIFT), 1.0, 0.0)
        for s in range(acc.shape[1] // LANES):
            r = _rope_slot(acc[:, s * LANES:(s + 1) * LANES], cos, sin, half, 0)
            if scale != 1.0:
                r = r * scale
            if onehot:
                r = r + hot
            o_ref[:, s * LANES:(s + 1) * LANES] = r.astype(o_ref.dtype)
    return epi


def _epi_rope_b_scaled(acc, o_ref, g_ref, tab_ref):
    cos = tab_ref[:, 2 * LANES:3 * LANES]
    sin = tab_ref[:, 3 * LANES:4 * LANES]
    scale = (MLA_NOPE_DIM + MLA_ROPE_DIM) ** -0.5 * LOG2E
    for s in range(acc.shape[1] // LANES):
        r = _rope_slot(acc[:, s * LANES:(s + 1) * LANES], cos, sin, MLA_ROPE_DIM // 2, MLA_ROPE_OFF)
        o_ref[:, s * LANES:(s + 1) * LANES] = (r * scale).astype(o_ref.dtype)


def _epi_gate_kr(acc, o_ref, tab_ref):
    o_ref[:, 0:LANES] = jax.nn.sigmoid(acc[:, 0:LANES]).astype(o_ref.dtype)
    cos = tab_ref[:, 2 * LANES:3 * LANES]
    sin = tab_ref[:, 3 * LANES:4 * LANES]
    r = _rope_slot(acc[:, LANES:2 * LANES], cos, sin, MLA_ROPE_DIM // 2, MLA_ROPE_OFF)
    o_ref[:, LANES:2 * LANES] = r.astype(o_ref.dtype)


def _pro_rms(xv, g_ref, *ex):
    return _rms(xv, g_ref[...]).astype(BF16)


def _epi_add_kr(acc, o_ref, g_ref, kr_ref):
    kr = kr_ref[...].astype(F32)
    for s in range(acc.shape[1] // LANES):
        o_ref[:, s * LANES:(s + 1) * LANES] = (acc[:, s * LANES:(s + 1) * LANES] + kr).astype(o_ref.dtype)


def _epi_plain_g(acc, o_ref, g_ref):
    o_ref[...] = acc.astype(o_ref.dtype)


def _compress_kernel(k16_ref, v16_ref, w1_ref, pe_ref, b1_ref, w2_ref, kc_ref, vc_ref):
    half = w1_ref.shape[1] // 2
    for which, (src, dst) in enumerate(((k16_ref, kc_ref), (v16_ref, vc_ref))):
        xb = src[...]
        a = jnp.dot(xb, w1_ref[which, 0:half, :], preferred_element_type=F32)
        b = jnp.dot(xb, w1_ref[which, half:2 * half, :], preferred_element_type=F32)
        c = jnp.dot(pe_ref[which], w1_ref[which], preferred_element_type=F32)[0:1, :] + b1_ref[which]
        rows = a.shape[0]
        pre = a + pltpu.roll(b, rows - 1, 0) + c
        hid = _gelu(pre).astype(BF16)
        dst[...] = jnp.dot(hid, w2_ref[which], preferred_element_type=F32).astype(dst.dtype)


def _compress(k16, v16, w1, pe, b1, w2p):
    bg, rows, feat = k16.shape
    blk = pl.BlockSpec((None, rows, feat), lambda i: (i, 0, 0))
    full = lambda a: pl.BlockSpec(a.shape, lambda i: (0,) * a.ndim)
    out = pl.BlockSpec((None, rows, LANES), lambda i: (i, 0, 0))
    return pl.pallas_call(
        _compress_kernel,
        grid=(bg,),
        in_specs=[blk, blk, full(w1), full(pe), full(b1), full(w2p)],
        out_specs=[out, out],
        out_shape=[jax.ShapeDtypeStruct((bg, rows, LANES), BF16)] * 2,
        compiler_params=_cparams(("arbitrary",)),
        name="nsa_compress",
    )(k16, v16, w1, pe, b1, w2p)


def _row_bcast(col, width):
    if width % LANES:
        return jnp.broadcast_to(col, (col.shape[0], width))
    tile = jnp.broadcast_to(col, (col.shape[0], LANES))
    return jnp.concatenate([tile] * (width // LANES), axis=1)


def _cmp_attn_kernel(q_ref, kc_ref, vc_ref, ovt_ref, o_ref, bias_ref, *, tq, n_sel):
    qi = pl.program_id(2)
    ncmp = kc_ref.shape[0]
    t_row = qi * tq + lax.broadcasted_iota(I32, (tq, ncmp), 0)
    n_col = lax.broadcasted_iota(I32, (tq, ncmp), 1)
    cmask = (n_col * CMP_STRIDE + (CMP_BLOCK - 1)) <= t_row
    kc = kc_ref[...]
    vc = vc_ref[...]
    imp_t = jnp.zeros((n_sel, tq), F32)
    outs = []
    for r in range(NSA_R):
        q = q_ref[:, r * LANES:(r + 1) * LANES]
        s = lax.dot_general(q, kc, (((1,), (1,)), ((), ())), preferred_element_type=F32)
        s = jnp.where(cmask, s, NEG)
        e = jnp.exp2(s - _row_bcast(jnp.max(s, axis=1, keepdims=True), ncmp))
        p = e / _row_bcast(jnp.sum(e, axis=1, keepdims=True), ncmp)
        p = jnp.where(cmask, p, 0.0).astype(BF16)
        outs.append(jnp.dot(p, vc, preferred_element_type=F32))
        imp_t = imp_t + lax.dot_general(ovt_ref[...], p, (((1,), (1,)), ((), ())),
                                        preferred_element_type=F32)
    _store_heads_compact(o_ref, outs)
    blk = lax.broadcasted_iota(I32, (n_sel, tq), 0)
    t = qi * tq + lax.broadcasted_iota(I32, (n_sel, tq), 1)
    tb = jnp.right_shift(t, SEL_SHIFT)
    forced = (blk == 0) | (blk == tb) | (blk == tb - 1)
    valid = blk * SEL_BLOCK <= t
    x = jnp.where(valid, imp_t + jnp.where(forced, FORCE_BONUS, 0.0), NEG)
    sel = jnp.zeros((n_sel, tq), F32)
    for _ in range(min(SEL_TOPK, n_sel)):
        m = jnp.max(x, axis=0, keepdims=True)
        idx = jnp.min(jnp.where(x == m, blk, n_sel), axis=0, keepdims=True)
        hit = blk == idx
        sel = jnp.where(hit, 1.0, sel)
        x = jnp.where(hit, -jnp.inf, x)
    bias_t = jnp.where(sel > 0.5, 0.0, NEG)
    parts = [jnp.zeros((SEL_BLOCK, tq), F32), bias_t]
    if LANES - SEL_BLOCK - n_sel > 0:
        parts.append(jnp.zeros((LANES - SEL_BLOCK - n_sel, tq), F32))
    full_t = jnp.concatenate(parts, axis=0)
    bias_ref[...] = full_t.T.astype(bias_ref.dtype)


def _cmp_attn(qn, kc, vc, ovt, bsz, seq, tq=256):
    n = qn.shape[0]
    nq = seq // tq
    n_sel = seq // SEL_BLOCK
    ncmp = kc.shape[1]
    qspec = pl.BlockSpec((tq, NSA_R * LANES), lambda b, g, qi: (b * nq + qi, g))
    kspec = pl.BlockSpec((None, ncmp, LANES), lambda b, g, qi: (b * NSA_GROUPS + g, 0, 0))
    return pl.pallas_call(
        functools.partial(_cmp_attn_kernel, tq=tq, n_sel=n_sel),
        grid=(bsz, NSA_GROUPS, nq),
        in_specs=[qspec, kspec, kspec, pl.BlockSpec(ovt.shape, lambda b, g, qi: (0, 0))],
        out_specs=[pl.BlockSpec((tq, NSA_R * NSA_HEAD_DIM), lambda b, g, qi: (b * nq + qi, g)),
                   pl.BlockSpec((tq, LANES), lambda b, g, qi: (b * nq + qi, g))],
        out_shape=[jax.ShapeDtypeStruct((n, NSA_HEADS * NSA_HEAD_DIM), BF16),
                   jax.ShapeDtypeStruct((n, NSA_GROUPS * LANES), BF16)],
        compiler_params=_cparams(("arbitrary",) * 3),
        name="nsa_cmp_attn",
    )(qn, kc, vc, ovt)


def _store_heads_compact(o_ref, heads):
    hd = NSA_HEAD_DIM
    for p in range(len(heads) // 2):
        pair = jnp.concatenate([heads[2 * p][:, 0:hd], heads[2 * p + 1][:, 0:hd]], axis=1)
        o_ref[:, p * LANES:(p + 1) * LANES] = pair.astype(o_ref.dtype)


def _flash_kernel(*refs, mode, tq, tk, nr, nh, window):
    if mode == "sel":
        q_ref, k_ref, v_ref, bias_ref, o_ref, qs, m_s, l_s, acc_s = refs
    else:
        q_ref, k_ref, v_ref, o_ref, qs, m_s, l_s, acc_s = refs
    qi = pl.program_id(2)
    kk = pl.program_id(3)
    nk = pl.num_programs(3)
    rows = nr * tq

    @pl.when(kk == 0)
    def _init():
        for r in range(nh * nr):
            qr = q_ref[:, r * LANES:(r + 1) * LANES]
            if mode == "sel":
                qr = qr + bias_ref[:, (r // nr) * LANES:(r // nr + 1) * LANES]
            qs[r * tq:(r + 1) * tq, :] = qr
        m_s[...] = jnp.full(m_s.shape, -jnp.inf, F32)
        l_s[...] = jnp.zeros(l_s.shape, F32)
        acc_s[...] = jnp.zeros(acc_s.shape, F32)

    if mode == "win":
        kidx = qi * (tq // tk) - window // tk + kk
        needed = kidx >= 0
        full_vis = (kidx * tk + tk - 1 <= qi * tq) & (qi * tq + tq - 1 - kidx * tk < window)
    else:
        kidx = kk
        needed = kk * tk <= qi * tq + tq - 1
        full_vis = kk * tk + tk - 1 <= qi * tq

    nt = tk // LANES

    def update(masked):
        if masked:
            rel = (lax.broadcasted_iota(I32, (rows, tk), 0) & (tq - 1)) - lax.broadcasted_iota(I32, (rows, tk), 1)
            off = kidx * tk - qi * tq
            vis = rel >= off
            if mode == "win":
                vis = vis & (rel < off + window)
        for hh in range(nh):
            rs = slice(hh * rows, (hh + 1) * rows)
            cs = slice(hh * LANES, (hh + 1) * LANES)
            s = lax.dot_general(qs[rs, :], k_ref[:, cs], (((1,), (1,)), ((), ())),
                                preferred_element_type=F32)
            if masked:
                s = jnp.where(vis, s, NEG)
            m_prev = m_s[rs, :]
            m_new = jnp.maximum(m_prev, jnp.max(s, axis=1, keepdims=True))
            p = jnp.exp2(s - jnp.concatenate([m_new] * nt, axis=1))
            alpha = jnp.exp2(m_prev - m_new)
            psum = p[:, 0:LANES]
            for c in range(1, nt):
                psum = psum + p[:, c * LANES:(c + 1) * LANES]
            l_s[rs, :] = alpha * l_s[rs, :] + psum
            acc_s[rs, :] = alpha * acc_s[rs, :] + jnp.dot(p.astype(BF16), v_ref[:, cs],
                                                          preferred_element_type=F32)
            m_s[rs, :] = m_new

    @pl.when(needed & full_vis)
    def _full():
        update(False)

    @pl.when(needed & jnp.logical_not(full_vis))
    def _edge():
        update(True)

    @pl.when(kk == nk - 1)
    def _fin():
        out = acc_s[...] / jnp.sum(l_s[...], axis=1, keepdims=True)
        _store_heads_compact(o_ref, [out[r * tq:(r + 1) * tq, :] for r in range(nh * nr)])


def _flash(q, k, v, *, mode, bsz, seq, nslots, nr, tq, tk, nh=1, k_off=0, v_off=0, bias=None,
           name):
    assert tq & (tq - 1) == 0
    n = q.shape[0]
    nq = seq // tq
    nkb = seq // tk
    if mode == "win":
        assert tq % tk == 0 and WINDOW % tk == 0
        steps = WINDOW // tk + tq // tk

        def krow(b, qi, kk):
            return b * nkb + jnp.maximum(qi * (tq // tk) - WINDOW // tk + kk, 0)
    else:
        steps = nkb

        def krow(b, qi, kk):
            return b * nkb + jnp.minimum(kk, (qi * tq + tq - 1) // tk)

    qspec = pl.BlockSpec((tq, nh * nr * LANES), lambda b, g, qi, kk: (b * nq + qi, g))
    in_specs = [qspec,
                pl.BlockSpec((tk, nh * LANES), lambda b, g, qi, kk: (krow(b, qi, kk), k_off // nh + g)),
                pl.BlockSpec((tk, nh * LANES), lambda b, g, qi, kk: (krow(b, qi, kk), v_off // nh + g))]
    args = [q, k, v]
    if mode == "sel":
        in_specs.append(pl.BlockSpec((tq, nh * LANES), lambda b, g, qi, kk: (b * nq + qi, g)))
        args.append(bias)
    rows = nh * nr * tq
    ow = nh * nr * NSA_HEAD_DIM
    return pl.pallas_call(
        functools.partial(_flash_kernel, mode=mode, tq=tq, tk=tk, nr=nr, nh=nh, window=WINDOW),
        grid=(bsz, nslots, nq, steps),
        in_specs=in_specs,
        out_specs=pl.BlockSpec((tq, ow), lambda b, g, qi, kk: (b * nq + qi, g)),
        out_shape=jax.ShapeDtypeStruct((n, nslots * ow), BF16),
        scratch_shapes=[pltpu.VMEM((rows, LANES), BF16), pltpu.VMEM((rows, LANES), F32),
                        pltpu.VMEM((rows, LANES), F32), pltpu.VMEM((rows, LANES), F32)],
        compiler_params=_cparams(("arbitrary",) * 4, 48),
        name=name,
    )(*args)


def _combine_kernel(oc_ref, os_ref, ow_ref, g_ref, e_ref, o_ref):
    g = g_ref[...]
    acc = None
    for br, src in enumerate((oc_ref, os_ref, ow_ref)):
        ge = jnp.dot(g, e_ref[br], preferred_element_type=F32)
        term = ge * src[...].astype(F32)
        acc = term if acc is None else acc + term
    o_ref[...] = acc.astype(o_ref.dtype)


def _combine(o_cmp, o_sel, o_win, gates, gates_col, tm=512):
    n, w = o_cmp.shape
    e = np.zeros((3, LANES, w), np.float32)
    for h in range(NSA_HEADS):
        for br in range(3):
            e[br, h * 3 + br, h * NSA_HEAD_DIM:(h + 1) * NSA_HEAD_DIM] = 1.0
    row = pl.BlockSpec((tm, w), lambda i: (i, 0))
    return pl.pallas_call(
        _combine_kernel,
        grid=(n // tm,),
        in_specs=[row, row, row, pl.BlockSpec((tm, LANES), lambda i: (i, gates_col)),
                  pl.BlockSpec(e.shape, lambda i: (0, 0, 0))],
        out_specs=row,
        out_shape=jax.ShapeDtypeStruct((n, w), BF16),
        compiler_params=_cparams(("arbitrary",)),
        name="nsa_combine",
    )(o_cmp, o_sel, o_win, gates, jnp.asarray(e, BF16))


def _merge_kernel(h_ref, wg0_ref, wg1_ref, a_ref, wa_ref, b_ref, wb_ref, o_ref):
    h = h_ref[...]
    g0 = jax.nn.sigmoid(jnp.dot(h, wg0_ref[...], preferred_element_type=F32))
    g1 = jax.nn.sigmoid(jnp.dot(h, wg1_ref[...], preferred_element_type=F32))
    ya = jnp.dot(a_ref[...], wa_ref[...], preferred_element_type=F32)
    yb = jnp.dot(b_ref[...], wb_ref[...], preferred_element_type=F32)
    o_ref[...] = (g0 * ya + g1 * yb).astype(o_ref.dtype)


def _merge(h, w_gm, o_nsa, w_nsa, o_mla, w_mla, tm, tn=512):
    n, k = o_nsa.shape
    kh = h.shape[1]
    d = w_nsa.shape[1]
    nj = d // tn
    row = pl.BlockSpec((tm, k), lambda i, j: (i, 0))
    wsp = pl.BlockSpec((k, tn), lambda i, j: (0, j))
    return pl.pallas_call(
        _merge_kernel,
        grid=(n // tm, nj),
        in_specs=[pl.BlockSpec((tm, kh), lambda i, j: (i, 0)),
                  pl.BlockSpec((kh, tn), lambda i, j: (0, j)),
                  pl.BlockSpec((kh, tn), lambda i, j: (0, nj + j)),
                  row, wsp, row, wsp],
        out_specs=pl.BlockSpec((tm, tn), lambda i, j: (i, j)),
        out_shape=jax.ShapeDtypeStruct((n, d), BF16),
        compiler_params=_cparams(("arbitrary", "arbitrary"), 48),
        name="branch_merge",
    )(h, w_gm, w_gm, o_nsa, w_nsa, o_mla, w_mla)


def _merge_desc(xs):
    xs = list(xs)
    n = len(xs)
    stride = n // 2
    while stride >= 1:
        for i in range(n):
            p = i ^ stride
            if p > i:
                xs[i], xs[p] = jnp.maximum(xs[i], xs[p]), jnp.minimum(xs[i], xs[p])
        stride //= 2
    return xs


def _top16_sorted(xs):
    xs = list(xs)
    n = len(xs)
    size = 2
    while size <= n:
        stride = size // 2
        while stride >= 1:
            for i in range(n):
                p = i ^ stride
                if p > i:
                    hi, lo = jnp.maximum(xs[i], xs[p]), jnp.minimum(xs[i], xs[p])
                    xs[i], xs[p] = (hi, lo) if (i & size) == 0 else (lo, hi)
            stride //= 2
        size *= 2
    shift = SUBLANES // 2
    while shift >= 1:
        ys = [pltpu.roll(x, shift, 0) for x in xs]
        xs = _merge_desc([jnp.maximum(xs[k], ys[n - 1 - k]) for k in range(n)])
        shift //= 2
    return xs


def _prefix_len(pred, vs):
    p8 = pred(vs[7])
    p4 = pred(jnp.where(p8, vs[11], vs[3]))
    p2 = pred(jnp.where(p8, jnp.where(p4, vs[13], vs[9]), jnp.where(p4, vs[5], vs[1])))
    q = [jnp.where(p2, vs[4 * i + 2], vs[4 * i]) for i in range(4)]
    p1 = pred(jnp.where(p8, jnp.where(p4, q[3], q[2]), jnp.where(p4, q[1], q[0])))
    n = (jnp.where(p8, 8.0, 0.0) + jnp.where(p4, 4.0, 0.0)) + (jnp.where(p2, 2.0, 0.0) + jnp.where(p1, 1.0, 0.0))
    return jnp.where(pred(vs[15]), 16.0, n)


def _rows8(x):
    return [x[k * SUBLANES:(k + 1) * SUBLANES, :] for k in range(x.shape[0] // SUBLANES)]


def _peer_route_kernel(q_ref, sk_ref, tb_ref, tf_ref):
    nk = PEER_N_KEYS
    sec = PEER_HEADS * nk
    tr = q_ref.shape[0]
    sub = lax.broadcasted_iota(I32, (SUBLANES, tr), 0)
    ninf = jnp.full((SUBLANES, tr), -jnp.inf, F32)

    def spread(vs):
        out = vs[SUBLANES - 1]
        for j in range(SUBLANES - 2, -1, -1):
            out = jnp.where(sub == j, vs[j], out)
        return out

    for h in range(PEER_HEADS):
        s = []
        for p in range(2):
            c = (2 * h + p) * nk
            s.append(lax.dot_general(sk_ref[2 * h + p], q_ref[:, c:c + nk], (((1,), (1,)), ((), ())),
                                     preferred_element_type=F32))
        x1, x2 = _rows8(s[0]), _rows8(s[1])
        v1, v2 = _top16_sorted(x1), _top16_sorted(x2)
        v2lo, v2hi = spread(v2[:SUBLANES]), spread(v2[SUBLANES:])
        cands = [v1[0] + v2lo, v1[0] + v2hi]
        for i in range(1, SUBLANES):
            c = v1[i] + v2lo
            cnt = PEER_TOPK // (i + 1)
            cands.append(jnp.where(sub < cnt, c, -jnp.inf) if cnt < SUBLANES else c)
        cands.append(spread(v1[SUBLANES:]) + v2[0])
        top = _top16_sorted(cands + [ninf] * (PEER_TOPK - len(cands)))
        tau = top[PEER_TOPK - 1]
        z = None
        for k in range(PEER_TOPK):
            ek = jnp.exp(top[k] - top[0])
            z = ek if z is None else z + ek
        rank2 = jnp.concatenate([_prefix_len(lambda v, x=x: v > x, v2) for x in x2], axis=0)
        count = jnp.concatenate([_prefix_len(lambda v, x=x: (x + v) >= tau, v2) for x in x1], axis=0)
        tb_ref[h * nk:(h + 1) * nk, :] = rank2.astype(tb_ref.dtype)
        tb_ref[sec + h * nk:sec + (h + 1) * nk, :] = jnp.exp(s[1] - v2[0][0:1, :]).astype(tb_ref.dtype)
        tf_ref[h * nk:(h + 1) * nk, :] = count
        tf_ref[sec + h * nk:sec + (h + 1) * nk, :] = jnp.exp(s[0] - v1[0][0:1, :]) / z[0:1, :]


def _peer_route(qry, sk, tr=256):
    n, w = qry.shape
    col = pl.BlockSpec((w, tr), lambda i: (0, i))
    return pl.pallas_call(
        _peer_route_kernel,
        grid=(n // tr,),
        in_specs=[pl.BlockSpec((tr, w), lambda i: (i, 0)),
                  pl.BlockSpec(sk.shape, lambda i: (0, 0, 0))],
        out_specs=[col, col],
        out_shape=[jax.ShapeDtypeStruct((w, n), BF16), jax.ShapeDtypeStruct((w, n), F32)],
        compiler_params=_cparams(("arbitrary",), 40),
        name="peer_route",
    )(qry, sk)


PEER_CHUNK = 32
PEER_TOKEN_GROUPS = 1


def _peer_dense_kernel(h_ref, u_ref, vt_ref, tb_ref, tf_ref, o_ref, acc_s, pt_s, *, te):
    j = pl.program_id(1)
    nk = PEER_N_KEYS
    sec = PEER_HEADS * nk
    gdt = pt_s.dtype

    @pl.when(j == 0)
    def _init():
        acc_s[...] = jnp.zeros(acc_s.shape, F32)

    def row_tile(idx, cols):
        r = tf_ref[pl.ds(idx, 1), cols]
        return jnp.broadcast_to(r.astype(gdt), (PEER_CHUNK, r.shape[1]))

    tm = h_ref.shape[0]
    tg = tm // PEER_TOKEN_GROUPS
    for grp in range(PEER_TOKEN_GROUPS):
        cols = slice(grp * tg, (grp + 1) * tg)
        zt = lax.dot_general(u_ref[...], h_ref[cols, :], (((1,), (1,)), ((), ())),
                             preferred_element_type=F32)
        for al in range(te // nk):
            a = j * (te // nk) + al
            cnt = [row_tile(h * nk + a, cols) for h in range(PEER_HEADS)]
            e1 = [row_tile(sec + h * nk + a, cols) for h in range(PEER_HEADS)]
            for c in range(nk // PEER_CHUNK):
                lo = c * PEER_CHUNK
                g = None
                for h in range(PEER_HEADS):
                    rank2 = tb_ref[h * nk + lo:h * nk + lo + PEER_CHUNK, cols]
                    e2 = tb_ref[sec + h * nk + lo:sec + h * nk + lo + PEER_CHUNK, cols]
                    term = jnp.where(rank2 < cnt[h], e1[h] * e2, jnp.zeros((), gdt))
                    g = term if g is None else g + term
                r0 = al * nk + lo
                pt_s[r0:r0 + PEER_CHUNK, cols] = g * _gelu(zt[r0:r0 + PEER_CHUNK, :]).astype(gdt)
        acc_s[:, cols] += jnp.dot(vt_ref[...], pt_s[:, cols], preferred_element_type=F32)

    @pl.when(j == pl.num_programs(1) - 1)
    def _fin():
        o_ref[...] = acc_s[...].T


def _peer_dense(h2, u_bf, vt_bf, tb, tf, tm=512, te=1024):
    n, d = h2.shape
    ne = u_bf.shape[0]
    w = tb.shape[0]
    return pl.pallas_call(
        functools.partial(_peer_dense_kernel, te=te),
        grid=(n // tm, ne // te),
        in_specs=[pl.BlockSpec((tm, d), lambda i, j: (i, 0)),
                  pl.BlockSpec((te, d), lambda i, j: (j, 0)),
                  pl.BlockSpec((d, te), lambda i, j: (0, j)),
                  pl.BlockSpec((w, tm), lambda i, j: (0, i)),
                  pl.BlockSpec((w, tm), lambda i, j: (0, i))],
        out_specs=pl.BlockSpec((tm, d), lambda i, j: (i, 0)),
        out_shape=jax.ShapeDtypeStruct((n, d), F32),
        scratch_shapes=[pltpu.VMEM((d, tm), F32), pltpu.VMEM((te, tm), BF16)],
        compiler_params=_cparams(("arbitrary", "arbitrary"), 56),
        name="peer_dense",
    )(h2, u_bf, vt_bf, tb, tf)


def _slots(w, n, width, off=0):
    k = w.shape[0]
    w = w.reshape(k, n, width)
    w = jnp.pad(w, ((0, 0), (0, 0), (off, LANES - width - off)))
    return w.reshape(k, n * LANES)


def _overlap_t(seq):
    n_cmp_rows = seq // CMP_STRIDE
    n_sel = seq // SEL_BLOCK
    cs = np.arange(n_cmp_rows) * CMP_STRIDE
    ce = cs + CMP_BLOCK - 1
    ss = np.arange(n_sel) * SEL_BLOCK
    ov = (cs[None, :] < ss[:, None] + SEL_BLOCK) & (ce[None, :] >= ss[:, None])
    ov[:, n_cmp_rows - 1] = False
    return jnp.asarray(ov.astype(np.float32), BF16)


def _token_mixer(h, positions, bsz, seq, w_in, cmp_pe, cmp_w1, cmp_b1, cmp_w2, mla_q_norm_g,
                 mla_w_q_up, mla_kv_norm_g, mla_w_kv_up, w_branch_nsa, w_branch_mla):
    d = h.shape[1]
    g_, hd = NSA_GROUPS, NSA_HEAD_DIM
    tab = _rope_tables(positions)
    tm = min(PROJ_TM, seq)
    tab_spec = pl.BlockSpec((tm, 4 * LANES), lambda i, j: (i, 0))

    wb = w_in.astype(BF16)
    o1 = NSA_HEADS * hd
    kv = wb[:, o1:o1 + 3 * 2 * g_ * hd].reshape(d, 3, 2, g_ * hd)
    o2 = o1 + 3 * 2 * g_ * hd
    o3 = o2 + 3 * NSA_HEADS
    mq = mla_w_q_up.shape[0]
    mkv = mla_w_kv_up.shape[0]
    o4 = o3 + mq
    o5 = o4 + mkv
    o6 = o5 + MLA_ROPE_DIM
    w_q = _slots(wb[:, :o1], NSA_HEADS, hd)
    w_kk = jnp.concatenate([_slots(kv[:, 0, 0], g_, hd), _slots(kv[:, 2, 0], g_, hd)], axis=1)
    w_ks = _slots(kv[:, 1, 0], g_, hd)
    w_vv = jnp.concatenate([_slots(kv[:, 1, 1], g_, hd), _slots(kv[:, 2, 1], g_, hd),
                            kv[:, 0, 1], jnp.zeros((d, g_ * hd), BF16)], axis=1)
    w_c = wb[:, o3:o5]
    w_gk = jnp.concatenate([jnp.pad(wb[:, o2:o3], ((0, 0), (0, LANES - 3 * NSA_HEADS))),
                            _slots(wb[:, o5:o6], 1, MLA_ROPE_DIM, MLA_ROPE_OFF)], axis=1)
    w_gm = wb[:, o6:]

    qn = _mm(h, w_q, _make_epi_rope_a(hd ** -0.5 * LOG2E), tm=tm, tn=512, out_dtype=BF16, name="proj_q",
             extras=[(tab, tab_spec)])
    kk = _mm(h, w_kk, _make_epi_rope_a(1.0), tm=tm, tn=512, out_dtype=BF16, name="proj_k",
             extras=[(tab, tab_spec)])
    ks = _mm(h, w_ks, _make_epi_rope_a(1.0, seq, tm, True), tm=tm, tn=512, out_dtype=BF16,
             name="proj_ksel", extras=[(tab, tab_spec)])
    vv = _mm(h, w_vv, _epi_plain, tm=tm, tn=512, out_dtype=BF16, name="proj_v")
    cqkv = _mm(h, w_c, _epi_plain, tm=tm, tn=mq + mkv, out_dtype=F32, name="proj_c")
    gk = _mm(h, w_gk, _epi_gate_kr, tm=tm, tn=2 * LANES, out_dtype=BF16, name="proj_gate_kr",
             extras=[(tab, tab_spec)])

    def to16(a, width):
        a = a.reshape(bsz, seq, g_, width)[..., :hd]
        a = jnp.transpose(a, (0, 2, 1, 3))
        return a.reshape(bsz * g_, seq // CMP_STRIDE, CMP_STRIDE * hd)

    k16 = to16(kk[:, :g_ * LANES], LANES)
    v16 = to16(vv[:, 2 * g_ * LANES:2 * g_ * LANES + g_ * hd], hd)
    pe = jnp.pad(cmp_pe.reshape(2, 1, CMP_BLOCK * hd), ((0, 0), (0, SUBLANES - 1), (0, 0))).astype(BF16)
    w2p = jnp.pad(cmp_w2, ((0, 0), (0, 0), (0, LANES - hd))).astype(BF16)
    kc, vc = _compress(k16, v16, cmp_w1.astype(BF16), pe, cmp_b1.reshape(2, 1, CMP_HIDDEN), w2p)
    o_cmp, bias = _cmp_attn(qn, kc, vc, _overlap_t(seq), bsz, seq, tq=min(512, seq))
    o_sel = _flash(qn, ks, vv, mode="sel", bsz=bsz, seq=seq, nslots=g_ // 2, nr=NSA_R, nh=2,
                   tq=512, tk=512, bias=bias, name="nsa_sel_attn")
    o_win = _flash(qn, kk, vv, mode="win", bsz=bsz, seq=seq, nslots=g_ // 2, nr=NSA_R, nh=2,
                   tq=256, tk=256, k_off=g_, v_off=g_, name="nsa_win_attn")
    o_nsa = _combine(o_cmp, o_sel, o_win, gk, 0)

    qd = MLA_NOPE_DIM + MLA_ROPE_DIM
    w_qup = _slots(mla_w_q_up.astype(BF16), MLA_HEADS, qd)
    kvu = mla_w_kv_up.astype(BF16).reshape(mkv, MLA_HEADS, MLA_NOPE_DIM + MLA_V_DIM)
    w_kup = _slots(kvu[:, :, :MLA_NOPE_DIM].reshape(mkv, -1), MLA_HEADS, MLA_NOPE_DIM)
    w_vup = _slots(kvu[:, :, MLA_NOPE_DIM:].reshape(mkv, -1), MLA_HEADS, MLA_V_DIM)
    gq_spec = pl.BlockSpec((1, mq), lambda i, j: (0, 0))
    gkv_spec = pl.BlockSpec((1, mkv), lambda i, j: (0, 0))
    ckv_spec = pl.BlockSpec((tm, mkv), lambda i, j: (i, mq // mkv))
    q_mla = _mm(cqkv, w_qup, _epi_rope_b_scaled, tm=tm, tn=512,
                out_dtype=BF16, name="mla_q_up", pro=_pro_rms,
                x_spec=pl.BlockSpec((tm, mq), lambda i, j: (i, 0)),
                extras=[(mla_q_norm_g.reshape(1, mq), gq_spec), (tab, tab_spec)])
    k_mla = _mm(cqkv, w_kup, _epi_add_kr, tm=tm, tn=512, out_dtype=BF16, name="mla_k_up",
                pro=_pro_rms, x_spec=ckv_spec,
                extras=[(mla_kv_norm_g.reshape(1, mkv), gkv_spec),
                        (gk, pl.BlockSpec((tm, LANES), lambda i, j: (i, 1)))])
    v_mla = _mm(cqkv, w_vup, _epi_plain_g, tm=tm, tn=512, out_dtype=BF16, name="mla_v_up",
                pro=_pro_rms, x_spec=ckv_spec,
                extras=[(mla_kv_norm_g.reshape(1, mkv), gkv_spec)])
    o_mla = _flash(q_mla, k_mla, v_mla, mode="causal", bsz=bsz, seq=seq, nslots=MLA_HEADS // 8, nr=1,
                   nh=8, tq=512, tk=512, name="mla_attn")

    return _merge(h, w_gm, o_nsa, w_branch_nsa.astype(BF16), o_mla, w_branch_mla.astype(BF16), tm)


def _peer(h2, peer_w_q, peer_sub_keys, peer_u, peer_v):
    qry = _mm(h2, peer_w_q.astype(BF16), _epi_plain, tm=min(PROJ_TM, h2.shape[0]), tn=512,
              out_dtype=BF16, name="peer_q")
    sk = peer_sub_keys.astype(BF16).reshape(2 * PEER_HEADS, PEER_N_KEYS, -1)
    tb, tf = _peer_route(qry, sk)
    return _peer_dense(h2, peer_u.astype(BF16), jnp.transpose(peer_v).astype(BF16), tb, tf)


def _block(x, c, positions, ada_w, ada_b, attn_pre_g, attn_post_g, w_in, cmp_pe, cmp_w1, cmp_b1,
           cmp_w2, mla_q_norm_g, mla_w_q_up, mla_kv_norm_g, mla_w_kv_up, w_branch_nsa, w_branch_mla,
           w_out, ffn_pre_g, ffn_post_g, peer_w_q, peer_sub_keys, peer_u, peer_v):
    bsz, seq, d = x.shape
    x2 = x.reshape(bsz * seq, d)
    mod = _adaln(c, ada_w, ada_b)
    sh_a, sc_a, g_a, sh_f, sc_f, g_f = [mod[:, i * d:(i + 1) * d] for i in range(6)]

    h = _normmod(x2, attn_pre_g, sc_a, sh_a, seq)
    merged = _token_mixer(h, positions, bsz, seq, w_in, cmp_pe, cmp_w1, cmp_b1, cmp_w2, mla_q_norm_g,
                          mla_w_q_up, mla_kv_norm_g, mla_w_kv_up, w_branch_nsa, w_branch_mla)
    x1, h2 = _out_proj(merged, w_out.astype(BF16), x2, attn_post_g, g_a, ffn_pre_g, sc_f, sh_f, seq)

    y_ffn = _peer(h2, peer_w_q, peer_sub_keys, peer_u, peer_v)
    out = _norm_res(x1, y_ffn, ffn_post_g, g_f, seq)
    return out.reshape(bsz, seq, d)


def kernel(x, c, positions, ada_w, ada_b, attn_pre_g, attn_post_g, w_in, cmp_pe, cmp_w1, cmp_b1,
           cmp_w2, mla_q_norm_g, mla_w_q_up, mla_kv_norm_g, mla_w_kv_up, w_branch_nsa, w_branch_mla,
           w_out, ffn_pre_g, ffn_post_g, peer_w_q, peer_sub_keys, peer_u, peer_v):
    depth = ada_w.shape[0]
    for l in range(depth):
        x = _block(x, c, positions, ada_w[l], ada_b[l], attn_pre_g[l], attn_post_g[l], w_in[l],
                   cmp_pe[l], cmp_w1[l], cmp_b1[l], cmp_w2[l], mla_q_norm_g[l], mla_w_q_up[l],
                   mla_kv_norm_g[l], mla_w_kv_up[l], w_branch_nsa[l], w_branch_mla[l], w_out[l],
                   ffn_pre_g[l], ffn_post_g[l], peer_w_q[l], peer_sub_keys[l], peer_u[l], peer_v[l])
    return x
```

```python
import functools

import numpy as np
import jax
import jax.numpy as jnp
from jax import lax
from jax.experimental import pallas as pl
from jax.experimental.pallas import tpu as pltpu

F32 = jnp.float32
BF16 = jnp.bfloat16
I32 = jnp.int32

LANES = 128
SUBLANES = 8

NSA_HEADS = 16
NSA_GROUPS = 4
NSA_R = NSA_HEADS // NSA_GROUPS
NSA_HEAD_DIM = 64
NSA_ROPE_DIM = NSA_HEAD_DIM // 4
CMP_BLOCK = 32
CMP_STRIDE = 16
CMP_HIDDEN = 256
SEL_BLOCK = 64
SEL_SHIFT = 6
SEL_TOPK = 16
WINDOW = 512
MLA_HEADS = 16
MLA_NOPE_DIM = 64
MLA_ROPE_DIM = 32
MLA_V_DIM = 64
PEER_HEADS = 8
PEER_N_KEYS = 128
PEER_TOPK = 16
ROPE_THETA = 500000.0
NORM_EPS = 1e-6
NEG = -1e30
FORCE_BONUS = 1e4
LOG2E = 1.4426950408889634
MLA_ROPE_OFF = MLA_NOPE_DIM
PROJ_TM = 1024


def _cparams(sem, vmem_mb=None):
    kw = dict(dimension_semantics=sem)
    if vmem_mb is not None:
        kw["vmem_limit_bytes"] = vmem_mb * 1024 * 1024
    return pltpu.CompilerParams(**kw)


def _rms(x, g):
    return x * lax.rsqrt(jnp.mean(x * x, axis=-1, keepdims=True) + NORM_EPS) * g


def _gelu(x):
    return jax.nn.gelu(x, approximate=True)


def _adaln_kernel(c_ref, w_ref, b_ref, o_ref):
    c = c_ref[...]
    s = (c * jax.nn.sigmoid(c)).astype(BF16)
    o_ref[...] = jnp.dot(s, w_ref[...].astype(BF16), preferred_element_type=F32) + b_ref[...]


def _adaln(c, w, b, tn=1024):
    bsz, d = c.shape
    n = w.shape[1]
    cp = jnp.pad(c, ((0, SUBLANES - bsz), (0, 0)))
    out = pl.pallas_call(
        _adaln_kernel,
        grid=(n // tn,),
        in_specs=[pl.BlockSpec((SUBLANES, d), lambda j: (0, 0)),
                  pl.BlockSpec((d, tn), lambda j: (0, j)),
                  pl.BlockSpec((1, tn), lambda j: (0, j))],
        out_specs=pl.BlockSpec((SUBLANES, tn), lambda j: (0, j)),
        out_shape=jax.ShapeDtypeStruct((SUBLANES, n), F32),
        compiler_params=_cparams(("arbitrary",), 40),
        name="adaln",
    )(cp, w, b.reshape(1, n))
    return out[:bsz]


def _rope_table_kernel(pos_ref, inv_ref, sgn_ref, o_ref):
    pos = pos_ref[...]
    for k in range(2):
        ang = pos * inv_ref[k:k + 1, :]
        o_ref[:, (2 * k) * LANES:(2 * k + 1) * LANES] = jnp.cos(ang)
        o_ref[:, (2 * k + 1) * LANES:(2 * k + 2) * LANES] = jnp.sin(ang) * sgn_ref[k:k + 1, :]


def _rope_tables(positions, tm=512):
    n = positions.size
    half_a = NSA_ROPE_DIM // 2
    half_b = MLA_ROPE_DIM // 2
    inv_a = ROPE_THETA ** (-jnp.arange(half_a, dtype=F32) * (2.0 / NSA_ROPE_DIM))
    inv_b = ROPE_THETA ** (-jnp.arange(half_b, dtype=F32) * (2.0 / MLA_ROPE_DIM))
    inv = jnp.zeros((2, LANES), F32)
    inv = inv.at[0, 0:half_a].set(inv_a).at[0, half_a:2 * half_a].set(inv_a)
    o = MLA_ROPE_OFF
    inv = inv.at[1, o:o + half_b].set(inv_b).at[1, o + half_b:o + 2 * half_b].set(inv_b)
    sgn = np.zeros((2, LANES), np.float32)
    sgn[0, 0:half_a] = -1.0
    sgn[0, half_a:2 * half_a] = 1.0
    sgn[1, o:o + half_b] = -1.0
    sgn[1, o + half_b:o + 2 * half_b] = 1.0
    posf = jnp.broadcast_to(positions.reshape(n, 1).astype(F32), (n, LANES))
    return pl.pallas_call(
        _rope_table_kernel,
        grid=(n // tm,),
        in_specs=[pl.BlockSpec((tm, LANES), lambda i: (i, 0)),
                  pl.BlockSpec((2, LANES), lambda i: (0, 0)),
                  pl.BlockSpec((2, LANES), lambda i: (0, 0))],
        out_specs=pl.BlockSpec((tm, 4 * LANES), lambda i: (i, 0)),
        out_shape=jax.ShapeDtypeStruct((n, 4 * LANES), F32),
        compiler_params=_cparams(("arbitrary",)),
        name="rope_tables",
    )(posf, inv, jnp.asarray(sgn))


def _rope_slot(a, cos, sin, half, off):
    lane = lax.broadcasted_iota(I32, (1, LANES), 1)
    first = (lane - off) < half
    partner = jnp.where(first, pltpu.roll(a, LANES - half, 1), pltpu.roll(a, half, 1))
    return a * cos + partner * sin


def _normmod_kernel(x_ref, g_ref, sc_ref, sh_ref, o_ref):
    y = _rms(x_ref[...], g_ref[...])
    o_ref[...] = (y * (1.0 + sc_ref[0]) + sh_ref[0]).astype(o_ref.dtype)


def _normmod(x, g, sc, sh, seq, tm=512):
    n, d = x.shape
    bsz = sc.shape[0]
    bspec = pl.BlockSpec((1, 1, d), lambda i: ((i * tm) // seq, 0, 0))
    return pl.pallas_call(
        _normmod_kernel,
        grid=(n // tm,),
        in_specs=[pl.BlockSpec((tm, d), lambda i: (i, 0)),
                  pl.BlockSpec((1, d), lambda i: (0, 0)), bspec, bspec],
        out_specs=pl.BlockSpec((tm, d), lambda i: (i, 0)),
        out_shape=jax.ShapeDtypeStruct((n, d), BF16),
        compiler_params=_cparams(("arbitrary",)),
        name="normmod",
    )(x, g.reshape(1, d), sc.reshape(bsz, 1, d), sh.reshape(bsz, 1, d))


def _out_proj_kernel(m_ref, w_ref, x_ref, pg_ref, ga_ref, fg_ref, sc_ref, sh_ref, x1_ref, h2_ref):
    y = jnp.dot(m_ref[...], w_ref[...], preferred_element_type=F32)
    x1 = x_ref[...] + ga_ref[0] * _rms(y, pg_ref[...])
    x1_ref[...] = x1
    h2_ref[...] = (_rms(x1, fg_ref[...]) * (1.0 + sc_ref[0]) + sh_ref[0]).astype(h2_ref.dtype)


def _out_proj(merged, w, x, post_g, gate, pre_g, sc, sh, seq, tm=512):
    n, d = x.shape
    bsz = gate.shape[0]
    row = lambda width: pl.BlockSpec((tm, width), lambda i: (i, 0))
    vec = pl.BlockSpec((1, d), lambda i: (0, 0))
    bvec = pl.BlockSpec((1, 1, d), lambda i: ((i * tm) // seq, 0, 0))
    return pl.pallas_call(
        _out_proj_kernel,
        grid=(n // tm,),
        in_specs=[row(merged.shape[1]), pl.BlockSpec(w.shape, lambda i: (0, 0)), row(d), vec, bvec,
                  vec, bvec, bvec],
        out_specs=[row(d), row(d)],
        out_shape=[jax.ShapeDtypeStruct((n, d), F32), jax.ShapeDtypeStruct((n, d), BF16)],
        compiler_params=_cparams(("arbitrary",), 56),
        name="out_proj",
    )(merged, w, x, post_g.reshape(1, d), gate.reshape(bsz, 1, d), pre_g.reshape(1, d),
      sc.reshape(bsz, 1, d), sh.reshape(bsz, 1, d))


def _mm(x, w, epi, *, tm, tn, out_dtype, name, extras=(), pro=None, x_spec=None, vmem_mb=None):
    m = x.shape[0]
    k, nc = w.shape
    if x_spec is None:
        x_spec = pl.BlockSpec((tm, k), lambda i, j: (i, 0))
    in_specs = [x_spec, pl.BlockSpec((k, tn), lambda i, j: (0, j))]
    args = [x, w]
    for arr, spec in extras:
        in_specs.append(spec)
        args.append(arr)

    def kern(x_ref, w_ref, *rest):
        o_ref = rest[-1]
        ex = rest[:-1]
        xv = x_ref[...]
        if pro is not None:
            xv = pro(xv, *ex)
        acc = jnp.dot(xv, w_ref[...], preferred_element_type=F32)
        epi(acc, o_ref, *ex)

    return pl.pallas_call(
        kern,
        grid=(m // tm, nc // tn),
        in_specs=in_specs,
        out_specs=pl.BlockSpec((tm, tn), lambda i, j: (i, j)),
        out_shape=jax.ShapeDtypeStruct((m, nc), out_dtype),
        compiler_params=_cparams(("arbitrary", "arbitrary"), vmem_mb),
        name=name,
    )(*args)


def _epi_plain(acc, o_ref, *ex):
    o_ref[...] = acc.astype(o_ref.dtype)


def _make_epi_rope_a(scale, seq=None, tm=None, onehot=False):
    half = NSA_ROPE_DIM // 2

    def epi(acc, o_ref, tab_ref):
        cos = tab_ref[:, 0:LANES]
        sin = tab_ref[:, LANES:2 * LANES]
        if onehot:
            base = lax.rem(pl.program_id(0) * tm, seq)
            t = base + lax.broadcasted_iota(I32, (acc.shape[0], LANES), 0)
            lane = lax.broadcasted_iota(I32, (acc.shape[0], LANES), 1)
            hot = jnp.where(lane - SEL_BLOCK == jnp.right_shift(t, SEL_SHIFT), 1.0, 0.0)
        for s in range(acc.shape[1] // LANES):
            r = _rope_slot(acc[:, s * LANES:(s + 1) * LANES], cos, sin, half, 0)
            if scale != 1.0:
                r = r * scale
            if onehot:
                r = r + hot
            o_ref[:, s * LANES:(s + 1) * LANES] = r.astype(o_ref.dtype)
    return epi


def _epi_rope_b_scaled(acc, o_ref, g_ref, tab_ref):
    cos = tab_ref[:, 2 * LANES:3 * LANES]
    sin = tab_ref[:, 3 * LANES:4 * LANES]
    scale = (MLA_NOPE_DIM + MLA_ROPE_DIM) ** -0.5 * LOG2E
    for s in range(acc.shape[1] // LANES):
        r = _rope_slot(acc[:, s * LANES:(s + 1) * LANES], cos, sin, MLA_ROPE_DIM // 2, MLA_ROPE_OFF)
        o_ref[:, s * LANES:(s + 1) * LANES] = (r * scale).astype(o_ref.dtype)


def _epi_gate_kr(acc, o_ref, tab_ref):
    o_ref[:, 0:LANES] = jax.nn.sigmoid(acc[:, 0:LANES]).astype(o_ref.dtype)
    cos = tab_ref[:, 2 * LANES:3 * LANES]
    sin = tab_ref[:, 3 * LANES:4 * LANES]
    r = _rope_slot(acc[:, LANES:2 * LANES], cos, sin, MLA_ROPE_DIM // 2, MLA_ROPE_OFF)
    o_ref[:, LANES:2 * LANES] = r.astype(o_ref.dtype)


def _pro_rms(xv, g_ref, *ex):
    return _rms(xv, g_ref[...]).astype(BF16)


def _epi_add_kr(acc, o_ref, g_ref, kr_ref):
    kr = kr_ref[...].astype(F32)
    for s in range(acc.shape[1] // LANES):
        o_ref[:, s * LANES:(s + 1) * LANES] = (acc[:, s * LANES:(s + 1) * LANES] + kr).astype(o_ref.dtype)


def _epi_plain_g(acc, o_ref, g_ref):
    o_ref[...] = acc.astype(o_ref.dtype)


def _compress_kernel(k16_ref, v16_ref, w1_ref, pe_ref, b1_ref, w2_ref, kc_ref, vc_ref):
    half = w1_ref.shape[1] // 2
    for which, (src, dst) in enumerate(((k16_ref, kc_ref), (v16_ref, vc_ref))):
        xb = src[...]
        a = jnp.dot(xb, w1_ref[which, 0:half, :], preferred_element_type=F32)
        b = jnp.dot(xb, w1_ref[which, half:2 * half, :], preferred_element_type=F32)
        c = jnp.dot(pe_ref[which], w1_ref[which], preferred_element_type=F32)[0:1, :] + b1_ref[which]
        rows = a.shape[0]
        pre = a + pltpu.roll(b, rows - 1, 0) + c
        hid = _gelu(pre).astype(BF16)
        dst[...] = jnp.dot(hid, w2_ref[which], preferred_element_type=F32).astype(dst.dtype)


def _compress(k16, v16, w1, pe, b1, w2p):
    bg, rows, feat = k16.shape
    blk = pl.BlockSpec((None, rows, feat), lambda i: (i, 0, 0))
    full = lambda a: pl.BlockSpec(a.shape, lambda i: (0,) * a.ndim)
    out = pl.BlockSpec((None, rows, LANES), lambda i: (i, 0, 0))
    return pl.pallas_call(
        _compress_kernel,
        grid=(bg,),
        in_specs=[blk, blk, full(w1), full(pe), full(b1), full(w2p)],
        out_specs=[out, out],
        out_shape=[jax.ShapeDtypeStruct((bg, rows, LANES), BF16)] * 2,
        compiler_params=_cparams(("arbitrary",)),
        name="nsa_compress",
    )(k16, v16, w1, pe, b1, w2p)


def _row_bcast(col, width):
    if width % LANES:
        return jnp.broadcast_to(col, (col.shape[0], width))
    tile = jnp.broadcast_to(col, (col.shape[0], LANES))
    return jnp.concatenate([tile] * (width // LANES), axis=1)


def _cmp_attn_kernel(q_ref, kc_ref, vc_ref, ovt_ref, o_ref, bias_ref, *, tq, n_sel):
    qi = pl.program_id(2)
    ncmp = kc_ref.shape[0]
    t_row = qi * tq + lax.broadcasted_iota(I32, (tq, ncmp), 0)
    n_col = lax.broadcasted_iota(I32, (tq, ncmp), 1)
    cmask = (n_col * CMP_STRIDE + (CMP_BLOCK - 1)) <= t_row
    kc = kc_ref[...]
    vc = vc_ref[...]
    imp_t = jnp.zeros((n_sel, tq), F32)
    outs = []
    for r in range(NSA_R):
        q = q_ref[:, r * LANES:(r + 1) * LANES]
        s = lax.dot_general(q, kc, (((1,), (1,)), ((), ())), preferred_element_type=F32)
        s = jnp.where(cmask, s, NEG)
        e = jnp.exp2(s - _row_bcast(jnp.max(s, axis=1, keepdims=True), ncmp))
        p = e / _row_bcast(jnp.sum(e, axis=1, keepdims=True), ncmp)
        p = jnp.where(cmask, p, 0.0).astype(BF16)
        outs.append(jnp.dot(p, vc, preferred_element_type=F32))
        imp_t = imp_t + lax.dot_general(ovt_ref[...], p, (((1,), (1,)), ((), ())),
                                        preferred_element_type=F32)
    _store_heads_compact(o_ref, outs)
    blk = lax.broadcasted_iota(I32, (n_sel, tq), 0)
    t = qi * tq + lax.broadcasted_iota(I32, (n_sel, tq), 1)
    tb = jnp.right_shift(t, SEL_SHIFT)
    forced = (blk == 0) | (blk == tb) | (blk == tb - 1)
    valid = blk * SEL_BLOCK <= t
    x = jnp.where(valid, imp_t + jnp.where(forced, FORCE_BONUS, 0.0), NEG)
    sel = jnp.zeros((n_sel, tq), F32)
    for _ in range(min(SEL_TOPK, n_sel)):
        m = jnp.max(x, axis=0, keepdims=True)
        idx = jnp.min(jnp.where(x == m, blk, n_sel), axis=0, keepdims=True)
        hit = blk == idx
        sel = jnp.where(hit, 1.0, sel)
        x = jnp.where(hit, -jnp.inf, x)
    bias_t = jnp.where(sel > 0.5, 0.0, NEG)
    parts = [jnp.zeros((SEL_BLOCK, tq), F32), bias_t]
    if LANES - SEL_BLOCK - n_sel > 0:
        parts.append(jnp.zeros((LANES - SEL_BLOCK - n_sel, tq), F32))
    full_t = jnp.concatenate(parts, axis=0)
    bias_ref[...] = full_t.T.astype(bias_ref.dtype)


def _cmp_attn(qn, kc, vc, ovt, bsz, seq, tq=256):
    n = qn.shape[0]
    nq = seq // tq
    n_sel = seq // SEL_BLOCK
    ncmp = kc.shape[1]
    qspec = pl.BlockSpec((tq, NSA_R * LANES), lambda b, g, qi: (b * nq + qi, g))
    kspec = pl.BlockSpec((None, ncmp, LANES), lambda b, g, qi: (b * NSA_GROUPS + g, 0, 0))
    return pl.pallas_call(
        functools.partial(_cmp_attn_kernel, tq=tq, n_sel=n_sel),
        grid=(bsz, NSA_GROUPS, nq),
        in_specs=[qspec, kspec, kspec, pl.BlockSpec(ovt.shape, lambda b, g, qi: (0, 0))],
        out_specs=[pl.BlockSpec((tq, NSA_R * NSA_HEAD_DIM), lambda b, g, qi: (b * nq + qi, g)),
                   pl.BlockSpec((tq, LANES), lambda b, g, qi: (b * nq + qi, g))],
        out_shape=[jax.ShapeDtypeStruct((n, NSA_HEADS * NSA_HEAD_DIM), BF16),
                   jax.ShapeDtypeStruct((n, NSA_GROUPS * LANES), BF16)],
        compiler_params=_cparams(("arbitrary",) * 3),
        name="nsa_cmp_attn",
    )(qn, kc, vc, ovt)


def _store_heads_compact(o_ref, heads):
    hd = NSA_HEAD_DIM
    for p in range(len(heads) // 2):
        pair = jnp.concatenate([heads[2 * p][:, 0:hd], heads[2 * p + 1][:, 0:hd]], axis=1)
        o_ref[:, p * LANES:(p + 1) * LANES] = pair.astype(o_ref.dtype)


def _flash_kernel(*refs, mode, tq, tk, nr, nh, window):
    if mode == "sel":
        q_ref, k_ref, v_ref, bias_ref, o_ref, qs, m_s, l_s, acc_s = refs
    else:
        q_ref, k_ref, v_ref, o_ref, qs, m_s, l_s, acc_s = refs
    qi = pl.program_id(2)
    kk = pl.program_id(3)
    nk = pl.num_programs(3)
    rows = nr * tq

    @pl.when(kk == 0)
    def _init():
        for r in range(nh * nr):
            qr = q_ref[:, r * LANES:(r + 1) * LANES]
            if mode == "sel":
                qr = qr + bias_ref[:, (r // nr) * LANES:(r // nr + 1) * LANES]
            qs[r * tq:(r + 1) * tq, :] = qr
        m_s[...] = jnp.full(m_s.shape, -jnp.inf, F32)
        l_s[...] = jnp.zeros(l_s.shape, F32)
        acc_s[...] = jnp.zeros(acc_s.shape, F32)

    if mode == "win":
        kidx = qi * (tq // tk) - window // tk + kk
        needed = kidx >= 0
        full_vis = (kidx * tk + tk - 1 <= qi * tq) & (qi * tq + tq - 1 - kidx * tk < window)
    else:
        kidx = kk
        needed = kk * tk <= qi * tq + tq - 1
        full_vis = kk * tk + tk - 1 <= qi * tq

    nt = tk // LANES

    def update(masked):
        if masked:
            rel = (lax.broadcasted_iota(I32, (rows, tk), 0) & (tq - 1)) - lax.broadcasted_iota(I32, (rows, tk), 1)
            off = kidx * tk - qi * tq
            vis = rel >= off
            if mode == "win":
                vis = vis & (rel < off + window)
        for hh in range(nh):
            rs = slice(hh * rows, (hh + 1) * rows)
            cs = slice(hh * LANES, (hh + 1) * LANES)
            s = lax.dot_general(qs[rs, :], k_ref[:, cs], (((1,), (1,)), ((), ())),
                                preferred_element_type=F32)
            if masked:
                s = jnp.where(vis, s, NEG)
            m_prev = m_s[rs, :]
            m_new = jnp.maximum(m_prev, jnp.max(s, axis=1, keepdims=True))
            p = jnp.exp2(s - jnp.concatenate([m_new] * nt, axis=1))
            alpha = jnp.exp2(m_prev - m_new)
            psum = p[:, 0:LANES]
            for c in range(1, nt):
                psum = psum + p[:, c * LANES:(c + 1) * LANES]
            l_s[rs, :] = alpha * l_s[rs, :] + psum
            acc_s[rs, :] = alpha * acc_s[rs, :] + jnp.dot(p.astype(BF16), v_ref[:, cs],
                                                          preferred_element_type=F32)
            m_s[rs, :] = m_new

    @pl.when(needed & full_vis)
    def _full():
        update(False)

    @pl.when(needed & jnp.logical_not(full_vis))
    def _edge():
        update(True)

    @pl.when(kk == nk - 1)
    def _fin():
        out = acc_s[...] / jnp.sum(l_s[...], axis=1, keepdims=True)
        _store_heads_compact(o_ref, [out[r * tq:(r + 1) * tq, :] for r in range(nh * nr)])


def _flash(q, k, v, *, mode, bsz, seq, nslots, nr, tq, tk, nh=1, k_off=0, v_off=0, bias=None,
           name):
    assert tq & (tq - 1) == 0
    n = q.shape[0]
    nq = seq // tq
    nkb = seq // tk
    if mode == "win":
        assert tq % tk == 0 and WINDOW % tk == 0
        steps = WINDOW // tk + tq // tk

        def krow(b, qi, kk):
            return b * nkb + jnp.maximum(qi * (tq // tk) - WINDOW // tk + kk, 0)
    else:
        steps = nkb

        def krow(b, qi, kk):
            return b * nkb + jnp.minimum(kk, (qi * tq + tq - 1) // tk)

    qspec = pl.BlockSpec((tq, nh * nr * LANES), lambda b, g, qi, kk: (b * nq + qi, g))
    in_specs = [qspec,
                pl.BlockSpec((tk, nh * LANES), lambda b, g, qi, kk: (krow(b, qi, kk), k_off // nh + g)),
                pl.BlockSpec((tk, nh * LANES), lambda b, g, qi, kk: (krow(b, qi, kk), v_off // nh + g))]
    args = [q, k, v]
    if mode == "sel":
        in_specs.append(pl.BlockSpec((tq, nh * LANES), lambda b, g, qi, kk: (b * nq + qi, g)))
        args.append(bias)
    rows = nh * nr * tq
    ow = nh * nr * NSA_HEAD_DIM
    return pl.pallas_call(
        functools.partial(_flash_kernel, mode=mode, tq=tq, tk=tk, nr=nr, nh=nh, window=WINDOW),
        grid=(bsz, nslots, nq, steps),
        in_specs=in_specs,
        out_specs=pl.BlockSpec((tq, ow), lambda b, g, qi, kk: (b * nq + qi, g)),
        out_shape=jax.ShapeDtypeStruct((n, nslots * ow), BF16),
        scratch_shapes=[pltpu.VMEM((rows, LANES), BF16), pltpu.VMEM((rows, LANES), F32),
                        pltpu.VMEM((rows, LANES), F32), pltpu.VMEM((rows, LANES), F32)],
        compiler_params=_cparams(("arbitrary",) * 4, 48),
        name=name,
    )(*args)


def _combine_kernel(oc_ref, os_ref, ow_ref, g_ref, e_ref, o_ref):
    g = g_ref[...]
    acc = None
    for br, src in enumerate((oc_ref, os_ref, ow_ref)):
        ge = jnp.dot(g, e_ref[br], preferred_element_type=F32)
        term = ge * src[...].astype(F32)
        acc = term if acc is None else acc + term
    o_ref[...] = acc.astype(o_ref.dtype)


def _combine(o_cmp, o_sel, o_win, gates, gates_col, tm=512):
    n, w = o_cmp.shape
    e = np.zeros((3, LANES, w), np.float32)
    for h in range(NSA_HEADS):
        for br in range(3):
            e[br, h * 3 + br, h * NSA_HEAD_DIM:(h + 1) * NSA_HEAD_DIM] = 1.0
    row = pl.BlockSpec((tm, w), lambda i: (i, 0))
    return pl.pallas_call(
        _combine_kernel,
        grid=(n // tm,),
        in_specs=[row, row, row, pl.BlockSpec((tm, LANES), lambda i: (i, gates_col)),
                  pl.BlockSpec(e.shape, lambda i: (0, 0, 0))],
        out_specs=row,
        out_shape=jax.ShapeDtypeStruct((n, w), BF16),
        compiler_params=_cparams(("arbitrary",)),
        name="nsa_combine",
    )(o_cmp, o_sel, o_win, gates, jnp.asarray(e, BF16))


def _merge_kernel(h_ref, wg0_ref, wg1_ref, a_ref, wa_ref, b_ref, wb_ref, o_ref):
    h = h_ref[...]
    g0 = jax.nn.sigmoid(jnp.dot(h, wg0_ref[...], preferred_element_type=F32))
    g1 = jax.nn.sigmoid(jnp.dot(h, wg1_ref[...], preferred_element_type=F32))
    ya = jnp.dot(a_ref[...], wa_ref[...], preferred_element_type=F32)
    yb = jnp.dot(b_ref[...], wb_ref[...], preferred_element_type=F32)
    o_ref[...] = (g0 * ya + g1 * yb).astype(o_ref.dtype)


def _merge(h, w_gm, o_nsa, w_nsa, o_mla, w_mla, tm, tn=512):
    n, k = o_nsa.shape
    kh = h.shape[1]
    d = w_nsa.shape[1]
    nj = d // tn
    row = pl.BlockSpec((tm, k), lambda i, j: (i, 0))
    wsp = pl.BlockSpec((k, tn), lambda i, j: (0, j))
    return pl.pallas_call(
        _merge_kernel,
        grid=(n // tm, nj),
        in_specs=[pl.BlockSpec((tm, kh), lambda i, j: (i, 0)),
                  pl.BlockSpec((kh, tn), lambda i, j: (0, j)),
                  pl.BlockSpec((kh, tn), lambda i, j: (0, nj + j)),
                  row, wsp, row, wsp],
        out_specs=pl.BlockSpec((tm, tn), lambda i, j: (i, j)),
        out_shape=jax.ShapeDtypeStruct((n, d), BF16),
        compiler_params=_cparams(("arbitrary", "arbitrary"), 48),
        name="branch_merge",
    )(h, w_gm, w_gm, o_nsa, w_nsa, o_mla, w_mla)


def _merge_desc(xs):
    xs = list(xs)
    n = len(xs)
    stride = n // 2
    while stride >= 1:
        for i in range(n):
            p = i ^ stride
            if p > i:
                xs[i], xs[p] = jnp.maximum(xs[i], xs[p]), jnp.minimum(xs[i], xs[p])
        stride //= 2
    return xs


def _top16_sorted(xs):
    xs = list(xs)
    n = len(xs)
    size = 2
    while size <= n:
        stride = size // 2
        while stride >= 1:
            for i in range(n):
                p = i ^ stride
                if p > i:
                    hi, lo = jnp.maximum(xs[i], xs[p]), jnp.minimum(xs[i], xs[p])
                    xs[i], xs[p] = (hi, lo) if (i & size) == 0 else (lo, hi)
            stride //= 2
        size *= 2
    shift = SUBLANES // 2
    while shift >= 1:
        ys = [pltpu.roll(x, shift, 0) for x in xs]
        xs = _merge_desc([jnp.maximum(xs[k], ys[n - 1 - k]) for k in range(n)])
        shift //= 2
    return xs


def _prefix_len(pred, vs):
    p8 = pred(vs[7])
    p4 = pred(jnp.where(p8, vs[11], vs[3]))
    p2 = pred(jnp.where(p8, jnp.where(p4, vs[13], vs[9]), jnp.where(p4, vs[5], vs[1])))
    q = [jnp.where(p2, vs[4 * i + 2], vs[4 * i]) for i in range(4)]
    p1 = pred(jnp.where(p8, jnp.where(p4, q[3], q[2]), jnp.where(p4, q[1], q[0])))
    n = (jnp.where(p8, 8.0, 0.0) + jnp.where(p4, 4.0, 0.0)) + (jnp.where(p2, 2.0, 0.0) + jnp.where(p1, 1.0, 0.0))
    return jnp.where(pred(vs[15]), 16.0, n)


def _rows8(x):
    return [x[k * SUBLANES:(k + 1) * SUBLANES, :] for k in range(x.shape[0] // SUBLANES)]


def _peer_route_kernel(q_ref, sk_ref, tb_ref, tf_ref):
    nk = PEER_N_KEYS
    sec = PEER_HEADS * nk
    tr = q_ref.shape[0]
    sub = lax.broadcasted_iota(I32, (SUBLANES, tr), 0)
    ninf = jnp.full((SUBLANES, tr), -jnp.inf, F32)

    def spread(vs):
        out = vs[SUBLANES - 1]
        for j in range(SUBLANES - 2, -1, -1):
            out = jnp.where(sub == j, vs[j], out)
        return out

    for h in range(PEER_HEADS):
        s = []
        for p in range(2):
            c = (2 * h + p) * nk
            s.append(lax.dot_general(sk_ref[2 * h + p], q_ref[:, c:c + nk], (((1,), (1,)), ((), ())),
                                     preferred_element_type=F32))
        x1, x2 = _rows8(s[0]), _rows8(s[1])
        v1, v2 = _top16_sorted(x1), _top16_sorted(x2)
        v2lo, v2hi = spread(v2[:SUBLANES]), spread(v2[SUBLANES:])
        cands = [v1[0] + v2lo, v1[0] + v2hi]
        for i in range(1, SUBLANES):
            c = v1[i] + v2lo
            cnt = PEER_TOPK // (i + 1)
            cands.append(jnp.where(sub < cnt, c, -jnp.inf) if cnt < SUBLANES else c)
        cands.append(spread(v1[SUBLANES:]) + v2[0])
        top = _top16_sorted(cands + [ninf] * (PEER_TOPK - len(cands)))
        tau = top[PEER_TOPK - 1]
        z = None
        for k in range(PEER_TOPK):
            ek = jnp.exp(top[k] - top[0])
            z = ek if z is None else z + ek
        rank2 = jnp.concatenate([_prefix_len(lambda v, x=x: v > x, v2) for x in x2], axis=0)
        count = jnp.concatenate([_prefix_len(lambda v, x=x: (x + v) >= tau, v2) for x in x1], axis=0)
        tb_ref[h * nk:(h + 1) * nk, :] = rank2.astype(tb_ref.dtype)
        tb_ref[sec + h * nk:sec + (h + 1) * nk, :] = jnp.exp(s[1] - v2[0][0:1, :]).astype(tb_ref.dtype)
        apb = tf_ref.shape[1] // (2 * PEER_HEADS)
        for k, vals in enumerate((count, jnp.exp(s[0] - v1[0][0:1, :]) / z[0:1, :])):
            r0 = (k * PEER_HEADS + h) * apb
            tf_ref[:, r0:r0 + apb, :] = vals.reshape(nk // apb, apb, tr)


def _peer_route(qry, sk, apb, tr=256):
    n, w = qry.shape
    nj = PEER_N_KEYS // apb
    rows = 2 * PEER_HEADS * apb
    return pl.pallas_call(
        _peer_route_kernel,
        grid=(n // tr,),
        in_specs=[pl.BlockSpec((tr, w), lambda i: (i, 0)),
                  pl.BlockSpec(sk.shape, lambda i: (0, 0, 0))],
        out_specs=[pl.BlockSpec((w, tr), lambda i: (0, i)),
                   pl.BlockSpec((nj, rows, tr), lambda i: (0, 0, i))],
        out_shape=[jax.ShapeDtypeStruct((w, n), BF16), jax.ShapeDtypeStruct((nj, rows, n), F32)],
        compiler_params=_cparams(("arbitrary",), 40),
        name="peer_route",
    )(qry, sk)


PEER_CHUNK = 32
PEER_TILE = 1024


def _peer_dense_kernel(h_ref, u_ref, vt_ref, tb_ref, tf_ref, x_ref, pg_ref, gf_ref, o_ref, acc_s, pt_s,
                       *, te):
    j = pl.program_id(1)
    nk = PEER_N_KEYS
    sec = PEER_HEADS * nk
    apb = te // nk
    gdt = pt_s.dtype

    @pl.when(j == 0)
    def _init():
        acc_s[...] = jnp.zeros(acc_s.shape, F32)

    def row_tile(row):
        r = tf_ref[row:row + 1, :]
        return jnp.broadcast_to(r.astype(gdt), (PEER_CHUNK, r.shape[1]))

    zt = lax.dot_general(u_ref[...], h_ref[...], (((1,), (1,)), ((), ())), preferred_element_type=F32)
    for al in range(apb):
        cnt = [row_tile(h * apb + al) for h in range(PEER_HEADS)]
        e1 = [row_tile((PEER_HEADS + h) * apb + al) for h in range(PEER_HEADS)]
        for c in range(nk // PEER_CHUNK):
            lo = c * PEER_CHUNK
            g = None
            for h in range(PEER_HEADS):
                rank2 = tb_ref[h * nk + lo:h * nk + lo + PEER_CHUNK, :]
                e2 = tb_ref[sec + h * nk + lo:sec + h * nk + lo + PEER_CHUNK, :]
                term = jnp.where(rank2 < cnt[h], e1[h] * e2, jnp.zeros((), gdt))
                g = term if g is None else g + term
            r0 = al * nk + lo
            pt_s[r0:r0 + PEER_CHUNK, :] = g * _gelu(zt[r0:r0 + PEER_CHUNK, :]).astype(gdt)
    acc_s[...] += jnp.dot(vt_ref[...], pt_s[...], preferred_element_type=F32)

    @pl.when(j == pl.num_programs(1) - 1)
    def _fin():
        o_ref[...] = x_ref[...] + gf_ref[0] * _rms(acc_s[...].T, pg_ref[...])


def _peer_dense(h2, u_bf, vt_bf, tb, tf, x1, post_g, gate, seq, te, tm=512):
    n, d = h2.shape
    ne = u_bf.shape[0]
    w = tb.shape[0]
    bsz = gate.shape[0]
    row = pl.BlockSpec((tm, d), lambda i, j: (i, 0))
    return pl.pallas_call(
        functools.partial(_peer_dense_kernel, te=te),
        grid=(n // tm, ne // te),
        in_specs=[row,
                  pl.BlockSpec((te, d), lambda i, j: (j, 0)),
                  pl.BlockSpec((d, te), lambda i, j: (0, j)),
                  pl.BlockSpec((w, tm), lambda i, j: (0, i)),
                  pl.BlockSpec((None, tf.shape[1], tm), lambda i, j: (j, 0, i)),
                  row,
                  pl.BlockSpec((1, d), lambda i, j: (0, 0)),
                  pl.BlockSpec((1, 1, d), lambda i, j: ((i * tm) // seq, 0, 0))],
        out_specs=row,
        out_shape=jax.ShapeDtypeStruct((n, d), F32),
        scratch_shapes=[pltpu.VMEM((d, tm), F32), pltpu.VMEM((te, tm), BF16)],
        compiler_params=_cparams(("arbitrary", "arbitrary"), 56),
        name="peer_dense",
    )(h2, u_bf, vt_bf, tb, tf, x1, post_g.reshape(1, d), gate.reshape(bsz, 1, d))


def _slots(w, n, width, off=0):
    k = w.shape[0]
    w = w.reshape(k, n, width)
    w = jnp.pad(w, ((0, 0), (0, 0), (off, LANES - width - off)))
    return w.reshape(k, n * LANES)


def _overlap_t(seq):
    n_cmp_rows = seq // CMP_STRIDE
    n_sel = seq // SEL_BLOCK
    cs = np.arange(n_cmp_rows) * CMP_STRIDE
    ce = cs + CMP_BLOCK - 1
    ss = np.arange(n_sel) * SEL_BLOCK
    ov = (cs[None, :] < ss[:, None] + SEL_BLOCK) & (ce[None, :] >= ss[:, None])
    ov[:, n_cmp_rows - 1] = False
    return jnp.asarray(ov.astype(np.float32), BF16)


def _token_mixer(h, positions, bsz, seq, w_in, cmp_pe, cmp_w1, cmp_b1, cmp_w2, mla_q_norm_g,
                 mla_w_q_up, mla_kv_norm_g, mla_w_kv_up, w_branch_nsa, w_branch_mla):
    d = h.shape[1]
    g_, hd = NSA_GROUPS, NSA_HEAD_DIM
    tab = _rope_tables(positions)
    tm = min(PROJ_TM, seq)
    tab_spec = pl.BlockSpec((tm, 4 * LANES), lambda i, j: (i, 0))

    wb = w_in.astype(BF16)
    o1 = NSA_HEADS * hd
    kv = wb[:, o1:o1 + 3 * 2 * g_ * hd].reshape(d, 3, 2, g_ * hd)
    o2 = o1 + 3 * 2 * g_ * hd
    o3 = o2 + 3 * NSA_HEADS
    mq = mla_w_q_up.shape[0]
    mkv = mla_w_kv_up.shape[0]
    o4 = o3 + mq
    o5 = o4 + mkv
    o6 = o5 + MLA_ROPE_DIM
    w_q = _slots(wb[:, :o1], NSA_HEADS, hd)
    w_kk = jnp.concatenate([_slots(kv[:, 0, 0], g_, hd), _slots(kv[:, 2, 0], g_, hd)], axis=1)
    w_ks = _slots(kv[:, 1, 0], g_, hd)
    w_vv = jnp.concatenate([_slots(kv[:, 1, 1], g_, hd), _slots(kv[:, 2, 1], g_, hd),
                            kv[:, 0, 1], jnp.zeros((d, g_ * hd), BF16)], axis=1)
    w_c = wb[:, o3:o5]
    w_gk = jnp.concatenate([jnp.pad(wb[:, o2:o3], ((0, 0), (0, LANES - 3 * NSA_HEADS))),
                            _slots(wb[:, o5:o6], 1, MLA_ROPE_DIM, MLA_ROPE_OFF)], axis=1)
    w_gm = wb[:, o6:]

    qn = _mm(h, w_q, _make_epi_rope_a(hd ** -0.5 * LOG2E), tm=tm, tn=512, out_dtype=BF16, name="proj_q",
             extras=[(tab, tab_spec)])
    kk = _mm(h, w_kk, _make_epi_rope_a(1.0), tm=tm, tn=512, out_dtype=BF16, name="proj_k",
             extras=[(tab, tab_spec)])
    ks = _mm(h, w_ks, _make_epi_rope_a(1.0, seq, tm, True), tm=tm, tn=512, out_dtype=BF16,
             name="proj_ksel", extras=[(tab, tab_spec)])
    vv = _mm(h, w_vv, _epi_plain, tm=tm, tn=512, out_dtype=BF16, name="proj_v")
    cqkv = _mm(h, w_c, _epi_plain, tm=tm, tn=mq + mkv, out_dtype=F32, name="proj_c")
    gk = _mm(h, w_gk, _epi_gate_kr, tm=tm, tn=2 * LANES, out_dtype=BF16, name="proj_gate_kr",
             extras=[(tab, tab_spec)])

    def to16(a, width):
        a = a.reshape(bsz, seq, g_, width)[..., :hd]
        a = jnp.transpose(a, (0, 2, 1, 3))
        return a.reshape(bsz * g_, seq // CMP_STRIDE, CMP_STRIDE * hd)

    k16 = to16(kk[:, :g_ * LANES], LANES)
    v16 = to16(vv[:, 2 * g_ * LANES:2 * g_ * LANES + g_ * hd], hd)
    pe = jnp.pad(cmp_pe.reshape(2, 1, CMP_BLOCK * hd), ((0, 0), (0, SUBLANES - 1), (0, 0))).astype(BF16)
    w2p = jnp.pad(cmp_w2, ((0, 0), (0, 0), (0, LANES - hd))).astype(BF16)
    kc, vc = _compress(k16, v16, cmp_w1.astype(BF16), pe, cmp_b1.reshape(2, 1, CMP_HIDDEN), w2p)
    o_cmp, bias = _cmp_attn(qn, kc, vc, _overlap_t(seq), bsz, seq, tq=min(1024, seq))
    o_sel = _flash(qn, ks, vv, mode="sel", bsz=bsz, seq=seq, nslots=g_ // 2, nr=NSA_R, nh=2,
                   tq=512, tk=512, bias=bias, name="nsa_sel_attn")
    o_win = _flash(qn, kk, vv, mode="win", bsz=bsz, seq=seq, nslots=g_ // 2, nr=NSA_R, nh=2,
                   tq=256, tk=256, k_off=g_, v_off=g_, name="nsa_win_attn")
    o_nsa = _combine(o_cmp, o_sel, o_win, gk, 0)

    qd = MLA_NOPE_DIM + MLA_ROPE_DIM
    w_qup = _slots(mla_w_q_up.astype(BF16), MLA_HEADS, qd)
    kvu = mla_w_kv_up.astype(BF16).reshape(mkv, MLA_HEADS, MLA_NOPE_DIM + MLA_V_DIM)
    w_kup = _slots(kvu[:, :, :MLA_NOPE_DIM].reshape(mkv, -1), MLA_HEADS, MLA_NOPE_DIM)
    w_vup = _slots(kvu[:, :, MLA_NOPE_DIM:].reshape(mkv, -1), MLA_HEADS, MLA_V_DIM)
    gq_spec = pl.BlockSpec((1, mq), lambda i, j: (0, 0))
    gkv_spec = pl.BlockSpec((1, mkv), lambda i, j: (0, 0))
    ckv_spec = pl.BlockSpec((tm, mkv), lambda i, j: (i, mq // mkv))
    q_mla = _mm(cqkv, w_qup, _epi_rope_b_scaled, tm=tm, tn=512,
                out_dtype=BF16, name="mla_q_up", pro=_pro_rms,
                x_spec=pl.BlockSpec((tm, mq), lambda i, j: (i, 0)),
                extras=[(mla_q_norm_g.reshape(1, mq), gq_spec), (tab, tab_spec)])
    k_mla = _mm(cqkv, w_kup, _epi_add_kr, tm=tm, tn=512, out_dtype=BF16, name="mla_k_up",
                pro=_pro_rms, x_spec=ckv_spec,
                extras=[(mla_kv_norm_g.reshape(1, mkv), gkv_spec),
                        (gk, pl.BlockSpec((tm, LANES), lambda i, j: (i, 1)))])
    v_mla = _mm(cqkv, w_vup, _epi_plain_g, tm=tm, tn=512, out_dtype=BF16, name="mla_v_up",
                pro=_pro_rms, x_spec=ckv_spec,
                extras=[(mla_kv_norm_g.reshape(1, mkv), gkv_spec)])
    o_mla = _flash(q_mla, k_mla, v_mla, mode="causal", bsz=bsz, seq=seq, nslots=MLA_HEADS // 8, nr=1,
                   nh=8, tq=512, tk=512, name="mla_attn")

    return _merge(h, w_gm, o_nsa, w_branch_nsa.astype(BF16), o_mla, w_branch_mla.astype(BF16), tm)


def _peer(h2, x1, post_g, gate, seq, peer_w_q, peer_sub_keys, peer_u, peer_v):
    qry = _mm(h2, peer_w_q.astype(BF16), _epi_plain, tm=min(PROJ_TM, h2.shape[0]), tn=512,
              out_dtype=BF16, name="peer_q")
    sk = peer_sub_keys.astype(BF16).reshape(2 * PEER_HEADS, PEER_N_KEYS, -1)
    tb, tf = _peer_route(qry, sk, PEER_TILE // PEER_N_KEYS)
    return _peer_dense(h2, peer_u.astype(BF16), jnp.transpose(peer_v).astype(BF16), tb, tf,
                       x1, post_g, gate, seq, PEER_TILE)


def _block(x, c, positions, ada_w, ada_b, attn_pre_g, attn_post_g, w_in, cmp_pe, cmp_w1, cmp_b1,
           cmp_w2, mla_q_norm_g, mla_w_q_up, mla_kv_norm_g, mla_w_kv_up, w_branch_nsa, w_branch_mla,
           w_out, ffn_pre_g, ffn_post_g, peer_w_q, peer_sub_keys, peer_u, peer_v):
    bsz, seq, d = x.shape
    x2 = x.reshape(bsz * seq, d)
    mod = _adaln(c, ada_w, ada_b)
    sh_a, sc_a, g_a, sh_f, sc_f, g_f = [mod[:, i * d:(i + 1) * d] for i in range(6)]

    h = _normmod(x2, attn_pre_g, sc_a, sh_a, seq)
    merged = _token_mixer(h, positions, bsz, seq, w_in, cmp_pe, cmp_w1, cmp_b1, cmp_w2, mla_q_norm_g,
                          mla_w_q_up, mla_kv_norm_g, mla_w_kv_up, w_branch_nsa, w_branch_mla)
    x1, h2 = _out_proj(merged, w_out.astype(BF16), x2, attn_post_g, g_a, ffn_pre_g, sc_f, sh_f, seq)

    out = _peer(h2, x1, ffn_post_g, g_f, seq, peer_w_q, peer_sub_keys, peer_u, peer_v)
    return out.reshape(bsz, seq, d)


def kernel(x, c, positions, ada_w, ada_b, attn_pre_g, attn_post_g, w_in, cmp_pe, cmp_w1, cmp_b1,
           cmp_w2, mla_q_norm_g, mla_w_q_up, mla_kv_norm_g, mla_w_kv_up, w_branch_nsa, w_branch_mla,
           w_out, ffn_pre_g, ffn_post_g, peer_w_q, peer_sub_keys, peer_u, peer_v):
    depth = ada_w.shape[0]
    for l in range(depth):
        x = _block(x, c, positions, ada_w[l], ada_b[l], attn_pre_g[l], attn_post_g[l], w_in[l],
                   cmp_pe[l], cmp_w1[l], cmp_b1[l], cmp_w2[l], mla_q_norm_g[l], mla_w_q_up[l],
                   mla_kv_norm_g[l], mla_w_kv_up[l], w_branch_nsa[l], w_branch_mla[l], w_out[l],
                   ffn_pre_g[l], ffn_post_g[l], peer_w_q[l], peer_sub_keys[l], peer_u[l], peer_v[l])
    return x
```

```python
import functools

import numpy as np
import jax
import jax.numpy as jnp
from jax import lax
from jax.experimental import pallas as pl
from jax.experimental.pallas import tpu as pltpu

F32 = jnp.float32
BF16 = jnp.bfloat16
I32 = jnp.int32

LANES = 128
SUBLANES = 8

NSA_HEADS = 16
NSA_GROUPS = 4
NSA_R = NSA_HEADS // NSA_GROUPS
NSA_HEAD_DIM = 64
NSA_ROPE_DIM = NSA_HEAD_DIM // 4
CMP_BLOCK = 32
CMP_STRIDE = 16
CMP_HIDDEN = 256
SEL_BLOCK = 64
SEL_SHIFT = 6
SEL_TOPK = 16
WINDOW = 512
MLA_HEADS = 16
MLA_NOPE_DIM = 64
MLA_ROPE_DIM = 32
MLA_V_DIM = 64
PEER_HEADS = 8
PEER_N_KEYS = 128
PEER_TOPK = 16
ROPE_THETA = 500000.0
NORM_EPS = 1e-6
NEG = -1e30
FORCE_BONUS = 1e4
LOG2E = 1.4426950408889634
MLA_ROPE_OFF = MLA_NOPE_DIM
PROJ_TM = 1024


def _cparams(sem, vmem_mb=None):
    kw = dict(dimension_semantics=sem)
    if vmem_mb is not None:
        kw["vmem_limit_bytes"] = vmem_mb * 1024 * 1024
    return pltpu.CompilerParams(**kw)


def _rms(x, g):
    return x * lax.rsqrt(jnp.mean(x * x, axis=-1, keepdims=True) + NORM_EPS) * g


def _gelu(x):
    return jax.nn.gelu(x, approximate=True)


def _adaln_kernel(c_ref, w_ref, b_ref, o_ref):
    c = c_ref[...]
    s = (c * jax.nn.sigmoid(c)).astype(BF16)
    o_ref[...] = jnp.dot(s, w_ref[...].astype(BF16), preferred_element_type=F32) + b_ref[...]


def _adaln(c, w, b, tn=1024):
    bsz, d = c.shape
    n = w.shape[1]
    cp = jnp.pad(c, ((0, SUBLANES - bsz), (0, 0)))
    out = pl.pallas_call(
        _adaln_kernel,
        grid=(n // tn,),
        in_specs=[pl.BlockSpec((SUBLANES, d), lambda j: (0, 0)),
                  pl.BlockSpec((d, tn), lambda j: (0, j)),
                  pl.BlockSpec((1, tn), lambda j: (0, j))],
        out_specs=pl.BlockSpec((SUBLANES, tn), lambda j: (0, j)),
        out_shape=jax.ShapeDtypeStruct((SUBLANES, n), F32),
        compiler_params=_cparams(("arbitrary",), 40),
        name="adaln",
    )(cp, w, b.reshape(1, n))
    return out[:bsz]


def _rope_table_kernel(pos_ref, inv_ref, sgn_ref, o_ref):
    pos = pos_ref[...]
    for k in range(2):
        ang = pos * inv_ref[k:k + 1, :]
        o_ref[:, (2 * k) * LANES:(2 * k + 1) * LANES] = jnp.cos(ang)
        o_ref[:, (2 * k + 1) * LANES:(2 * k + 2) * LANES] = jnp.sin(ang) * sgn_ref[k:k + 1, :]


def _rope_tables(positions, tm=512):
    n = positions.size
    half_a = NSA_ROPE_DIM // 2
    half_b = MLA_ROPE_DIM // 2
    inv_a = ROPE_THETA ** (-jnp.arange(half_a, dtype=F32) * (2.0 / NSA_ROPE_DIM))
    inv_b = ROPE_THETA ** (-jnp.arange(half_b, dtype=F32) * (2.0 / MLA_ROPE_DIM))
    inv = jnp.zeros((2, LANES), F32)
    inv = inv.at[0, 0:half_a].set(inv_a).at[0, half_a:2 * half_a].set(inv_a)
    o = MLA_ROPE_OFF
    inv = inv.at[1, o:o + half_b].set(inv_b).at[1, o + half_b:o + 2 * half_b].set(inv_b)
    sgn = np.zeros((2, LANES), np.float32)
    sgn[0, 0:half_a] = -1.0
    sgn[0, half_a:2 * half_a] = 1.0
    sgn[1, o:o + half_b] = -1.0
    sgn[1, o + half_b:o + 2 * half_b] = 1.0
    posf = jnp.broadcast_to(positions.reshape(n, 1).astype(F32), (n, LANES))
    return pl.pallas_call(
        _rope_table_kernel,
        grid=(n // tm,),
        in_specs=[pl.BlockSpec((tm, LANES), lambda i: (i, 0)),
                  pl.BlockSpec((2, LANES), lambda i: (0, 0)),
                  pl.BlockSpec((2, LANES), lambda i: (0, 0))],
        out_specs=pl.BlockSpec((tm, 4 * LANES), lambda i: (i, 0)),
        out_shape=jax.ShapeDtypeStruct((n, 4 * LANES), F32),
        compiler_params=_cparams(("arbitrary",)),
        name="rope_tables",
    )(posf, inv, jnp.asarray(sgn))


def _rope_slot(a, cos, sin, half, off):
    lane = lax.broadcasted_iota(I32, (1, LANES), 1)
    first = (lane - off) < half
    partner = jnp.where(first, pltpu.roll(a, LANES - half, 1), pltpu.roll(a, half, 1))
    return a * cos + partner * sin


def _normmod_kernel(x_ref, g_ref, sc_ref, sh_ref, o_ref):
    y = _rms(x_ref[...], g_ref[...])
    o_ref[...] = (y * (1.0 + sc_ref[0]) + sh_ref[0]).astype(o_ref.dtype)


def _normmod(x, g, sc, sh, seq, tm=512):
    n, d = x.shape
    bsz = sc.shape[0]
    bspec = pl.BlockSpec((1, 1, d), lambda i: ((i * tm) // seq, 0, 0))
    return pl.pallas_call(
        _normmod_kernel,
        grid=(n // tm,),
        in_specs=[pl.BlockSpec((tm, d), lambda i: (i, 0)),
                  pl.BlockSpec((1, d), lambda i: (0, 0)), bspec, bspec],
        out_specs=pl.BlockSpec((tm, d), lambda i: (i, 0)),
        out_shape=jax.ShapeDtypeStruct((n, d), BF16),
        compiler_params=_cparams(("arbitrary",)),
        name="normmod",
    )(x, g.reshape(1, d), sc.reshape(bsz, 1, d), sh.reshape(bsz, 1, d))


def _out_proj_kernel(m_ref, w_ref, x_ref, pg_ref, ga_ref, fg_ref, sc_ref, sh_ref, x1_ref, h2_ref):
    y = jnp.dot(m_ref[...], w_ref[...], preferred_element_type=F32)
    x1 = x_ref[...] + ga_ref[0] * _rms(y, pg_ref[...])
    x1_ref[...] = x1
    h2_ref[...] = (_rms(x1, fg_ref[...]) * (1.0 + sc_ref[0]) + sh_ref[0]).astype(h2_ref.dtype)


def _out_proj(merged, w, x, post_g, gate, pre_g, sc, sh, seq, tm=512):
    n, d = x.shape
    bsz = gate.shape[0]
    row = lambda width: pl.BlockSpec((tm, width), lambda i: (i, 0))
    vec = pl.BlockSpec((1, d), lambda i: (0, 0))
    bvec = pl.BlockSpec((1, 1, d), lambda i: ((i * tm) // seq, 0, 0))
    return pl.pallas_call(
        _out_proj_kernel,
        grid=(n // tm,),
        in_specs=[row(merged.shape[1]), pl.BlockSpec(w.shape, lambda i: (0, 0)), row(d), vec, bvec,
                  vec, bvec, bvec],
        out_specs=[row(d), row(d)],
        out_shape=[jax.ShapeDtypeStruct((n, d), F32), jax.ShapeDtypeStruct((n, d), BF16)],
        compiler_params=_cparams(("arbitrary",), 56),
        name="out_proj",
    )(merged, w, x, post_g.reshape(1, d), gate.reshape(bsz, 1, d), pre_g.reshape(1, d),
      sc.reshape(bsz, 1, d), sh.reshape(bsz, 1, d))


def _mm(x, w, epi, *, tm, tn, out_dtype, name, extras=(), pro=None, x_spec=None, vmem_mb=None):
    m = x.shape[0]
    k, nc = w.shape
    if x_spec is None:
        x_spec = pl.BlockSpec((tm, k), lambda i, j: (i, 0))
    in_specs = [x_spec, pl.BlockSpec((k, tn), lambda i, j: (0, j))]
    args = [x, w]
    for arr, spec in extras:
        in_specs.append(spec)
        args.append(arr)

    def kern(x_ref, w_ref, *rest):
        o_ref = rest[-1]
        ex = rest[:-1]
        xv = x_ref[...]
        if pro is not None:
            xv = pro(xv, *ex)
        acc = jnp.dot(xv, w_ref[...], preferred_element_type=F32)
        epi(acc, o_ref, *ex)

    return pl.pallas_call(
        kern,
        grid=(m // tm, nc // tn),
        in_specs=in_specs,
        out_specs=pl.BlockSpec((tm, tn), lambda i, j: (i, j)),
        out_shape=jax.ShapeDtypeStruct((m, nc), out_dtype),
        compiler_params=_cparams(("arbitrary", "arbitrary"), vmem_mb),
        name=name,
    )(*args)


def _epi_plain(acc, o_ref, *ex):
    o_ref[...] = acc.astype(o_ref.dtype)


def _make_epi_rope_a(scale, seq=None, tm=None, onehot=False):
    half = NSA_ROPE_DIM // 2

    def epi(acc, o_ref, tab_ref):
        cos = tab_ref[:, 0:LANES]
        sin = tab_ref[:, LANES:2 * LANES]
        if onehot:
            base = lax.rem(pl.program_id(0) * tm, seq)
            t = base + lax.broadcasted_iota(I32, (acc.shape[0], LANES), 0)
            lane = lax.broadcasted_iota(I32, (acc.shape[0], LANES), 1)
            hot = jnp.where(lane - SEL_BLOCK == jnp.right_shift(t, SEL_SHIFT), 1.0, 0.0)
        for s in range(acc.shape[1] // LANES):
            r = _rope_slot(acc[:, s * LANES:(s + 1) * LANES], cos, sin, half, 0)
            if scale != 1.0:
                r = r * scale
            if onehot:
                r = r + hot
            o_ref[:, s * LANES:(s + 1) * LANES] = r.astype(o_ref.dtype)
    return epi


def _epi_rope_b_scaled(acc, o_ref, g_ref, tab_ref):
    cos = tab_ref[:, 2 * LANES:3 * LANES]
    sin = tab_ref[:, 3 * LANES:4 * LANES]
    scale = (MLA_NOPE_DIM + MLA_ROPE_DIM) ** -0.5 * LOG2E
    for s in range(acc.shape[1] // LANES):
        r = _rope_slot(acc[:, s * LANES:(s + 1) * LANES], cos, sin, MLA_ROPE_DIM // 2, MLA_ROPE_OFF)
        o_ref[:, s * LANES:(s + 1) * LANES] = (r * scale).astype(o_ref.dtype)


def _epi_gate_kr(acc, o_ref, tab_ref):
    o_ref[:, 0:LANES] = jax.nn.sigmoid(acc[:, 0:LANES]).astype(o_ref.dtype)
    cos = tab_ref[:, 2 * LANES:3 * LANES]
    sin = tab_ref[:, 3 * LANES:4 * LANES]
    r = _rope_slot(acc[:, LANES:2 * LANES], cos, sin, MLA_ROPE_DIM // 2, MLA_ROPE_OFF)
    o_ref[:, LANES:2 * LANES] = r.astype(o_ref.dtype)


def _pro_rms(xv, g_ref, *ex):
    return _rms(xv, g_ref[...]).astype(BF16)


def _epi_add_kr(acc, o_ref, g_ref, kr_ref):
    kr = kr_ref[...].astype(F32)
    for s in range(acc.shape[1] // LANES):
        o_ref[:, s * LANES:(s + 1) * LANES] = (acc[:, s * LANES:(s + 1) * LANES] + kr).astype(o_ref.dtype)


def _compress_kernel(k16_ref, v16_ref, w1_ref, pe_ref, b1_ref, w2_ref, kc_ref, vc_ref):
    half = w1_ref.shape[1] // 2
    for which, (src, dst) in enumerate(((k16_ref, kc_ref), (v16_ref, vc_ref))):
        xb = src[...]
        a = jnp.dot(xb, w1_ref[which, 0:half, :], preferred_element_type=F32)
        b = jnp.dot(xb, w1_ref[which, half:2 * half, :], preferred_element_type=F32)
        c = jnp.dot(pe_ref[which], w1_ref[which], preferred_element_type=F32)[0:1, :] + b1_ref[which]
        rows = a.shape[0]
        pre = a + pltpu.roll(b, rows - 1, 0) + c
        hid = _gelu(pre).astype(BF16)
        dst[...] = jnp.dot(hid, w2_ref[which], preferred_element_type=F32).astype(dst.dtype)


def _compress(k16, v16, w1, pe, b1, w2p):
    bg, rows, feat = k16.shape
    blk = pl.BlockSpec((None, rows, feat), lambda i: (i, 0, 0))
    full = lambda a: pl.BlockSpec(a.shape, lambda i: (0,) * a.ndim)
    out = pl.BlockSpec((None, rows, LANES), lambda i: (i, 0, 0))
    return pl.pallas_call(
        _compress_kernel,
        grid=(bg,),
        in_specs=[blk, blk, full(w1), full(pe), full(b1), full(w2p)],
        out_specs=[out, out],
        out_shape=[jax.ShapeDtypeStruct((bg, rows, LANES), BF16)] * 2,
        compiler_params=_cparams(("arbitrary",)),
        name="nsa_compress",
    )(k16, v16, w1, pe, b1, w2p)


def _row_bcast(col, width):
    if width % LANES:
        return jnp.broadcast_to(col, (col.shape[0], width))
    tile = jnp.broadcast_to(col, (col.shape[0], LANES))
    return jnp.concatenate([tile] * (width // LANES), axis=1)


def _cmp_attn_kernel(q_ref, kc_ref, vc_ref, ovt_ref, o_ref, bias_ref, *, tq, n_sel):
    qi = pl.program_id(2)
    ncmp = kc_ref.shape[0]
    t_row = qi * tq + lax.broadcasted_iota(I32, (tq, ncmp), 0)
    n_col = lax.broadcasted_iota(I32, (tq, ncmp), 1)
    cmask = (n_col * CMP_STRIDE + (CMP_BLOCK - 1)) <= t_row
    kc = kc_ref[...]
    vc = vc_ref[...]
    imp_t = jnp.zeros((n_sel, tq), F32)
    outs = []
    for r in range(NSA_R):
        q = q_ref[:, r * LANES:(r + 1) * LANES]
        s = lax.dot_general(q, kc, (((1,), (1,)), ((), ())), preferred_element_type=F32)
        s = jnp.where(cmask, s, NEG)
        e = jnp.exp2(s - _row_bcast(jnp.max(s, axis=1, keepdims=True), ncmp))
        p = e / _row_bcast(jnp.sum(e, axis=1, keepdims=True), ncmp)
        p = jnp.where(cmask, p, 0.0).astype(BF16)
        outs.append(jnp.dot(p, vc, preferred_element_type=F32))
        imp_t = imp_t + lax.dot_general(ovt_ref[...], p, (((1,), (1,)), ((), ())),
                                        preferred_element_type=F32)
    _store_heads_compact(o_ref, outs)
    blk = lax.broadcasted_iota(I32, (n_sel, tq), 0)
    t = qi * tq + lax.broadcasted_iota(I32, (n_sel, tq), 1)
    tb = jnp.right_shift(t, SEL_SHIFT)
    forced = (blk == 0) | (blk == tb) | (blk == tb - 1)
    valid = blk * SEL_BLOCK <= t
    x = jnp.where(valid, imp_t + jnp.where(forced, FORCE_BONUS, 0.0), NEG)
    sel = jnp.zeros((n_sel, tq), F32)
    for _ in range(min(SEL_TOPK, n_sel)):
        m = jnp.max(x, axis=0, keepdims=True)
        idx = jnp.min(jnp.where(x == m, blk, n_sel), axis=0, keepdims=True)
        hit = blk == idx
        sel = jnp.where(hit, 1.0, sel)
        x = jnp.where(hit, -jnp.inf, x)
    bias_t = jnp.where(sel > 0.5, 0.0, NEG)
    parts = [jnp.zeros((SEL_BLOCK, tq), F32), bias_t]
    if LANES - SEL_BLOCK - n_sel > 0:
        parts.append(jnp.zeros((LANES - SEL_BLOCK - n_sel, tq), F32))
    full_t = jnp.concatenate(parts, axis=0)
    bias_ref[...] = full_t.T.astype(bias_ref.dtype)


def _cmp_attn(qn, kc, vc, ovt, bsz, seq, tq=256):
    n = qn.shape[0]
    nq = seq // tq
    n_sel = seq // SEL_BLOCK
    ncmp = kc.shape[1]
    qspec = pl.BlockSpec((tq, NSA_R * LANES), lambda b, g, qi: (b * nq + qi, g))
    kspec = pl.BlockSpec((None, ncmp, LANES), lambda b, g, qi: (b * NSA_GROUPS + g, 0, 0))
    return pl.pallas_call(
        functools.partial(_cmp_attn_kernel, tq=tq, n_sel=n_sel),
        grid=(bsz, NSA_GROUPS, nq),
        in_specs=[qspec, kspec, kspec, pl.BlockSpec(ovt.shape, lambda b, g, qi: (0, 0))],
        out_specs=[pl.BlockSpec((tq, NSA_R * NSA_HEAD_DIM), lambda b, g, qi: (b * nq + qi, g)),
                   pl.BlockSpec((tq, LANES), lambda b, g, qi: (b * nq + qi, g))],
        out_shape=[jax.ShapeDtypeStruct((n, NSA_HEADS * NSA_HEAD_DIM), BF16),
                   jax.ShapeDtypeStruct((n, NSA_GROUPS * LANES), BF16)],
        compiler_params=_cparams(("arbitrary",) * 3),
        name="nsa_cmp_attn",
    )(qn, kc, vc, ovt)


def _store_heads_compact(o_ref, heads):
    hd = NSA_HEAD_DIM
    for p in range(len(heads) // 2):
        pair = jnp.concatenate([heads[2 * p][:, 0:hd], heads[2 * p + 1][:, 0:hd]], axis=1)
        o_ref[:, p * LANES:(p + 1) * LANES] = pair.astype(o_ref.dtype)


def _tri_step(p, s, nq):
    first = s <= p
    return jnp.where(first, p, nq - 1 - p), jnp.where(first, s, s - p - 1)


def _flash_kernel(*refs, mode, tq, tk, nr, nh, window, nq):
    if mode == "sel":
        q_ref, k_ref, v_ref, bias_ref, o_ref, qs, m_s, l_s, acc_s = refs
    else:
        q_ref, k_ref, v_ref, o_ref, qs, m_s, l_s, acc_s = refs
    if mode == "win":
        qi = pl.program_id(2)
        kk = pl.program_id(3)
        last = kk == pl.num_programs(3) - 1
    else:
        qi, kk = _tri_step(pl.program_id(2), pl.program_id(3), nq)
        last = kk == qi
    rows = nr * tq

    @pl.when(kk == 0)
    def _init():
        for r in range(nh * nr):
            qr = q_ref[:, r * LANES:(r + 1) * LANES]
            if mode == "sel":
                qr = qr + bias_ref[:, (r // nr) * LANES:(r // nr + 1) * LANES]
            qs[r * tq:(r + 1) * tq, :] = qr
        m_s[...] = jnp.full(m_s.shape, -jnp.inf, F32)
        l_s[...] = jnp.zeros(l_s.shape, F32)
        acc_s[...] = jnp.zeros(acc_s.shape, F32)

    if mode == "win":
        kidx = qi * (tq // tk) - window // tk + kk
        needed = kidx >= 0
        full_vis = (kidx * tk + tk - 1 <= qi * tq) & (qi * tq + tq - 1 - kidx * tk < window)
    else:
        kidx = kk
        needed = True
        full_vis = kk < qi

    nt = tk // LANES

    def update(masked):
        if masked:
            rel = (lax.broadcasted_iota(I32, (rows, tk), 0) & (tq - 1)) - lax.broadcasted_iota(I32, (rows, tk), 1)
            off = kidx * tk - qi * tq
            vis = rel >= off
            if mode == "win":
                vis = vis & (rel < off + window)
        for hh in range(nh):
            rs = slice(hh * rows, (hh + 1) * rows)
            cs = slice(hh * LANES, (hh + 1) * LANES)
            s = lax.dot_general(qs[rs, :], k_ref[:, cs], (((1,), (1,)), ((), ())),
                                preferred_element_type=F32)
            if masked:
                s = jnp.where(vis, s, NEG)
            m_prev = m_s[rs, :]
            m_new = jnp.maximum(m_prev, jnp.max(s, axis=1, keepdims=True))
            p = jnp.exp2(s - jnp.concatenate([m_new] * nt, axis=1))
            alpha = jnp.exp2(m_prev - m_new)
            psum = p[:, 0:LANES]
            for c in range(1, nt):
                psum = psum + p[:, c * LANES:(c + 1) * LANES]
            l_s[rs, :] = alpha * l_s[rs, :] + psum
            acc_s[rs, :] = alpha * acc_s[rs, :] + jnp.dot(p.astype(BF16), v_ref[:, cs],
                                                          preferred_element_type=F32)
            m_s[rs, :] = m_new

    @pl.when(needed & full_vis)
    def _full():
        update(False)

    @pl.when(needed & jnp.logical_not(full_vis))
    def _edge():
        update(True)

    @pl.when(last)
    def _fin():
        out = acc_s[...] / jnp.sum(l_s[...], axis=1, keepdims=True)
        _store_heads_compact(o_ref, [out[r * tq:(r + 1) * tq, :] for r in range(nh * nr)])


def _flash(q, k, v, *, mode, bsz, seq, nslots, nr, tq, tk, nh=1, k_off=0, v_off=0, bias=None,
           name):
    assert tq & (tq - 1) == 0
    n = q.shape[0]
    nq = seq // tq
    nkb = seq // tk
    if mode == "win":
        assert tq % tk == 0 and WINDOW % tk == 0
        grid = (bsz, nslots, nq, WINDOW // tk + tq // tk)

        def qrow(b, p, s):
            return b * nq + p

        def krow(b, p, s):
            return b * nkb + jnp.maximum(p * (tq // tk) - WINDOW // tk + s, 0)
    else:
        assert tq == tk
        grid = (bsz, nslots, (nq + 1) // 2, nq + 1)

        def qrow(b, p, s):
            return b * nq + _tri_step(p, s, nq)[0]

        def krow(b, p, s):
            return b * nkb + _tri_step(p, s, nq)[1]

    qspec = pl.BlockSpec((tq, nh * nr * LANES), lambda b, g, p, s: (qrow(b, p, s), g))
    in_specs = [qspec,
                pl.BlockSpec((tk, nh * LANES), lambda b, g, p, s: (krow(b, p, s), k_off // nh + g)),
                pl.BlockSpec((tk, nh * LANES), lambda b, g, p, s: (krow(b, p, s), v_off // nh + g))]
    args = [q, k, v]
    if mode == "sel":
        in_specs.append(pl.BlockSpec((tq, nh * LANES), lambda b, g, p, s: (qrow(b, p, s), g)))
        args.append(bias)
    rows = nh * nr * tq
    ow = nh * nr * NSA_HEAD_DIM
    return pl.pallas_call(
        functools.partial(_flash_kernel, mode=mode, tq=tq, tk=tk, nr=nr, nh=nh, window=WINDOW, nq=nq),
        grid=grid,
        in_specs=in_specs,
        out_specs=pl.BlockSpec((tq, ow), lambda b, g, p, s: (qrow(b, p, s), g)),
        out_shape=jax.ShapeDtypeStruct((n, nslots * ow), BF16),
        scratch_shapes=[pltpu.VMEM((rows, LANES), BF16), pltpu.VMEM((rows, LANES), F32),
                        pltpu.VMEM((rows, LANES), F32), pltpu.VMEM((rows, LANES), F32)],
        compiler_params=_cparams(("arbitrary",) * 4, 48),
        name=name,
    )(*args)


def _combine_kernel(oc_ref, os_ref, ow_ref, g_ref, e_ref, o_ref):
    g = g_ref[...]
    acc = None
    for br, src in enumerate((oc_ref, os_ref, ow_ref)):
        ge = jnp.dot(g, e_ref[br], preferred_element_type=F32)
        term = ge * src[...].astype(F32)
        acc = term if acc is None else acc + term
    o_ref[...] = acc.astype(o_ref.dtype)


def _combine(o_cmp, o_sel, o_win, gates, gates_col, tm=512):
    n, w = o_cmp.shape
    e = np.zeros((3, LANES, w), np.float32)
    for h in range(NSA_HEADS):
        for br in range(3):
            e[br, h * 3 + br, h * NSA_HEAD_DIM:(h + 1) * NSA_HEAD_DIM] = 1.0
    row = pl.BlockSpec((tm, w), lambda i: (i, 0))
    return pl.pallas_call(
        _combine_kernel,
        grid=(n // tm,),
        in_specs=[row, row, row, pl.BlockSpec((tm, LANES), lambda i: (i, gates_col)),
                  pl.BlockSpec(e.shape, lambda i: (0, 0, 0))],
        out_specs=row,
        out_shape=jax.ShapeDtypeStruct((n, w), BF16),
        compiler_params=_cparams(("arbitrary",)),
        name="nsa_combine",
    )(o_cmp, o_sel, o_win, gates, jnp.asarray(e, BF16))


def _merge_kernel(h_ref, wg0_ref, wg1_ref, a_ref, wa_ref, b_ref, wb_ref, o_ref):
    h = h_ref[...]
    g0 = jax.nn.sigmoid(jnp.dot(h, wg0_ref[...], preferred_element_type=F32))
    g1 = jax.nn.sigmoid(jnp.dot(h, wg1_ref[...], preferred_element_type=F32))
    ya = jnp.dot(a_ref[...], wa_ref[...], preferred_element_type=F32)
    yb = jnp.dot(b_ref[...], wb_ref[...], preferred_element_type=F32)
    o_ref[...] = (g0 * ya + g1 * yb).astype(o_ref.dtype)


def _merge(h, w_gm, o_nsa, w_nsa, o_mla, w_mla, tm, tn=512):
    n, k = o_nsa.shape
    kh = h.shape[1]
    d = w_nsa.shape[1]
    nj = d // tn
    row = pl.BlockSpec((tm, k), lambda i, j: (i, 0))
    wsp = pl.BlockSpec((k, tn), lambda i, j: (0, j))
    return pl.pallas_call(
        _merge_kernel,
        grid=(n // tm, nj),
        in_specs=[pl.BlockSpec((tm, kh), lambda i, j: (i, 0)),
                  pl.BlockSpec((kh, tn), lambda i, j: (0, j)),
                  pl.BlockSpec((kh, tn), lambda i, j: (0, nj + j)),
                  row, wsp, row, wsp],
        out_specs=pl.BlockSpec((tm, tn), lambda i, j: (i, j)),
        out_shape=jax.ShapeDtypeStruct((n, d), BF16),
        compiler_params=_cparams(("arbitrary", "arbitrary"), 48),
        name="branch_merge",
    )(h, w_gm, w_gm, o_nsa, w_nsa, o_mla, w_mla)


def _merge_desc(xs):
    xs = list(xs)
    n = len(xs)
    stride = n // 2
    while stride >= 1:
        for i in range(n):
            p = i ^ stride
            if p > i:
                xs[i], xs[p] = jnp.maximum(xs[i], xs[p]), jnp.minimum(xs[i], xs[p])
        stride //= 2
    return xs


def _top16_sorted(xs):
    xs = list(xs)
    n = len(xs)
    size = 2
    while size <= n:
        stride = size // 2
        while stride >= 1:
            for i in range(n):
                p = i ^ stride
                if p > i:
                    hi, lo = jnp.maximum(xs[i], xs[p]), jnp.minimum(xs[i], xs[p])
                    xs[i], xs[p] = (hi, lo) if (i & size) == 0 else (lo, hi)
            stride //= 2
        size *= 2
    shift = SUBLANES // 2
    while shift >= 1:
        ys = [pltpu.roll(x, shift, 0) for x in xs]
        xs = _merge_desc([jnp.maximum(xs[k], ys[n - 1 - k]) for k in range(n)])
        shift //= 2
    return xs


def _prefix_len(pred, vs):
    p8 = pred(vs[7])
    p4 = pred(jnp.where(p8, vs[11], vs[3]))
    p2 = pred(jnp.where(p8, jnp.where(p4, vs[13], vs[9]), jnp.where(p4, vs[5], vs[1])))
    q = [jnp.where(p2, vs[4 * i + 2], vs[4 * i]) for i in range(4)]
    p1 = pred(jnp.where(p8, jnp.where(p4, q[3], q[2]), jnp.where(p4, q[1], q[0])))
    n = (jnp.where(p8, 8.0, 0.0) + jnp.where(p4, 4.0, 0.0)) + (jnp.where(p2, 2.0, 0.0) + jnp.where(p1, 1.0, 0.0))
    return jnp.where(pred(vs[15]), 16.0, n)


def _rows8(x):
    return [x[k * SUBLANES:(k + 1) * SUBLANES, :] for k in range(x.shape[0] // SUBLANES)]


def _peer_route_kernel(q_ref, sk_ref, tb_ref, tf_ref):
    nk = PEER_N_KEYS
    sec = PEER_HEADS * nk
    tr = q_ref.shape[0]
    sub = lax.broadcasted_iota(I32, (SUBLANES, tr), 0)
    ninf = jnp.full((SUBLANES, tr), -jnp.inf, F32)

    def spread(vs):
        out = vs[SUBLANES - 1]
        for j in range(SUBLANES - 2, -1, -1):
            out = jnp.where(sub == j, vs[j], out)
        return out

    for h in range(PEER_HEADS):
        s = []
        for p in range(2):
            c = (2 * h + p) * nk
            s.append(lax.dot_general(sk_ref[2 * h + p], q_ref[:, c:c + nk], (((1,), (1,)), ((), ())),
                                     preferred_element_type=F32))
        x1, x2 = _rows8(s[0]), _rows8(s[1])
        v1, v2 = _top16_sorted(x1), _top16_sorted(x2)
        v2lo, v2hi = spread(v2[:SUBLANES]), spread(v2[SUBLANES:])
        cands = [v1[0] + v2lo, v1[0] + v2hi]
        for i in range(1, SUBLANES):
            c = v1[i] + v2lo
            cnt = PEER_TOPK // (i + 1)
            cands.append(jnp.where(sub < cnt, c, -jnp.inf) if cnt < SUBLANES else c)
        cands.append(spread(v1[SUBLANES:]) + v2[0])
        top = _top16_sorted(cands + [ninf] * (PEER_TOPK - len(cands)))
        tau = top[PEER_TOPK - 1]
        z = None
        for k in range(PEER_TOPK):
            ek = jnp.exp(top[k] - top[0])
            z = ek if z is None else z + ek
        rank2 = jnp.concatenate([_prefix_len(lambda v, x=x: v > x, v2) for x in x2], axis=0)
        count = jnp.concatenate([_prefix_len(lambda v, x=x: (x + v) >= tau, v2) for x in x1], axis=0)
        tb_ref[h * nk:(h + 1) * nk, :] = rank2.astype(tb_ref.dtype)
        tb_ref[sec + h * nk:sec + (h + 1) * nk, :] = jnp.exp(s[1] - v2[0][0:1, :]).astype(tb_ref.dtype)
        apb = tf_ref.shape[1] // (2 * PEER_HEADS)
        for k, vals in enumerate((count, jnp.exp(s[0] - v1[0][0:1, :]) / z[0:1, :])):
            r0 = (k * PEER_HEADS + h) * apb
            tf_ref[:, r0:r0 + apb, :] = vals.reshape(nk // apb, apb, tr)


def _peer_route(qry, sk, apb, tr=256):
    n, w = qry.shape
    nj = PEER_N_KEYS // apb
    rows = 2 * PEER_HEADS * apb
    return pl.pallas_call(
        _peer_route_kernel,
        grid=(n // tr,),
        in_specs=[pl.BlockSpec((tr, w), lambda i: (i, 0)),
                  pl.BlockSpec(sk.shape, lambda i: (0, 0, 0))],
        out_specs=[pl.BlockSpec((w, tr), lambda i: (0, i)),
                   pl.BlockSpec((nj, rows, tr), lambda i: (0, 0, i))],
        out_shape=[jax.ShapeDtypeStruct((w, n), BF16), jax.ShapeDtypeStruct((nj, rows, n), F32)],
        compiler_params=_cparams(("arbitrary",), 40),
        name="peer_route",
    )(qry, sk)


PEER_CHUNK = 32
PEER_TILE = 1024


def _peer_dense_kernel(h_ref, u_ref, vt_ref, tb_ref, tf_ref, x_ref, pg_ref, gf_ref, o_ref, acc_s, pt_s,
                       *, te):
    j = pl.program_id(1)
    nk = PEER_N_KEYS
    sec = PEER_HEADS * nk
    apb = te // nk
    gdt = pt_s.dtype

    @pl.when(j == 0)
    def _init():
        acc_s[...] = jnp.zeros(acc_s.shape, F32)

    def row_tile(row):
        r = tf_ref[row:row + 1, :]
        return jnp.broadcast_to(r.astype(gdt), (PEER_CHUNK, r.shape[1]))

    zt = lax.dot_general(u_ref[...], h_ref[...], (((1,), (1,)), ((), ())), preferred_element_type=F32)
    for al in range(apb):
        cnt = [row_tile(h * apb + al) for h in range(PEER_HEADS)]
        e1 = [row_tile((PEER_HEADS + h) * apb + al) for h in range(PEER_HEADS)]
        for c in range(nk // PEER_CHUNK):
            lo = c * PEER_CHUNK
            g = None
            for h in range(PEER_HEADS):
                rank2 = tb_ref[h * nk + lo:h * nk + lo + PEER_CHUNK, :]
                e2 = tb_ref[sec + h * nk + lo:sec + h * nk + lo + PEER_CHUNK, :]
                term = jnp.where(rank2 < cnt[h], e1[h] * e2, jnp.zeros((), gdt))
                g = term if g is None else g + term
            r0 = al * nk + lo
            pt_s[r0:r0 + PEER_CHUNK, :] = g * _gelu(zt[r0:r0 + PEER_CHUNK, :]).astype(gdt)
    acc_s[...] += jnp.dot(vt_ref[...], pt_s[...], preferred_element_type=F32)

    @pl.when(j == pl.num_programs(1) - 1)
    def _fin():
        o_ref[...] = x_ref[...] + gf_ref[0] * _rms(acc_s[...].T, pg_ref[...])


def _peer_dense(h2, u_bf, vt_bf, tb, tf, x1, post_g, gate, seq, te, tm=512):
    n, d = h2.shape
    ne = u_bf.shape[0]
    w = tb.shape[0]
    bsz = gate.shape[0]
    row = pl.BlockSpec((tm, d), lambda i, j: (i, 0))
    return pl.pallas_call(
        functools.partial(_peer_dense_kernel, te=te),
        grid=(n // tm, ne // te),
        in_specs=[row,
                  pl.BlockSpec((te, d), lambda i, j: (j, 0)),
                  pl.BlockSpec((d, te), lambda i, j: (0, j)),
                  pl.BlockSpec((w, tm), lambda i, j: (0, i)),
                  pl.BlockSpec((None, tf.shape[1], tm), lambda i, j: (j, 0, i)),
                  row,
                  pl.BlockSpec((1, d), lambda i, j: (0, 0)),
                  pl.BlockSpec((1, 1, d), lambda i, j: ((i * tm) // seq, 0, 0))],
        out_specs=row,
        out_shape=jax.ShapeDtypeStruct((n, d), F32),
        scratch_shapes=[pltpu.VMEM((d, tm), F32), pltpu.VMEM((te, tm), BF16)],
        compiler_params=_cparams(("arbitrary", "arbitrary"), 56),
        name="peer_dense",
    )(h2, u_bf, vt_bf, tb, tf, x1, post_g.reshape(1, d), gate.reshape(bsz, 1, d))


def _slots(w, n, width, off=0):
    k = w.shape[0]
    w = w.reshape(k, n, width)
    w = jnp.pad(w, ((0, 0), (0, 0), (off, LANES - width - off)))
    return w.reshape(k, n * LANES)


def _overlap_t(seq):
    n_cmp_rows = seq // CMP_STRIDE
    n_sel = seq // SEL_BLOCK
    cs = np.arange(n_cmp_rows) * CMP_STRIDE
    ce = cs + CMP_BLOCK - 1
    ss = np.arange(n_sel) * SEL_BLOCK
    ov = (cs[None, :] < ss[:, None] + SEL_BLOCK) & (ce[None, :] >= ss[:, None])
    ov[:, n_cmp_rows - 1] = False
    return jnp.asarray(ov.astype(np.float32), BF16)


def _token_mixer(h, positions, bsz, seq, w_in, cmp_pe, cmp_w1, cmp_b1, cmp_w2, mla_q_norm_g,
                 mla_w_q_up, mla_kv_norm_g, mla_w_kv_up, w_branch_nsa, w_branch_mla):
    d = h.shape[1]
    g_, hd = NSA_GROUPS, NSA_HEAD_DIM
    tab = _rope_tables(positions)
    tm = min(PROJ_TM, seq)
    tab_spec = pl.BlockSpec((tm, 4 * LANES), lambda i, j: (i, 0))

    wb = w_in.astype(BF16)
    o1 = NSA_HEADS * hd
    kv = wb[:, o1:o1 + 3 * 2 * g_ * hd].reshape(d, 3, 2, g_ * hd)
    o2 = o1 + 3 * 2 * g_ * hd
    o3 = o2 + 3 * NSA_HEADS
    mq = mla_w_q_up.shape[0]
    mkv = mla_w_kv_up.shape[0]
    o4 = o3 + mq
    o5 = o4 + mkv
    o6 = o5 + MLA_ROPE_DIM
    w_q = _slots(wb[:, :o1], NSA_HEADS, hd)
    w_kk = jnp.concatenate([_slots(kv[:, 0, 0], g_, hd), _slots(kv[:, 2, 0], g_, hd)], axis=1)
    w_ks = _slots(kv[:, 1, 0], g_, hd)
    w_vv = jnp.concatenate([_slots(kv[:, 1, 1], g_, hd), _slots(kv[:, 2, 1], g_, hd),
                            kv[:, 0, 1], jnp.zeros((d, g_ * hd), BF16)], axis=1)
    w_c = wb[:, o3:o5]
    w_gk = jnp.concatenate([jnp.pad(wb[:, o2:o3], ((0, 0), (0, LANES - 3 * NSA_HEADS))),
                            _slots(wb[:, o5:o6], 1, MLA_ROPE_DIM, MLA_ROPE_OFF)], axis=1)
    w_gm = wb[:, o6:]

    qn = _mm(h, w_q, _make_epi_rope_a(hd ** -0.5 * LOG2E), tm=tm, tn=512, out_dtype=BF16, name="proj_q",
             extras=[(tab, tab_spec)])
    kk = _mm(h, w_kk, _make_epi_rope_a(1.0), tm=tm, tn=512, out_dtype=BF16, name="proj_k",
             extras=[(tab, tab_spec)])
    ks = _mm(h, w_ks, _make_epi_rope_a(1.0, seq, tm, True), tm=tm, tn=512, out_dtype=BF16,
             name="proj_ksel", extras=[(tab, tab_spec)])
    vv = _mm(h, w_vv, _epi_plain, tm=tm, tn=512, out_dtype=BF16, name="proj_v")
    cqkv = _mm(h, w_c, _epi_plain, tm=tm, tn=mq + mkv, out_dtype=F32, name="proj_c")
    gk = _mm(h, w_gk, _epi_gate_kr, tm=tm, tn=2 * LANES, out_dtype=BF16, name="proj_gate_kr",
             extras=[(tab, tab_spec)])

    def to16(a, width):
        a = a.reshape(bsz, seq, g_, width)[..., :hd]
        a = jnp.transpose(a, (0, 2, 1, 3))
        return a.reshape(bsz * g_, seq // CMP_STRIDE, CMP_STRIDE * hd)

    k16 = to16(kk[:, :g_ * LANES], LANES)
    v16 = to16(vv[:, 2 * g_ * LANES:2 * g_ * LANES + g_ * hd], hd)
    pe = jnp.pad(cmp_pe.reshape(2, 1, CMP_BLOCK * hd), ((0, 0), (0, SUBLANES - 1), (0, 0))).astype(BF16)
    w2p = jnp.pad(cmp_w2, ((0, 0), (0, 0), (0, LANES - hd))).astype(BF16)
    kc, vc = _compress(k16, v16, cmp_w1.astype(BF16), pe, cmp_b1.reshape(2, 1, CMP_HIDDEN), w2p)
    o_cmp, bias = _cmp_attn(qn, kc, vc, _overlap_t(seq), bsz, seq, tq=min(1024, seq))
    o_sel = _flash(qn, ks, vv, mode="sel", bsz=bsz, seq=seq, nslots=g_ // 2, nr=NSA_R, nh=2,
                   tq=512, tk=512, bias=bias, name="nsa_sel_attn")
    o_win = _flash(qn, kk, vv, mode="win", bsz=bsz, seq=seq, nslots=g_ // 2, nr=NSA_R, nh=2,
                   tq=256, tk=256, k_off=g_, v_off=g_, name="nsa_win_attn")
    o_nsa = _combine(o_cmp, o_sel, o_win, gk, 0)

    qd = MLA_NOPE_DIM + MLA_ROPE_DIM
    w_qup = _slots(mla_w_q_up.astype(BF16), MLA_HEADS, qd)
    kvu = mla_w_kv_up.astype(BF16).reshape(mkv, MLA_HEADS, MLA_NOPE_DIM + MLA_V_DIM)
    w_kup = _slots(kvu[:, :, :MLA_NOPE_DIM].reshape(mkv, -1), MLA_HEADS, MLA_NOPE_DIM)
    w_vup = _slots(kvu[:, :, MLA_NOPE_DIM:].reshape(mkv, -1), MLA_HEADS, MLA_V_DIM)
    gq_spec = pl.BlockSpec((1, mq), lambda i, j: (0, 0))
    gkv_spec = pl.BlockSpec((1, mkv), lambda i, j: (0, 0))
    ckv_spec = pl.BlockSpec((tm, mkv), lambda i, j: (i, mq // mkv))
    q_mla = _mm(cqkv, w_qup, _epi_rope_b_scaled, tm=tm, tn=512,
                out_dtype=BF16, name="mla_q_up", pro=_pro_rms,
                x_spec=pl.BlockSpec((tm, mq), lambda i, j: (i, 0)),
                extras=[(mla_q_norm_g.reshape(1, mq), gq_spec), (tab, tab_spec)])
    k_mla = _mm(cqkv, w_kup, _epi_add_kr, tm=tm, tn=512, out_dtype=BF16, name="mla_k_up",
                pro=_pro_rms, x_spec=ckv_spec,
                extras=[(mla_kv_norm_g.reshape(1, mkv), gkv_spec),
                        (gk, pl.BlockSpec((tm, LANES), lambda i, j: (i, 1)))])
    v_mla = _mm(cqkv, w_vup, _epi_plain, tm=tm, tn=512, out_dtype=BF16, name="mla_v_up",
                pro=_pro_rms, x_spec=ckv_spec,
                extras=[(mla_kv_norm_g.reshape(1, mkv), gkv_spec)])
    o_mla = _flash(q_mla, k_mla, v_mla, mode="causal", bsz=bsz, seq=seq, nslots=MLA_HEADS // 8, nr=1,
                   nh=8, tq=512, tk=512, name="mla_attn")

    return _merge(h, w_gm, o_nsa, w_branch_nsa.astype(BF16), o_mla, w_branch_mla.astype(BF16), tm)


def _peer(h2, x1, post_g, gate, seq, peer_w_q, peer_sub_keys, peer_u, peer_v):
    qry = _mm(h2, peer_w_q.astype(BF16), _epi_plain, tm=min(PROJ_TM, h2.shape[0]), tn=512,
              out_dtype=BF16, name="peer_q")
    sk = peer_sub_keys.astype(BF16).reshape(2 * PEER_HEADS, PEER_N_KEYS, -1)
    tb, tf = _peer_route(qry, sk, PEER_TILE // PEER_N_KEYS)
    return _peer_dense(h2, peer_u.astype(BF16), jnp.transpose(peer_v).astype(BF16), tb, tf,
                       x1, post_g, gate, seq, PEER_TILE)


def _block(x, c, positions, ada_w, ada_b, attn_pre_g, attn_post_g, w_in, cmp_pe, cmp_w1, cmp_b1,
           cmp_w2, mla_q_norm_g, mla_w_q_up, mla_kv_norm_g, mla_w_kv_up, w_branch_nsa, w_branch_mla,
           w_out, ffn_pre_g, ffn_post_g, peer_w_q, peer_sub_keys, peer_u, peer_v):
    bsz, seq, d = x.shape
    x2 = x.reshape(bsz * seq, d)
    mod = _adaln(c, ada_w, ada_b)
    sh_a, sc_a, g_a, sh_f, sc_f, g_f = [mod[:, i * d:(i + 1) * d] for i in range(6)]

    h = _normmod(x2, attn_pre_g, sc_a, sh_a, seq)
    merged = _token_mixer(h, positions, bsz, seq, w_in, cmp_pe, cmp_w1, cmp_b1, cmp_w2, mla_q_norm_g,
                          mla_w_q_up, mla_kv_norm_g, mla_w_kv_up, w_branch_nsa, w_branch_mla)
    x1, h2 = _out_proj(merged, w_out.astype(BF16), x2, attn_post_g, g_a, ffn_pre_g, sc_f, sh_f, seq)

    out = _peer(h2, x1, ffn_post_g, g_f, seq, peer_w_q, peer_sub_keys, peer_u, peer_v)
    return out.reshape(bsz, seq, d)


def kernel(x, c, positions, ada_w, ada_b, attn_pre_g, attn_post_g, w_in, cmp_pe, cmp_w1, cmp_b1,
           cmp_w2, mla_q_norm_g, mla_w_q_up, mla_kv_norm_g, mla_w_kv_up, w_branch_nsa, w_branch_mla,
           w_out, ffn_pre_g, ffn_post_g, peer_w_q, peer_sub_keys, peer_u, peer_v):
    depth = ada_w.shape[0]
    for l in range(depth):
        x = _block(x, c, positions, ada_w[l], ada_b[l], attn_pre_g[l], attn_post_g[l], w_in[l],
                   cmp_pe[l], cmp_w1[l], cmp_b1[l], cmp_w2[l], mla_q_norm_g[l], mla_w_q_up[l],
                   mla_kv_norm_g[l], mla_w_kv_up[l], w_branch_nsa[l], w_branch_mla[l], w_out[l],
                   ffn_pre_g[l], ffn_post_g[l], peer_w_q[l], peer_sub_keys[l], peer_u[l], peer_v[l])
    return x
```

```python
import functools

import numpy as np
import jax
import jax.numpy as jnp
from jax import lax
from jax.experimental import pallas as pl
from jax.experimental.pallas import tpu as pltpu

F32 = jnp.float32
BF16 = jnp.bfloat16
I32 = jnp.int32

LANES = 128
SUBLANES = 8

NSA_HEADS = 16
NSA_GROUPS = 4
NSA_R = NSA_HEADS // NSA_GROUPS
NSA_HEAD_DIM = 64
NSA_ROPE_DIM = NSA_HEAD_DIM // 4
CMP_BLOCK = 32
CMP_STRIDE = 16
CMP_HIDDEN = 256
SEL_BLOCK = 64
SEL_SHIFT = 6
SEL_TOPK = 16
WINDOW = 512
MLA_HEADS = 16
MLA_NOPE_DIM = 64
MLA_ROPE_DIM = 32
MLA_V_DIM = 64
PEER_HEADS = 8
PEER_N_KEYS = 128
PEER_TOPK = 16
ROPE_THETA = 500000.0
NORM_EPS = 1e-6
NEG = -1e30
FORCE_BONUS = 1e4
LOG2E = 1.4426950408889634
MLA_ROPE_OFF = MLA_NOPE_DIM
PROJ_TM = 1024


def _cparams(sem, vmem_mb=None):
    kw = dict(dimension_semantics=sem)
    if vmem_mb is not None:
        kw["vmem_limit_bytes"] = vmem_mb * 1024 * 1024
    return pltpu.CompilerParams(**kw)


def _rms(x, g):
    return x * lax.rsqrt(jnp.mean(x * x, axis=-1, keepdims=True) + NORM_EPS) * g


def _gelu(x):
    return jax.nn.gelu(x, approximate=True)


def _adaln_kernel(c_ref, w_ref, b_ref, o_ref):
    c = c_ref[...]
    s = (c * jax.nn.sigmoid(c)).astype(BF16)
    o_ref[...] = jnp.dot(s, w_ref[...].astype(BF16), preferred_element_type=F32) + b_ref[...]


def _adaln(c, w, b, tn=1024):
    bsz, d = c.shape
    n = w.shape[1]
    cp = jnp.pad(c, ((0, SUBLANES - bsz), (0, 0)))
    out = pl.pallas_call(
        _adaln_kernel,
        grid=(n // tn,),
        in_specs=[pl.BlockSpec((SUBLANES, d), lambda j: (0, 0)),
                  pl.BlockSpec((d, tn), lambda j: (0, j)),
                  pl.BlockSpec((1, tn), lambda j: (0, j))],
        out_specs=pl.BlockSpec((SUBLANES, tn), lambda j: (0, j)),
        out_shape=jax.ShapeDtypeStruct((SUBLANES, n), F32),
        compiler_params=_cparams(("arbitrary",), 40),
        name="adaln",
    )(cp, w, b.reshape(1, n))
    return out[:bsz]


def _rope_table_kernel(pos_ref, inv_ref, sgn_ref, o_ref):
    pos = pos_ref[...]
    for k in range(2):
        ang = pos * inv_ref[k:k + 1, :]
        o_ref[:, (2 * k) * LANES:(2 * k + 1) * LANES] = jnp.cos(ang)
        o_ref[:, (2 * k + 1) * LANES:(2 * k + 2) * LANES] = jnp.sin(ang) * sgn_ref[k:k + 1, :]


def _rope_tables(positions, tm=512):
    n = positions.size
    half_a = NSA_ROPE_DIM // 2
    half_b = MLA_ROPE_DIM // 2
    inv_a = ROPE_THETA ** (-jnp.arange(half_a, dtype=F32) * (2.0 / NSA_ROPE_DIM))
    inv_b = ROPE_THETA ** (-jnp.arange(half_b, dtype=F32) * (2.0 / MLA_ROPE_DIM))
    inv = jnp.zeros((2, LANES), F32)
    inv = inv.at[0, 0:half_a].set(inv_a).at[0, half_a:2 * half_a].set(inv_a)
    o = MLA_ROPE_OFF
    inv = inv.at[1, o:o + half_b].set(inv_b).at[1, o + half_b:o + 2 * half_b].set(inv_b)
    sgn = np.zeros((2, LANES), np.float32)
    sgn[0, 0:half_a] = -1.0
    sgn[0, half_a:2 * half_a] = 1.0
    sgn[1, o:o + half_b] = -1.0
    sgn[1, o + half_b:o + 2 * half_b] = 1.0
    posf = jnp.broadcast_to(positions.reshape(n, 1).astype(F32), (n, LANES))
    return pl.pallas_call(
        _rope_table_kernel,
        grid=(n // tm,),
        in_specs=[pl.BlockSpec((tm, LANES), lambda i: (i, 0)),
                  pl.BlockSpec((2, LANES), lambda i: (0, 0)),
                  pl.BlockSpec((2, LANES), lambda i: (0, 0))],
        out_specs=pl.BlockSpec((tm, 4 * LANES), lambda i: (i, 0)),
        out_shape=jax.ShapeDtypeStruct((n, 4 * LANES), F32),
        compiler_params=_cparams(("arbitrary",)),
        name="rope_tables",
    )(posf, inv, jnp.asarray(sgn))


def _rope_slot(a, cos, sin, half, off):
    lane = lax.broadcasted_iota(I32, (1, LANES), 1)
    first = (lane - off) < half
    partner = jnp.where(first, pltpu.roll(a, LANES - half, 1), pltpu.roll(a, half, 1))
    return a * cos + partner * sin


def _normmod_kernel(x_ref, g_ref, sc_ref, sh_ref, o_ref):
    y = _rms(x_ref[...], g_ref[...])
    o_ref[...] = (y * (1.0 + sc_ref[0]) + sh_ref[0]).astype(o_ref.dtype)


def _normmod(x, g, sc, sh, seq, tm=512):
    n, d = x.shape
    bsz = sc.shape[0]
    bspec = pl.BlockSpec((1, 1, d), lambda i: ((i * tm) // seq, 0, 0))
    return pl.pallas_call(
        _normmod_kernel,
        grid=(n // tm,),
        in_specs=[pl.BlockSpec((tm, d), lambda i: (i, 0)),
                  pl.BlockSpec((1, d), lambda i: (0, 0)), bspec, bspec],
        out_specs=pl.BlockSpec((tm, d), lambda i: (i, 0)),
        out_shape=jax.ShapeDtypeStruct((n, d), BF16),
        compiler_params=_cparams(("arbitrary",)),
        name="normmod",
    )(x, g.reshape(1, d), sc.reshape(bsz, 1, d), sh.reshape(bsz, 1, d))


def _out_proj_kernel(m_ref, w_ref, x_ref, pg_ref, ga_ref, fg_ref, sc_ref, sh_ref, x1_ref, h2_ref):
    y = jnp.dot(m_ref[...], w_ref[...], preferred_element_type=F32)
    x1 = x_ref[...] + ga_ref[0] * _rms(y, pg_ref[...])
    x1_ref[...] = x1
    h2_ref[...] = (_rms(x1, fg_ref[...]) * (1.0 + sc_ref[0]) + sh_ref[0]).astype(h2_ref.dtype)


def _out_proj(merged, w, x, post_g, gate, pre_g, sc, sh, seq, tm=512):
    n, d = x.shape
    bsz = gate.shape[0]
    row = lambda width: pl.BlockSpec((tm, width), lambda i: (i, 0))
    vec = pl.BlockSpec((1, d), lambda i: (0, 0))
    bvec = pl.BlockSpec((1, 1, d), lambda i: ((i * tm) // seq, 0, 0))
    return pl.pallas_call(
        _out_proj_kernel,
        grid=(n // tm,),
        in_specs=[row(merged.shape[1]), pl.BlockSpec(w.shape, lambda i: (0, 0)), row(d), vec, bvec,
                  vec, bvec, bvec],
        out_specs=[row(d), row(d)],
        out_shape=[jax.ShapeDtypeStruct((n, d), F32), jax.ShapeDtypeStruct((n, d), BF16)],
        compiler_params=_cparams(("arbitrary",), 56),
        name="out_proj",
    )(merged, w, x, post_g.reshape(1, d), gate.reshape(bsz, 1, d), pre_g.reshape(1, d),
      sc.reshape(bsz, 1, d), sh.reshape(bsz, 1, d))


def _mm(x, w, epi, *, tm, tn, out_dtype, name, extras=(), pro=None, x_spec=None, vmem_mb=None):
    m = x.shape[0]
    k, nc = w.shape
    if x_spec is None:
        x_spec = pl.BlockSpec((tm, k), lambda i, j: (i, 0))
    in_specs = [x_spec, pl.BlockSpec((k, tn), lambda i, j: (0, j))]
    args = [x, w]
    for arr, spec in extras:
        in_specs.append(spec)
        args.append(arr)

    def kern(x_ref, w_ref, *rest):
        o_ref = rest[-1]
        ex = rest[:-1]
        xv = x_ref[...]
        if pro is not None:
            xv = pro(xv, *ex)
        acc = jnp.dot(xv, w_ref[...], preferred_element_type=F32)
        epi(acc, o_ref, *ex)

    return pl.pallas_call(
        kern,
        grid=(m // tm, nc // tn),
        in_specs=in_specs,
        out_specs=pl.BlockSpec((tm, tn), lambda i, j: (i, j)),
        out_shape=jax.ShapeDtypeStruct((m, nc), out_dtype),
        compiler_params=_cparams(("arbitrary", "arbitrary"), vmem_mb),
        name=name,
    )(*args)


def _epi_plain(acc, o_ref, *ex):
    o_ref[...] = acc.astype(o_ref.dtype)


def _make_epi_rope_a(scale, seq=None, tm=None, onehot=False):
    half = NSA_ROPE_DIM // 2

    def epi(acc, o_ref, tab_ref):
        cos = tab_ref[:, 0:LANES]
        sin = tab_ref[:, LANES:2 * LANES]
        if onehot:
            base = lax.rem(pl.program_id(0) * tm, seq)
            t = base + lax.broadcasted_iota(I32, (acc.shape[0], LANES), 0)
            lane = lax.broadcasted_iota(I32, (acc.shape[0], LANES), 1)
            hot = jnp.where(lane - SEL_BLOCK == jnp.right_shift(t, SEL_SHIFT), 1.0, 0.0)
        for s in range(acc.shape[1] // LANES):
            r = _rope_slot(acc[:, s * LANES:(s + 1) * LANES], cos, sin, half, 0)
            if scale != 1.0:
                r = r * scale
            if onehot:
                r = r + hot
            o_ref[:, s * LANES:(s + 1) * LANES] = r.astype(o_ref.dtype)
    return epi


def _epi_rope_b_scaled(acc, o_ref, g_ref, tab_ref):
    cos = tab_ref[:, 2 * LANES:3 * LANES]
    sin = tab_ref[:, 3 * LANES:4 * LANES]
    scale = (MLA_NOPE_DIM + MLA_ROPE_DIM) ** -0.5 * LOG2E
    for s in range(acc.shape[1] // LANES):
        r = _rope_slot(acc[:, s * LANES:(s + 1) * LANES], cos, sin, MLA_ROPE_DIM // 2, MLA_ROPE_OFF)
        o_ref[:, s * LANES:(s + 1) * LANES] = (r * scale).astype(o_ref.dtype)


def _epi_gate_kr(acc, o_ref, tab_ref):
    o_ref[:, 0:LANES] = jax.nn.sigmoid(acc[:, 0:LANES]).astype(o_ref.dtype)
    cos = tab_ref[:, 2 * LANES:3 * LANES]
    sin = tab_ref[:, 3 * LANES:4 * LANES]
    r = _rope_slot(acc[:, LANES:2 * LANES], cos, sin, MLA_ROPE_DIM // 2, MLA_ROPE_OFF)
    o_ref[:, LANES:2 * LANES] = r.astype(o_ref.dtype)


def _pro_rms(xv, g_ref, *ex):
    return _rms(xv, g_ref[...]).astype(BF16)


def _epi_add_kr(acc, o_ref, g_ref, kr_ref):
    kr = kr_ref[...].astype(F32)
    for s in range(acc.shape[1] // LANES):
        o_ref[:, s * LANES:(s + 1) * LANES] = (acc[:, s * LANES:(s + 1) * LANES] + kr).astype(o_ref.dtype)


def _compress_kernel(k16_ref, v16_ref, w1_ref, pe_ref, b1_ref, w2_ref, kc_ref, vc_ref):
    half = w1_ref.shape[1] // 2
    for which, (src, dst) in enumerate(((k16_ref, kc_ref), (v16_ref, vc_ref))):
        xb = src[...]
        a = jnp.dot(xb, w1_ref[which, 0:half, :], preferred_element_type=F32)
        b = jnp.dot(xb, w1_ref[which, half:2 * half, :], preferred_element_type=F32)
        c = jnp.dot(pe_ref[which], w1_ref[which], preferred_element_type=F32)[0:1, :] + b1_ref[which]
        rows = a.shape[0]
        pre = a + pltpu.roll(b, rows - 1, 0) + c
        hid = _gelu(pre).astype(BF16)
        dst[...] = jnp.dot(hid, w2_ref[which], preferred_element_type=F32).astype(dst.dtype)


def _compress(k16, v16, w1, pe, b1, w2p):
    bg, rows, feat = k16.shape
    blk = pl.BlockSpec((None, rows, feat), lambda i: (i, 0, 0))
    full = lambda a: pl.BlockSpec(a.shape, lambda i: (0,) * a.ndim)
    out = pl.BlockSpec((None, rows, LANES), lambda i: (i, 0, 0))
    return pl.pallas_call(
        _compress_kernel,
        grid=(bg,),
        in_specs=[blk, blk, full(w1), full(pe), full(b1), full(w2p)],
        out_specs=[out, out],
        out_shape=[jax.ShapeDtypeStruct((bg, rows, LANES), BF16)] * 2,
        compiler_params=_cparams(("arbitrary",)),
        name="nsa_compress",
    )(k16, v16, w1, pe, b1, w2p)


def _row_bcast(col, width):
    if width % LANES:
        return jnp.broadcast_to(col, (col.shape[0], width))
    tile = jnp.broadcast_to(col, (col.shape[0], LANES))
    return jnp.concatenate([tile] * (width // LANES), axis=1)


def _cmp_attn_kernel(q_ref, kc_ref, vc_ref, ovt_ref, o_ref, bias_ref, *, tq, n_sel):
    qi = pl.program_id(2)
    ncmp = kc_ref.shape[0]
    t_row = qi * tq + lax.broadcasted_iota(I32, (tq, ncmp), 0)
    n_col = lax.broadcasted_iota(I32, (tq, ncmp), 1)
    cmask = (n_col * CMP_STRIDE + (CMP_BLOCK - 1)) <= t_row
    kc = kc_ref[...]
    vc = vc_ref[...]
    imp_t = jnp.zeros((n_sel, tq), F32)
    outs = []
    for r in range(NSA_R):
        q = q_ref[:, r * LANES:(r + 1) * LANES]
        s = lax.dot_general(q, kc, (((1,), (1,)), ((), ())), preferred_element_type=F32)
        s = jnp.where(cmask, s, NEG)
        e = jnp.exp2(s - _row_bcast(jnp.max(s, axis=1, keepdims=True), ncmp))
        p = e / _row_bcast(jnp.sum(e, axis=1, keepdims=True), ncmp)
        p = jnp.where(cmask, p, 0.0).astype(BF16)
        outs.append(jnp.dot(p, vc, preferred_element_type=F32))
        imp_t = imp_t + lax.dot_general(ovt_ref[...], p, (((1,), (1,)), ((), ())),
                                        preferred_element_type=F32)
    _store_heads_compact(o_ref, outs)
    blk = lax.broadcasted_iota(I32, (n_sel, tq), 0)
    t = qi * tq + lax.broadcasted_iota(I32, (n_sel, tq), 1)
    tb = jnp.right_shift(t, SEL_SHIFT)
    forced = (blk == 0) | (blk == tb) | (blk == tb - 1)
    valid = blk * SEL_BLOCK <= t
    x = jnp.where(valid, imp_t + jnp.where(forced, FORCE_BONUS, 0.0), NEG)
    sel = jnp.zeros((n_sel, tq), F32)
    for _ in range(min(SEL_TOPK, n_sel)):
        m = jnp.max(x, axis=0, keepdims=True)
        idx = jnp.min(jnp.where(x == m, blk, n_sel), axis=0, keepdims=True)
        hit = blk == idx
        sel = jnp.where(hit, 1.0, sel)
        x = jnp.where(hit, -jnp.inf, x)
    bias_t = jnp.where(sel > 0.5, 0.0, NEG)
    parts = [jnp.zeros((SEL_BLOCK, tq), F32), bias_t]
    if LANES - SEL_BLOCK - n_sel > 0:
        parts.append(jnp.zeros((LANES - SEL_BLOCK - n_sel, tq), F32))
    full_t = jnp.concatenate(parts, axis=0)
    bias_ref[...] = full_t.T.astype(bias_ref.dtype)


def _cmp_attn(qn, kc, vc, ovt, bsz, seq, tq=256):
    n = qn.shape[0]
    nq = seq // tq
    n_sel = seq // SEL_BLOCK
    ncmp = kc.shape[1]
    qspec = pl.BlockSpec((tq, NSA_R * LANES), lambda b, g, qi: (b * nq + qi, g))
    kspec = pl.BlockSpec((None, ncmp, LANES), lambda b, g, qi: (b * NSA_GROUPS + g, 0, 0))
    return pl.pallas_call(
        functools.partial(_cmp_attn_kernel, tq=tq, n_sel=n_sel),
        grid=(bsz, NSA_GROUPS, nq),
        in_specs=[qspec, kspec, kspec, pl.BlockSpec(ovt.shape, lambda b, g, qi: (0, 0))],
        out_specs=[pl.BlockSpec((tq, NSA_R * NSA_HEAD_DIM), lambda b, g, qi: (b * nq + qi, g)),
                   pl.BlockSpec((tq, LANES), lambda b, g, qi: (b * nq + qi, g))],
        out_shape=[jax.ShapeDtypeStruct((n, NSA_HEADS * NSA_HEAD_DIM), BF16),
                   jax.ShapeDtypeStruct((n, NSA_GROUPS * LANES), BF16)],
        compiler_params=_cparams(("arbitrary",) * 3),
        name="nsa_cmp_attn",
    )(qn, kc, vc, ovt)


def _store_heads_compact(o_ref, heads):
    hd = NSA_HEAD_DIM
    for p in range(len(heads) // 2):
        pair = jnp.concatenate([heads[2 * p][:, 0:hd], heads[2 * p + 1][:, 0:hd]], axis=1)
        o_ref[:, p * LANES:(p + 1) * LANES] = pair.astype(o_ref.dtype)


def _tri_step(p, s, nq):
    first = s <= p
    return jnp.where(first, p, nq - 1 - p), jnp.where(first, s, s - p - 1)


def _flash_kernel(*refs, mode, tq, tk, nr, nh, window, nq):
    if mode == "sel":
        q_ref, k_ref, v_ref, bias_ref, o_ref, qs, m_s, l_s, acc_s = refs
    else:
        q_ref, k_ref, v_ref, o_ref, qs, m_s, l_s, acc_s = refs
    if mode == "win":
        qi = pl.program_id(2)
        kk = pl.program_id(3)
        last = kk == pl.num_programs(3) - 1
    else:
        qi, kk = _tri_step(pl.program_id(2), pl.program_id(3), nq)
        last = kk == qi
    rows = nr * tq

    @pl.when(kk == 0)
    def _init():
        for r in range(nh * nr):
            qr = q_ref[:, r * LANES:(r + 1) * LANES]
            if mode == "sel":
                qr = qr + bias_ref[:, (r // nr) * LANES:(r // nr + 1) * LANES]
            qs[r * tq:(r + 1) * tq, :] = qr
        m_s[...] = jnp.full(m_s.shape, -jnp.inf, F32)
        l_s[...] = jnp.zeros(l_s.shape, F32)
        acc_s[...] = jnp.zeros(acc_s.shape, F32)

    if mode == "win":
        kidx = qi * (tq // tk) - window // tk + kk
        needed = kidx >= 0
        full_vis = (kidx * tk + tk - 1 <= qi * tq) & (qi * tq + tq - 1 - kidx * tk < window)
    else:
        kidx = kk
        needed = True
        full_vis = kk < qi

    nt = tk // LANES

    def update(masked):
        if masked:
            rel = (lax.broadcasted_iota(I32, (rows, tk), 0) & (tq - 1)) - lax.broadcasted_iota(I32, (rows, tk), 1)
            off = kidx * tk - qi * tq
            vis = rel >= off
            if mode == "win":
                vis = vis & (rel < off + window)
        for hh in range(nh):
            rs = slice(hh * rows, (hh + 1) * rows)
            cs = slice(hh * LANES, (hh + 1) * LANES)
            s = lax.dot_general(qs[rs, :], k_ref[:, cs], (((1,), (1,)), ((), ())),
                                preferred_element_type=F32)
            if masked:
                s = jnp.where(vis, s, NEG)
            m_prev = m_s[rs, :]
            m_new = jnp.maximum(m_prev, jnp.max(s, axis=1, keepdims=True))
            p = jnp.exp2(s - jnp.concatenate([m_new] * nt, axis=1))
            alpha = jnp.exp2(m_prev - m_new)
            psum = p[:, 0:LANES]
            for c in range(1, nt):
                psum = psum + p[:, c * LANES:(c + 1) * LANES]
            l_s[rs, :] = alpha * l_s[rs, :] + psum
            acc_s[rs, :] = alpha * acc_s[rs, :] + jnp.dot(p.astype(BF16), v_ref[:, cs],
                                                          preferred_element_type=F32)
            m_s[rs, :] = m_new

    @pl.when(needed & full_vis)
    def _full():
        update(False)

    @pl.when(needed & jnp.logical_not(full_vis))
    def _edge():
        update(True)

    @pl.when(last)
    def _fin():
        out = acc_s[...] / jnp.sum(l_s[...], axis=1, keepdims=True)
        _store_heads_compact(o_ref, [out[r * tq:(r + 1) * tq, :] for r in range(nh * nr)])


def _flash(q, k, v, *, mode, bsz, seq, nslots, nr, tq, tk, nh=1, k_off=0, v_off=0, bias=None,
           name):
    assert tq & (tq - 1) == 0
    n = q.shape[0]
    nq = seq // tq
    nkb = seq // tk
    if mode == "win":
        assert tq % tk == 0 and WINDOW % tk == 0
        grid = (bsz, nslots, nq, WINDOW // tk + tq // tk)

        def qrow(b, p, s):
            return b * nq + p

        def krow(b, p, s):
            return b * nkb + jnp.maximum(p * (tq // tk) - WINDOW // tk + s, 0)
    else:
        assert tq == tk
        grid = (bsz, nslots, (nq + 1) // 2, nq + 1)

        def qrow(b, p, s):
            return b * nq + _tri_step(p, s, nq)[0]

        def krow(b, p, s):
            return b * nkb + _tri_step(p, s, nq)[1]

    qspec = pl.BlockSpec((tq, nh * nr * LANES), lambda b, g, p, s: (qrow(b, p, s), g))
    in_specs = [qspec,
                pl.BlockSpec((tk, nh * LANES), lambda b, g, p, s: (krow(b, p, s), k_off // nh + g)),
                pl.BlockSpec((tk, nh * LANES), lambda b, g, p, s: (krow(b, p, s), v_off // nh + g))]
    args = [q, k, v]
    if mode == "sel":
        in_specs.append(pl.BlockSpec((tq, nh * LANES), lambda b, g, p, s: (qrow(b, p, s), g)))
        args.append(bias)
    rows = nh * nr * tq
    ow = nh * nr * NSA_HEAD_DIM
    return pl.pallas_call(
        functools.partial(_flash_kernel, mode=mode, tq=tq, tk=tk, nr=nr, nh=nh, window=WINDOW, nq=nq),
        grid=grid,
        in_specs=in_specs,
        out_specs=pl.BlockSpec((tq, ow), lambda b, g, p, s: (qrow(b, p, s), g)),
        out_shape=jax.ShapeDtypeStruct((n, nslots * ow), BF16),
        scratch_shapes=[pltpu.VMEM((rows, LANES), BF16), pltpu.VMEM((rows, LANES), F32),
                        pltpu.VMEM((rows, LANES), F32), pltpu.VMEM((rows, LANES), F32)],
        compiler_params=_cparams(("arbitrary",) * 4, 48),
        name=name,
    )(*args)


def _combine_kernel(oc_ref, os_ref, ow_ref, g_ref, e_ref, o_ref):
    g = g_ref[...]
    acc = None
    for br, src in enumerate((oc_ref, os_ref, ow_ref)):
        ge = jnp.dot(g, e_ref[br], preferred_element_type=F32)
        term = ge * src[...].astype(F32)
        acc = term if acc is None else acc + term
    o_ref[...] = acc.astype(o_ref.dtype)


def _combine(o_cmp, o_sel, o_win, gates, gates_col, tm=512):
    n, w = o_cmp.shape
    e = np.zeros((3, LANES, w), np.float32)
    for h in range(NSA_HEADS):
        for br in range(3):
            e[br, h * 3 + br, h * NSA_HEAD_DIM:(h + 1) * NSA_HEAD_DIM] = 1.0
    row = pl.BlockSpec((tm, w), lambda i: (i, 0))
    return pl.pallas_call(
        _combine_kernel,
        grid=(n // tm,),
        in_specs=[row, row, row, pl.BlockSpec((tm, LANES), lambda i: (i, gates_col)),
                  pl.BlockSpec(e.shape, lambda i: (0, 0, 0))],
        out_specs=row,
        out_shape=jax.ShapeDtypeStruct((n, w), BF16),
        compiler_params=_cparams(("arbitrary",)),
        name="nsa_combine",
    )(o_cmp, o_sel, o_win, gates, jnp.asarray(e, BF16))


def _merge_kernel(h_ref, wg0_ref, wg1_ref, a_ref, wa_ref, b_ref, wb_ref, o_ref):
    h = h_ref[...]
    g0 = jax.nn.sigmoid(jnp.dot(h, wg0_ref[...], preferred_element_type=F32))
    g1 = jax.nn.sigmoid(jnp.dot(h, wg1_ref[...], preferred_element_type=F32))
    ya = jnp.dot(a_ref[...], wa_ref[...], preferred_element_type=F32)
    yb = jnp.dot(b_ref[...], wb_ref[...], preferred_element_type=F32)
    o_ref[...] = (g0 * ya + g1 * yb).astype(o_ref.dtype)


def _merge(h, w_gm, o_nsa, w_nsa, o_mla, w_mla, tm, tn=512):
    n, k = o_nsa.shape
    kh = h.shape[1]
    d = w_nsa.shape[1]
    nj = d // tn
    row = pl.BlockSpec((tm, k), lambda i, j: (i, 0))
    wsp = pl.BlockSpec((k, tn), lambda i, j: (0, j))
    return pl.pallas_call(
        _merge_kernel,
        grid=(n // tm, nj),
        in_specs=[pl.BlockSpec((tm, kh), lambda i, j: (i, 0)),
                  pl.BlockSpec((kh, tn), lambda i, j: (0, j)),
                  pl.BlockSpec((kh, tn), lambda i, j: (0, nj + j)),
                  row, wsp, row, wsp],
        out_specs=pl.BlockSpec((tm, tn), lambda i, j: (i, j)),
        out_shape=jax.ShapeDtypeStruct((n, d), BF16),
        compiler_params=_cparams(("arbitrary", "arbitrary"), 48),
        name="branch_merge",
    )(h, w_gm, w_gm, o_nsa, w_nsa, o_mla, w_mla)


def _merge_desc(xs):
    xs = list(xs)
    n = len(xs)
    stride = n // 2
    while stride >= 1:
        for i in range(n):
            p = i ^ stride
            if p > i:
                xs[i], xs[p] = jnp.maximum(xs[i], xs[p]), jnp.minimum(xs[i], xs[p])
        stride //= 2
    return xs


def _top16_sorted(xs):
    xs = list(xs)
    n = len(xs)
    size = 2
    while size <= n:
        stride = size // 2
        while stride >= 1:
            for i in range(n):
                p = i ^ stride
                if p > i:
                    hi, lo = jnp.maximum(xs[i], xs[p]), jnp.minimum(xs[i], xs[p])
                    xs[i], xs[p] = (hi, lo) if (i & size) == 0 else (lo, hi)
            stride //= 2
        size *= 2
    shift = SUBLANES // 2
    while shift >= 1:
        ys = [pltpu.roll(x, shift, 0) for x in xs]
        xs = _merge_desc([jnp.maximum(xs[k], ys[n - 1 - k]) for k in range(n)])
        shift //= 2
    return xs


def _prefix_len(pred, vs):
    p8 = pred(vs[7])
    p4 = pred(jnp.where(p8, vs[11], vs[3]))
    p2 = pred(jnp.where(p8, jnp.where(p4, vs[13], vs[9]), jnp.where(p4, vs[5], vs[1])))
    q = [jnp.where(p2, vs[4 * i + 2], vs[4 * i]) for i in range(4)]
    p1 = pred(jnp.where(p8, jnp.where(p4, q[3], q[2]), jnp.where(p4, q[1], q[0])))
    n = (jnp.where(p8, 8.0, 0.0) + jnp.where(p4, 4.0, 0.0)) + (jnp.where(p2, 2.0, 0.0) + jnp.where(p1, 1.0, 0.0))
    return jnp.where(pred(vs[15]), 16.0, n)


def _rows8(x):
    return [x[k * SUBLANES:(k + 1) * SUBLANES, :] for k in range(x.shape[0] // SUBLANES)]


def _peer_route_kernel(q_ref, sk_ref, tb_ref, tf_ref):
    nk = PEER_N_KEYS
    sec = PEER_HEADS * nk
    tr = q_ref.shape[0]
    sub = lax.broadcasted_iota(I32, (SUBLANES, tr), 0)
    ninf = jnp.full((SUBLANES, tr), -jnp.inf, F32)

    def spread(vs):
        out = vs[SUBLANES - 1]
        for j in range(SUBLANES - 2, -1, -1):
            out = jnp.where(sub == j, vs[j], out)
        return out

    for h in range(PEER_HEADS):
        s = []
        for p in range(2):
            c = (2 * h + p) * nk
            s.append(lax.dot_general(sk_ref[2 * h + p], q_ref[:, c:c + nk], (((1,), (1,)), ((), ())),
                                     preferred_element_type=F32))
        x1, x2 = _rows8(s[0]), _rows8(s[1])
        v1, v2 = _top16_sorted(x1), _top16_sorted(x2)
        v2lo, v2hi = spread(v2[:SUBLANES]), spread(v2[SUBLANES:])
        cands = [v1[0] + v2lo, v1[0] + v2hi]
        for i in range(1, SUBLANES):
            c = v1[i] + v2lo
            cnt = PEER_TOPK // (i + 1)
            cands.append(jnp.where(sub < cnt, c, -jnp.inf) if cnt < SUBLANES else c)
        cands.append(spread(v1[SUBLANES:]) + v2[0])
        top = _top16_sorted(cands + [ninf] * (PEER_TOPK - len(cands)))
        tau = top[PEER_TOPK - 1]
        z = None
        for k in range(PEER_TOPK):
            ek = jnp.exp(top[k] - top[0])
            z = ek if z is None else z + ek
        rank2 = jnp.concatenate([_prefix_len(lambda v, x=x: v > x, v2) for x in x2], axis=0)
        count = jnp.concatenate([_prefix_len(lambda v, x=x: (x + v) >= tau, v2) for x in x1], axis=0)
        tb_ref[h * nk:(h + 1) * nk, :] = rank2.astype(tb_ref.dtype)
        tb_ref[sec + h * nk:sec + (h + 1) * nk, :] = jnp.exp(s[1] - v2[0][0:1, :]).astype(tb_ref.dtype)
        apb = tf_ref.shape[1] // (2 * PEER_HEADS)
        for k, vals in enumerate((count, jnp.exp(s[0] - v1[0][0:1, :]) / z[0:1, :])):
            r0 = (k * PEER_HEADS + h) * apb
            tf_ref[:, r0:r0 + apb, :] = vals.reshape(nk // apb, apb, tr)


def _peer_route(qry, sk, apb, tr=256):
    n, w = qry.shape
    nj = PEER_N_KEYS // apb
    rows = 2 * PEER_HEADS * apb
    return pl.pallas_call(
        _peer_route_kernel,
        grid=(n // tr,),
        in_specs=[pl.BlockSpec((tr, w), lambda i: (i, 0)),
                  pl.BlockSpec(sk.shape, lambda i: (0, 0, 0))],
        out_specs=[pl.BlockSpec((w, tr), lambda i: (0, i)),
                   pl.BlockSpec((nj, rows, tr), lambda i: (0, 0, i))],
        out_shape=[jax.ShapeDtypeStruct((w, n), BF16), jax.ShapeDtypeStruct((nj, rows, n), F32)],
        compiler_params=_cparams(("arbitrary",), 40),
        name="peer_route",
    )(qry, sk)


PEER_CHUNK = 32
PEER_TILE = 1024


def _peer_dense_kernel(h_ref, u_ref, vt_ref, tb_ref, tf_ref, x_ref, pg_ref, gf_ref, o_ref, acc_s, pt_s,
                       *, te):
    j = pl.program_id(1)
    nk = PEER_N_KEYS
    sec = PEER_HEADS * nk
    apb = te // nk
    gdt = pt_s.dtype

    @pl.when(j == 0)
    def _init():
        acc_s[...] = jnp.zeros(acc_s.shape, F32)

    def row_tile(row):
        r = tf_ref[row:row + 1, :]
        return jnp.broadcast_to(r.astype(gdt), (PEER_CHUNK, r.shape[1]))

    zt = lax.dot_general(u_ref[...], h_ref[...], (((1,), (1,)), ((), ())), preferred_element_type=F32)
    for al in range(apb):
        cnt = [row_tile(h * apb + al) for h in range(PEER_HEADS)]
        e1 = [row_tile((PEER_HEADS + h) * apb + al) for h in range(PEER_HEADS)]
        for c in range(nk // PEER_CHUNK):
            lo = c * PEER_CHUNK
            g = None
            for h in range(PEER_HEADS):
                rank2 = tb_ref[h * nk + lo:h * nk + lo + PEER_CHUNK, :]
                e2 = tb_ref[sec + h * nk + lo:sec + h * nk + lo + PEER_CHUNK, :]
                term = jnp.where(rank2 < cnt[h], e1[h] * e2, jnp.zeros((), gdt))
                g = term if g is None else g + term
            r0 = al * nk + lo
            pt_s[r0:r0 + PEER_CHUNK, :] = g * _gelu(zt[r0:r0 + PEER_CHUNK, :]).astype(gdt)
    acc_s[...] += jnp.dot(vt_ref[...], pt_s[...], preferred_element_type=F32)

    @pl.when(j == pl.num_programs(1) - 1)
    def _fin():
        o_ref[...] = x_ref[...] + gf_ref[0] * _rms(acc_s[...].T, pg_ref[...])


def _peer_dense(h2, u_bf, vt_bf, tb, tf, x1, post_g, gate, seq, te, tm=512):
    n, d = h2.shape
    ne = u_bf.shape[0]
    w = tb.shape[0]
    bsz = gate.shape[0]
    row = pl.BlockSpec((tm, d), lambda i, j: (i, 0))
    return pl.pallas_call(
        functools.partial(_peer_dense_kernel, te=te),
        grid=(n // tm, ne // te),
        in_specs=[row,
                  pl.BlockSpec((te, d), lambda i, j: (j, 0)),
                  pl.BlockSpec((d, te), lambda i, j: (0, j)),
                  pl.BlockSpec((w, tm), lambda i, j: (0, i)),
                  pl.BlockSpec((None, tf.shape[1], tm), lambda i, j: (j, 0, i)),
                  row,
                  pl.BlockSpec((1, d), lambda i, j: (0, 0)),
                  pl.BlockSpec((1, 1, d), lambda i, j: ((i * tm) // seq, 0, 0))],
        out_specs=row,
        out_shape=jax.ShapeDtypeStruct((n, d), F32),
        scratch_shapes=[pltpu.VMEM((d, tm), F32), pltpu.VMEM((te, tm), BF16)],
        compiler_params=_cparams(("arbitrary", "arbitrary"), 56),
        name="peer_dense",
    )(h2, u_bf, vt_bf, tb, tf, x1, post_g.reshape(1, d), gate.reshape(bsz, 1, d))


def _slots(w, n, width, off=0):
    k = w.shape[0]
    w = w.reshape(k, n, width)
    w = jnp.pad(w, ((0, 0), (0, 0), (off, LANES - width - off)))
    return w.reshape(k, n * LANES)


def _overlap_t(seq):
    n_cmp_rows = seq // CMP_STRIDE
    n_sel = seq // SEL_BLOCK
    cs = np.arange(n_cmp_rows) * CMP_STRIDE
    ce = cs + CMP_BLOCK - 1
    ss = np.arange(n_sel) * SEL_BLOCK
    ov = (cs[None, :] < ss[:, None] + SEL_BLOCK) & (ce[None, :] >= ss[:, None])
    ov[:, n_cmp_rows - 1] = False
    return jnp.asarray(ov.astype(np.float32), BF16)


def _token_mixer(h, positions, bsz, seq, w_in, cmp_pe, cmp_w1, cmp_b1, cmp_w2, mla_q_norm_g,
                 mla_w_q_up, mla_kv_norm_g, mla_w_kv_up, w_branch_nsa, w_branch_mla):
    d = h.shape[1]
    g_, hd = NSA_GROUPS, NSA_HEAD_DIM
    tab = _rope_tables(positions)
    tm = min(PROJ_TM, seq)
    tab_spec = pl.BlockSpec((tm, 4 * LANES), lambda i, j: (i, 0))

    wb = w_in.astype(BF16)
    o1 = NSA_HEADS * hd
    kv = wb[:, o1:o1 + 3 * 2 * g_ * hd].reshape(d, 3, 2, g_ * hd)
    o2 = o1 + 3 * 2 * g_ * hd
    o3 = o2 + 3 * NSA_HEADS
    mq = mla_w_q_up.shape[0]
    mkv = mla_w_kv_up.shape[0]
    o4 = o3 + mq
    o5 = o4 + mkv
    o6 = o5 + MLA_ROPE_DIM
    w_q = _slots(wb[:, :o1], NSA_HEADS, hd)
    w_kk = jnp.concatenate([_slots(kv[:, 0, 0], g_, hd), _slots(kv[:, 2, 0], g_, hd)], axis=1)
    w_ks = _slots(kv[:, 1, 0], g_, hd)
    w_vv = jnp.concatenate([_slots(kv[:, 1, 1], g_, hd), _slots(kv[:, 2, 1], g_, hd),
                            kv[:, 0, 1], jnp.zeros((d, g_ * hd), BF16)], axis=1)
    w_c = wb[:, o3:o5]
    w_gk = jnp.concatenate([jnp.pad(wb[:, o2:o3], ((0, 0), (0, LANES - 3 * NSA_HEADS))),
                            _slots(wb[:, o5:o6], 1, MLA_ROPE_DIM, MLA_ROPE_OFF)], axis=1)
    w_gm = wb[:, o6:]

    qn = _mm(h, w_q, _make_epi_rope_a(hd ** -0.5 * LOG2E), tm=tm, tn=512, out_dtype=BF16, name="proj_q",
             extras=[(tab, tab_spec)])
    kk = _mm(h, w_kk, _make_epi_rope_a(1.0), tm=tm, tn=512, out_dtype=BF16, name="proj_k",
             extras=[(tab, tab_spec)])
    ks = _mm(h, w_ks, _make_epi_rope_a(1.0, seq, tm, True), tm=tm, tn=512, out_dtype=BF16,
             name="proj_ksel", extras=[(tab, tab_spec)])
    vv = _mm(h, w_vv, _epi_plain, tm=tm, tn=512, out_dtype=BF16, name="proj_v")
    cqkv = _mm(h, w_c, _epi_plain, tm=tm, tn=mq + mkv, out_dtype=F32, name="proj_c")
    gk = _mm(h, w_gk, _epi_gate_kr, tm=tm, tn=2 * LANES, out_dtype=BF16, name="proj_gate_kr",
             extras=[(tab, tab_spec)])

    def to16(a, width):
        a = a.reshape(bsz, seq, g_, width)[..., :hd]
        a = jnp.transpose(a, (0, 2, 1, 3))
        return a.reshape(bsz * g_, seq // CMP_STRIDE, CMP_STRIDE * hd)

    k16 = to16(kk[:, :g_ * LANES], LANES)
    v16 = to16(vv[:, 2 * g_ * LANES:2 * g_ * LANES + g_ * hd], hd)
    pe = jnp.pad(cmp_pe.reshape(2, 1, CMP_BLOCK * hd), ((0, 0), (0, SUBLANES - 1), (0, 0))).astype(BF16)
    w2p = jnp.pad(cmp_w2, ((0, 0), (0, 0), (0, LANES - hd))).astype(BF16)
    kc, vc = _compress(k16, v16, cmp_w1.astype(BF16), pe, cmp_b1.reshape(2, 1, CMP_HIDDEN), w2p)
    o_cmp, bias = _cmp_attn(qn, kc, vc, _overlap_t(seq), bsz, seq, tq=min(1024, seq))
    o_sel = _flash(qn, ks, vv, mode="sel", bsz=bsz, seq=seq, nslots=1, nr=NSA_R, nh=g_,
                   tq=512, tk=512, bias=bias, name="nsa_sel_attn")
    o_win = _flash(qn, kk, vv, mode="win", bsz=bsz, seq=seq, nslots=1, nr=NSA_R, nh=g_,
                   tq=256, tk=256, k_off=g_, v_off=g_, name="nsa_win_attn")
    o_nsa = _combine(o_cmp, o_sel, o_win, gk, 0)

    qd = MLA_NOPE_DIM + MLA_ROPE_DIM
    w_qup = _slots(mla_w_q_up.astype(BF16), MLA_HEADS, qd)
    kvu = mla_w_kv_up.astype(BF16).reshape(mkv, MLA_HEADS, MLA_NOPE_DIM + MLA_V_DIM)
    w_kup = _slots(kvu[:, :, :MLA_NOPE_DIM].reshape(mkv, -1), MLA_HEADS, MLA_NOPE_DIM)
    w_vup = _slots(kvu[:, :, MLA_NOPE_DIM:].reshape(mkv, -1), MLA_HEADS, MLA_V_DIM)
    gq_spec = pl.BlockSpec((1, mq), lambda i, j: (0, 0))
    gkv_spec = pl.BlockSpec((1, mkv), lambda i, j: (0, 0))
    ckv_spec = pl.BlockSpec((tm, mkv), lambda i, j: (i, mq // mkv))
    q_mla = _mm(cqkv, w_qup, _epi_rope_b_scaled, tm=tm, tn=512,
                out_dtype=BF16, name="mla_q_up", pro=_pro_rms,
                x_spec=pl.BlockSpec((tm, mq), lambda i, j: (i, 0)),
                extras=[(mla_q_norm_g.reshape(1, mq), gq_spec), (tab, tab_spec)])
    k_mla = _mm(cqkv, w_kup, _epi_add_kr, tm=tm, tn=512, out_dtype=BF16, name="mla_k_up",
                pro=_pro_rms, x_spec=ckv_spec,
                extras=[(mla_kv_norm_g.reshape(1, mkv), gkv_spec),
                        (gk, pl.BlockSpec((tm, LANES), lambda i, j: (i, 1)))])
    v_mla = _mm(cqkv, w_vup, _epi_plain, tm=tm, tn=512, out_dtype=BF16, name="mla_v_up",
                pro=_pro_rms, x_spec=ckv_spec,
                extras=[(mla_kv_norm_g.reshape(1, mkv), gkv_spec)])
    o_mla = _flash(q_mla, k_mla, v_mla, mode="causal", bsz=bsz, seq=seq, nslots=1, nr=1,
                   nh=MLA_HEADS, tq=512, tk=512, name="mla_attn")

    return _merge(h, w_gm, o_nsa, w_branch_nsa.astype(BF16), o_mla, w_branch_mla.astype(BF16), tm)


def _peer(h2, x1, post_g, gate, seq, peer_w_q, peer_sub_keys, peer_u, peer_v):
    qry = _mm(h2, peer_w_q.astype(BF16), _epi_plain, tm=min(PROJ_TM, h2.shape[0]), tn=512,
              out_dtype=BF16, name="peer_q")
    sk = peer_sub_keys.astype(BF16).reshape(2 * PEER_HEADS, PEER_N_KEYS, -1)
    tb, tf = _peer_route(qry, sk, PEER_TILE // PEER_N_KEYS)
    return _peer_dense(h2, peer_u.astype(BF16), jnp.transpose(peer_v).astype(BF16), tb, tf,
                       x1, post_g, gate, seq, PEER_TILE)


def _block(x, c, positions, ada_w, ada_b, attn_pre_g, attn_post_g, w_in, cmp_pe, cmp_w1, cmp_b1,
           cmp_w2, mla_q_norm_g, mla_w_q_up, mla_kv_norm_g, mla_w_kv_up, w_branch_nsa, w_branch_mla,
           w_out, ffn_pre_g, ffn_post_g, peer_w_q, peer_sub_keys, peer_u, peer_v):
    bsz, seq, d = x.shape
    x2 = x.reshape(bsz * seq, d)
    mod = _adaln(c, ada_w, ada_b)
    sh_a, sc_a, g_a, sh_f, sc_f, g_f = [mod[:, i * d:(i + 1) * d] for i in range(6)]

    h = _normmod(x2, attn_pre_g, sc_a, sh_a, seq)
    merged = _token_mixer(h, positions, bsz, seq, w_in, cmp_pe, cmp_w1, cmp_b1, cmp_w2, mla_q_norm_g,
                          mla_w_q_up, mla_kv_norm_g, mla_w_kv_up, w_branch_nsa, w_branch_mla)
    x1, h2 = _out_proj(merged, w_out.astype(BF16), x2, attn_post_g, g_a, ffn_pre_g, sc_f, sh_f, seq)

    out = _peer(h2, x1, ffn_post_g, g_f, seq, peer_w_q, peer_sub_keys, peer_u, peer_v)
    return out.reshape(bsz, seq, d)


def kernel(x, c, positions, ada_w, ada_b, attn_pre_g, attn_post_g, w_in, cmp_pe, cmp_w1, cmp_b1,
           cmp_w2, mla_q_norm_g, mla_w_q_up, mla_kv_norm_g, mla_w_kv_up, w_branch_nsa, w_branch_mla,
           w_out, ffn_pre_g, ffn_post_g, peer_w_q, peer_sub_keys, peer_u, peer_v):
    depth = ada_w.shape[0]
    for l in range(depth):
        x = _block(x, c, positions, ada_w[l], ada_b[l], attn_pre_g[l], attn_post_g[l], w_in[l],
                   cmp_pe[l], cmp_w1[l], cmp_b1[l], cmp_w2[l], mla_q_norm_g[l], mla_w_q_up[l],
                   mla_kv_norm_g[l], mla_w_kv_up[l], w_branch_nsa[l], w_branch_mla[l], w_out[l],
                   ffn_pre_g[l], ffn_post_g[l], peer_w_q[l], peer_sub_keys[l], peer_u[l], peer_v[l])
    return x
```

```python
import functools

import numpy as np
import jax
import jax.numpy as jnp
from jax import lax
from jax.experimental import pallas as pl
from jax.experimental.pallas import tpu as pltpu

F32 = jnp.float32
BF16 = jnp.bfloat16
I32 = jnp.int32

LANES = 128
SUBLANES = 8

NSA_HEADS = 16
NSA_GROUPS = 4
NSA_R = NSA_HEADS // NSA_GROUPS
NSA_HEAD_DIM = 64
NSA_ROPE_DIM = NSA_HEAD_DIM // 4
CMP_BLOCK = 32
CMP_STRIDE = 16
CMP_HIDDEN = 256
SEL_BLOCK = 64
SEL_SHIFT = 6
SEL_TOPK = 16
WINDOW = 512
MLA_HEADS = 16
MLA_NOPE_DIM = 64
MLA_ROPE_DIM = 32
MLA_V_DIM = 64
PEER_HEADS = 8
PEER_N_KEYS = 128
PEER_TOPK = 16
ROPE_THETA = 500000.0
NORM_EPS = 1e-6
NEG = -1e30
FORCE_BONUS = 1e4
LOG2E = 1.4426950408889634
MLA_ROPE_OFF = MLA_NOPE_DIM
PROJ_TM = 1024


def _cparams(sem, vmem_mb=None):
    kw = dict(dimension_semantics=sem)
    if vmem_mb is not None:
        kw["vmem_limit_bytes"] = vmem_mb * 1024 * 1024
    return pltpu.CompilerParams(**kw)


def _rms(x, g):
    return x * lax.rsqrt(jnp.mean(x * x, axis=-1, keepdims=True) + NORM_EPS) * g


def _gelu(x):
    return jax.nn.gelu(x, approximate=True)


def _adaln_kernel(c_ref, w_ref, b_ref, o_ref):
    c = c_ref[...]
    s = (c * jax.nn.sigmoid(c)).astype(BF16)
    o_ref[...] = jnp.dot(s, w_ref[...].astype(BF16), preferred_element_type=F32) + b_ref[...]


def _adaln(c, w, b, tn=1024):
    bsz, d = c.shape
    n = w.shape[1]
    cp = jnp.pad(c, ((0, SUBLANES - bsz), (0, 0)))
    out = pl.pallas_call(
        _adaln_kernel,
        grid=(n // tn,),
        in_specs=[pl.BlockSpec((SUBLANES, d), lambda j: (0, 0)),
                  pl.BlockSpec((d, tn), lambda j: (0, j)),
                  pl.BlockSpec((1, tn), lambda j: (0, j))],
        out_specs=pl.BlockSpec((SUBLANES, tn), lambda j: (0, j)),
        out_shape=jax.ShapeDtypeStruct((SUBLANES, n), F32),
        compiler_params=_cparams(("arbitrary",), 40),
        name="adaln",
    )(cp, w, b.reshape(1, n))
    return out[:bsz]


def _rope_table_kernel(pos_ref, inv_ref, sgn_ref, o_ref):
    pos = pos_ref[...]
    for k in range(2):
        ang = pos * inv_ref[k:k + 1, :]
        o_ref[:, (2 * k) * LANES:(2 * k + 1) * LANES] = jnp.cos(ang)
        o_ref[:, (2 * k + 1) * LANES:(2 * k + 2) * LANES] = jnp.sin(ang) * sgn_ref[k:k + 1, :]


def _rope_tables(positions, tm=512):
    n = positions.size
    half_a = NSA_ROPE_DIM // 2
    half_b = MLA_ROPE_DIM // 2
    inv_a = ROPE_THETA ** (-jnp.arange(half_a, dtype=F32) * (2.0 / NSA_ROPE_DIM))
    inv_b = ROPE_THETA ** (-jnp.arange(half_b, dtype=F32) * (2.0 / MLA_ROPE_DIM))
    inv = jnp.zeros((2, LANES), F32)
    inv = inv.at[0, 0:half_a].set(inv_a).at[0, half_a:2 * half_a].set(inv_a)
    o = MLA_ROPE_OFF
    inv = inv.at[1, o:o + half_b].set(inv_b).at[1, o + half_b:o + 2 * half_b].set(inv_b)
    sgn = np.zeros((2, LANES), np.float32)
    sgn[0, 0:half_a] = -1.0
    sgn[0, half_a:2 * half_a] = 1.0
    sgn[1, o:o + half_b] = -1.0
    sgn[1, o + half_b:o + 2 * half_b] = 1.0
    posf = jnp.broadcast_to(positions.reshape(n, 1).astype(F32), (n, LANES))
    return pl.pallas_call(
        _rope_table_kernel,
        grid=(n // tm,),
        in_specs=[pl.BlockSpec((tm, LANES), lambda i: (i, 0)),
                  pl.BlockSpec((2, LANES), lambda i: (0, 0)),
                  pl.BlockSpec((2, LANES), lambda i: (0, 0))],
        out_specs=pl.BlockSpec((tm, 4 * LANES), lambda i: (i, 0)),
        out_shape=jax.ShapeDtypeStruct((n, 4 * LANES), F32),
        compiler_params=_cparams(("arbitrary",)),
        name="rope_tables",
    )(posf, inv, jnp.asarray(sgn))


def _rope_slot(a, cos, sin, half, off):
    lane = lax.broadcasted_iota(I32, (1, LANES), 1)
    first = (lane - off) < half
    partner = jnp.where(first, pltpu.roll(a, LANES - half, 1), pltpu.roll(a, half, 1))
    return a * cos + partner * sin


def _normmod_kernel(x_ref, g_ref, sc_ref, sh_ref, o_ref):
    y = _rms(x_ref[...], g_ref[...])
    o_ref[...] = (y * (1.0 + sc_ref[0]) + sh_ref[0]).astype(o_ref.dtype)


def _normmod(x, g, sc, sh, seq, tm=512):
    n, d = x.shape
    bsz = sc.shape[0]
    bspec = pl.BlockSpec((1, 1, d), lambda i: ((i * tm) // seq, 0, 0))
    return pl.pallas_call(
        _normmod_kernel,
        grid=(n // tm,),
        in_specs=[pl.BlockSpec((tm, d), lambda i: (i, 0)),
                  pl.BlockSpec((1, d), lambda i: (0, 0)), bspec, bspec],
        out_specs=pl.BlockSpec((tm, d), lambda i: (i, 0)),
        out_shape=jax.ShapeDtypeStruct((n, d), BF16),
        compiler_params=_cparams(("arbitrary",)),
        name="normmod",
    )(x, g.reshape(1, d), sc.reshape(bsz, 1, d), sh.reshape(bsz, 1, d))


def _out_proj_kernel(m_ref, w_ref, x_ref, pg_ref, ga_ref, fg_ref, sc_ref, sh_ref, x1_ref, h2_ref):
    y = jnp.dot(m_ref[...], w_ref[...], preferred_element_type=F32)
    x1 = x_ref[...] + ga_ref[0] * _rms(y, pg_ref[...])
    x1_ref[...] = x1
    h2_ref[...] = (_rms(x1, fg_ref[...]) * (1.0 + sc_ref[0]) + sh_ref[0]).astype(h2_ref.dtype)


def _out_proj(merged, w, x, post_g, gate, pre_g, sc, sh, seq, tm=512):
    n, d = x.shape
    bsz = gate.shape[0]
    row = lambda width: pl.BlockSpec((tm, width), lambda i: (i, 0))
    vec = pl.BlockSpec((1, d), lambda i: (0, 0))
    bvec = pl.BlockSpec((1, 1, d), lambda i: ((i * tm) // seq, 0, 0))
    return pl.pallas_call(
        _out_proj_kernel,
        grid=(n // tm,),
        in_specs=[row(merged.shape[1]), pl.BlockSpec(w.shape, lambda i: (0, 0)), row(d), vec, bvec,
                  vec, bvec, bvec],
        out_specs=[row(d), row(d)],
        out_shape=[jax.ShapeDtypeStruct((n, d), F32), jax.ShapeDtypeStruct((n, d), BF16)],
        compiler_params=_cparams(("arbitrary",), 56),
        name="out_proj",
    )(merged, w, x, post_g.reshape(1, d), gate.reshape(bsz, 1, d), pre_g.reshape(1, d),
      sc.reshape(bsz, 1, d), sh.reshape(bsz, 1, d))


def _mm(x, w, epi, *, tm, tn, out_dtype, name, extras=(), pro=None, x_spec=None, vmem_mb=None):
    m = x.shape[0]
    k, nc = w.shape
    if x_spec is None:
        x_spec = pl.BlockSpec((tm, k), lambda i, j: (i, 0))
    in_specs = [x_spec, pl.BlockSpec((k, tn), lambda i, j: (0, j))]
    args = [x, w]
    for arr, spec in extras:
        in_specs.append(spec)
        args.append(arr)

    def kern(x_ref, w_ref, *rest):
        o_ref = rest[-1]
        ex = rest[:-1]
        xv = x_ref[...]
        if pro is not None:
            xv = pro(xv, *ex)
        acc = jnp.dot(xv, w_ref[...], preferred_element_type=F32)
        epi(acc, o_ref, *ex)

    return pl.pallas_call(
        kern,
        grid=(m // tm, nc // tn),
        in_specs=in_specs,
        out_specs=pl.BlockSpec((tm, tn), lambda i, j: (i, j)),
        out_shape=jax.ShapeDtypeStruct((m, nc), out_dtype),
        compiler_params=_cparams(("arbitrary", "arbitrary"), vmem_mb),
        name=name,
    )(*args)


def _epi_plain(acc, o_ref, *ex):
    o_ref[...] = acc.astype(o_ref.dtype)


def _make_epi_rope_a(scale, seq=None, tm=None, onehot=False):
    half = NSA_ROPE_DIM // 2

    def epi(acc, o_ref, tab_ref):
        cos = tab_ref[:, 0:LANES]
        sin = tab_ref[:, LANES:2 * LANES]
        if onehot:
            base = lax.rem(pl.program_id(0) * tm, seq)
            t = base + lax.broadcasted_iota(I32, (acc.shape[0], LANES), 0)
            lane = lax.broadcasted_iota(I32, (acc.shape[0], LANES), 1)
            hot = jnp.where(lane - SEL_BLOCK == jnp.right_shift(t, SEL_SHIFT), 1.0, 0.0)
        for s in range(acc.shape[1] // LANES):
            r = _rope_slot(acc[:, s * LANES:(s + 1) * LANES], cos, sin, half, 0)
            if scale != 1.0:
                r = r * scale
            if onehot:
                r = r + hot
            o_ref[:, s * LANES:(s + 1) * LANES] = r.astype(o_ref.dtype)
    return epi


def _epi_rope_b_scaled(acc, o_ref, g_ref, tab_ref):
    cos = tab_ref[:, 2 * LANES:3 * LANES]
    sin = tab_ref[:, 3 * LANES:4 * LANES]
    scale = (MLA_NOPE_DIM + MLA_ROPE_DIM) ** -0.5 * LOG2E
    for s in range(acc.shape[1] // LANES):
        r = _rope_slot(acc[:, s * LANES:(s + 1) * LANES], cos, sin, MLA_ROPE_DIM // 2, MLA_ROPE_OFF)
        o_ref[:, s * LANES:(s + 1) * LANES] = (r * scale).astype(o_ref.dtype)


def _epi_gate_kr(acc, o_ref, tab_ref):
    o_ref[:, 0:LANES] = jax.nn.sigmoid(acc[:, 0:LANES]).astype(o_ref.dtype)
    cos = tab_ref[:, 2 * LANES:3 * LANES]
    sin = tab_ref[:, 3 * LANES:4 * LANES]
    r = _rope_slot(acc[:, LANES:2 * LANES], cos, sin, MLA_ROPE_DIM // 2, MLA_ROPE_OFF)
    o_ref[:, LANES:2 * LANES] = r.astype(o_ref.dtype)


def _pro_rms(xv, g_ref, *ex):
    return _rms(xv, g_ref[...]).astype(BF16)


def _epi_add_kr(acc, o_ref, g_ref, kr_ref):
    kr = kr_ref[...].astype(F32)
    for s in range(acc.shape[1] // LANES):
        o_ref[:, s * LANES:(s + 1) * LANES] = (acc[:, s * LANES:(s + 1) * LANES] + kr).astype(o_ref.dtype)


def _compress_kernel(k16_ref, v16_ref, w1_ref, pe_ref, b1_ref, w2_ref, kc_ref, vc_ref):
    half = w1_ref.shape[1] // 2
    for which, (src, dst) in enumerate(((k16_ref, kc_ref), (v16_ref, vc_ref))):
        xb = src[...]
        a = jnp.dot(xb, w1_ref[which, 0:half, :], preferred_element_type=F32)
        b = jnp.dot(xb, w1_ref[which, half:2 * half, :], preferred_element_type=F32)
        c = jnp.dot(pe_ref[which], w1_ref[which], preferred_element_type=F32)[0:1, :] + b1_ref[which]
        rows = a.shape[0]
        pre = a + pltpu.roll(b, rows - 1, 0) + c
        hid = _gelu(pre).astype(BF16)
        dst[...] = jnp.dot(hid, w2_ref[which], preferred_element_type=F32).astype(dst.dtype)


def _compress(k16, v16, w1, pe, b1, w2p):
    bg, rows, feat = k16.shape
    blk = pl.BlockSpec((None, rows, feat), lambda i: (i, 0, 0))
    full = lambda a: pl.BlockSpec(a.shape, lambda i: (0,) * a.ndim)
    out = pl.BlockSpec((None, rows, LANES), lambda i: (i, 0, 0))
    return pl.pallas_call(
        _compress_kernel,
        grid=(bg,),
        in_specs=[blk, blk, full(w1), full(pe), full(b1), full(w2p)],
        out_specs=[out, out],
        out_shape=[jax.ShapeDtypeStruct((bg, rows, LANES), BF16)] * 2,
        compiler_params=_cparams(("arbitrary",)),
        name="nsa_compress",
    )(k16, v16, w1, pe, b1, w2p)


def _row_bcast(col, width):
    if width % LANES:
        return jnp.broadcast_to(col, (col.shape[0], width))
    tile = jnp.broadcast_to(col, (col.shape[0], LANES))
    return jnp.concatenate([tile] * (width // LANES), axis=1)


def _cmp_attn_kernel(q_ref, kc_ref, vc_ref, ovt_ref, o_ref, bias_ref, *, tq, n_sel):
    qi = pl.program_id(2)
    ncmp = kc_ref.shape[0]
    t_row = qi * tq + lax.broadcasted_iota(I32, (tq, ncmp), 0)
    n_col = lax.broadcasted_iota(I32, (tq, ncmp), 1)
    cmask = (n_col * CMP_STRIDE + (CMP_BLOCK - 1)) <= t_row
    kc = kc_ref[...]
    vc = vc_ref[...]
    imp_t = jnp.zeros((n_sel, tq), F32)
    outs = []
    for r in range(NSA_R):
        q = q_ref[:, r * LANES:(r + 1) * LANES]
        s = lax.dot_general(q, kc, (((1,), (1,)), ((), ())), preferred_element_type=F32)
        s = jnp.where(cmask, s, NEG)
        e = jnp.exp2(s - _row_bcast(jnp.max(s, axis=1, keepdims=True), ncmp))
        p = e / _row_bcast(jnp.sum(e, axis=1, keepdims=True), ncmp)
        p = jnp.where(cmask, p, 0.0).astype(BF16)
        outs.append(jnp.dot(p, vc, preferred_element_type=F32))
        imp_t = imp_t + lax.dot_general(ovt_ref[...], p, (((1,), (1,)), ((), ())),
                                        preferred_element_type=F32)
    _store_heads_compact(o_ref, outs)
    blk = lax.broadcasted_iota(I32, (n_sel, tq), 0)
    t = qi * tq + lax.broadcasted_iota(I32, (n_sel, tq), 1)
    tb = jnp.right_shift(t, SEL_SHIFT)
    forced = (blk == 0) | (blk == tb) | (blk == tb - 1)
    valid = blk * SEL_BLOCK <= t
    x = jnp.where(valid, imp_t + jnp.where(forced, FORCE_BONUS, 0.0), NEG)
    sel = jnp.zeros((n_sel, tq), F32)
    for _ in range(min(SEL_TOPK, n_sel)):
        m = jnp.max(x, axis=0, keepdims=True)
        idx = jnp.min(jnp.where(x == m, blk, n_sel), axis=0, keepdims=True)
        hit = blk == idx
        sel = jnp.where(hit, 1.0, sel)
        x = jnp.where(hit, -jnp.inf, x)
    bias_t = jnp.where(sel > 0.5, 0.0, NEG)
    parts = [jnp.zeros((SEL_BLOCK, tq), F32), bias_t]
    if LANES - SEL_BLOCK - n_sel > 0:
        parts.append(jnp.zeros((LANES - SEL_BLOCK - n_sel, tq), F32))
    full_t = jnp.concatenate(parts, axis=0)
    bias_ref[...] = full_t.T.astype(bias_ref.dtype)


def _cmp_attn(qn, kc, vc, ovt, bsz, seq, tq=256):
    n = qn.shape[0]
    nq = seq // tq
    n_sel = seq // SEL_BLOCK
    ncmp = kc.shape[1]
    qspec = pl.BlockSpec((tq, NSA_R * LANES), lambda b, g, qi: (b * nq + qi, g))
    kspec = pl.BlockSpec((None, ncmp, LANES), lambda b, g, qi: (b * NSA_GROUPS + g, 0, 0))
    return pl.pallas_call(
        functools.partial(_cmp_attn_kernel, tq=tq, n_sel=n_sel),
        grid=(bsz, NSA_GROUPS, nq),
        in_specs=[qspec, kspec, kspec, pl.BlockSpec(ovt.shape, lambda b, g, qi: (0, 0))],
        out_specs=[pl.BlockSpec((tq, NSA_R * NSA_HEAD_DIM), lambda b, g, qi: (b * nq + qi, g)),
                   pl.BlockSpec((tq, LANES), lambda b, g, qi: (b * nq + qi, g))],
        out_shape=[jax.ShapeDtypeStruct((n, NSA_HEADS * NSA_HEAD_DIM), BF16),
                   jax.ShapeDtypeStruct((n, NSA_GROUPS * LANES), BF16)],
        compiler_params=_cparams(("arbitrary",) * 3),
        name="nsa_cmp_attn",
    )(qn, kc, vc, ovt)


def _store_heads_compact(o_ref, heads):
    hd = NSA_HEAD_DIM
    for p in range(len(heads) // 2):
        pair = jnp.concatenate([heads[2 * p][:, 0:hd], heads[2 * p + 1][:, 0:hd]], axis=1)
        o_ref[:, p * LANES:(p + 1) * LANES] = pair.astype(o_ref.dtype)


def _tri_step(p, s, nq):
    first = s <= p
    return jnp.where(first, p, nq - 1 - p), jnp.where(first, s, s - p - 1)


def _flash_kernel(*refs, mode, tq, tk, nr, nh, window, nq):
    if mode == "sel":
        q_ref, k_ref, v_ref, bias_ref, o_ref, qs, m_s, l_s, acc_s = refs
    else:
        q_ref, k_ref, v_ref, o_ref, qs, m_s, l_s, acc_s = refs
    if mode == "win":
        qi = pl.program_id(2)
        kk = pl.program_id(3)
        last = kk == pl.num_programs(3) - 1
    else:
        qi, kk = _tri_step(pl.program_id(2), pl.program_id(3), nq)
        last = kk == qi
    rows = nr * tq

    @pl.when(kk == 0)
    def _init():
        for r in range(nh * nr):
            qr = q_ref[:, r * LANES:(r + 1) * LANES]
            if mode == "sel":
                qr = qr + bias_ref[:, (r // nr) * LANES:(r // nr + 1) * LANES]
            qs[r * tq:(r + 1) * tq, :] = qr
        m_s[...] = jnp.full(m_s.shape, -jnp.inf, F32)
        l_s[...] = jnp.zeros(l_s.shape, F32)
        acc_s[...] = jnp.zeros(acc_s.shape, F32)

    if mode == "win":
        kidx = qi * (tq // tk) - window // tk + kk
        needed = kidx >= 0
        full_vis = (kidx * tk + tk - 1 <= qi * tq) & (qi * tq + tq - 1 - kidx * tk < window)
    else:
        kidx = kk
        needed = True
        full_vis = kk < qi

    nt = tk // LANES

    def update(masked):
        if masked:
            rel = (lax.broadcasted_iota(I32, (rows, tk), 0) & (tq - 1)) - lax.broadcasted_iota(I32, (rows, tk), 1)
            off = kidx * tk - qi * tq
            vis = rel >= off
            if mode == "win":
                vis = vis & (rel < off + window)
        for hh in range(nh):
            rs = slice(hh * rows, (hh + 1) * rows)
            cs = slice(hh * LANES, (hh + 1) * LANES)
            s = lax.dot_general(qs[rs, :], k_ref[:, cs], (((1,), (1,)), ((), ())),
                                preferred_element_type=F32)
            if masked:
                s = jnp.where(vis, s, NEG)
            m_prev = m_s[rs, :]
            m_new = jnp.maximum(m_prev, jnp.max(s, axis=1, keepdims=True))
            p = jnp.exp2(s - jnp.concatenate([m_new] * nt, axis=1))
            alpha = jnp.exp2(m_prev - m_new)
            psum = p[:, 0:LANES]
            for c in range(1, nt):
                psum = psum + p[:, c * LANES:(c + 1) * LANES]
            l_s[rs, :] = alpha * l_s[rs, :] + psum
            acc_s[rs, :] = alpha * acc_s[rs, :] + jnp.dot(p.astype(BF16), v_ref[:, cs],
                                                          preferred_element_type=F32)
            m_s[rs, :] = m_new

    @pl.when(needed & full_vis)
    def _full():
        update(False)

    @pl.when(needed & jnp.logical_not(full_vis))
    def _edge():
        update(True)

    @pl.when(last)
    def _fin():
        out = acc_s[...] / jnp.sum(l_s[...], axis=1, keepdims=True)
        _store_heads_compact(o_ref, [out[r * tq:(r + 1) * tq, :] for r in range(nh * nr)])


def _win_kernel(*refs, t, nb, ng):
    q_ref = refs[0]
    k_refs = refs[1:1 + nb]
    v_refs = refs[1 + nb:1 + 2 * nb]
    o_ref = refs[1 + 2 * nb]
    qi = pl.program_id(1)
    rows = NSA_R * t
    rel = (lax.broadcasted_iota(I32, (rows, t), 0) & (t - 1)) - lax.broadcasted_iota(I32, (rows, t), 1)
    heads = []
    for g in range(ng):
        cs = slice(g * LANES, (g + 1) * LANES)
        qs = jnp.concatenate([q_ref[:, (g * NSA_R + r) * LANES:(g * NSA_R + r + 1) * LANES]
                              for r in range(NSA_R)], axis=0)
        ss = []
        for j in range(nb):
            s = lax.dot_general(qs, k_refs[j][:, cs], (((1,), (1,)), ((), ())), preferred_element_type=F32)
            tile_ok = qi - (nb - 1) + j >= 0
            if j == nb - 1:
                s = jnp.where(rel >= 0, s, NEG)
            elif j == 0:
                s = jnp.where((rel < 0) & tile_ok, s, NEG)
            else:
                s = jnp.where(tile_ok, s, NEG)
            ss.append(s)
        m = ss[0]
        for s in ss[1:]:
            m = jnp.maximum(m, s)
        m = _row_bcast(jnp.max(m, axis=1, keepdims=True), t)
        acc = None
        lsum = None
        for j in range(nb):
            p = jnp.exp2(ss[j] - m)
            lsum = p if lsum is None else lsum + p
            pv = jnp.dot(p.astype(BF16), v_refs[j][:, cs], preferred_element_type=F32)
            acc = pv if acc is None else acc + pv
        out = acc / jnp.sum(lsum, axis=1, keepdims=True)
        heads += [out[r * t:(r + 1) * t, :] for r in range(NSA_R)]
    _store_heads_compact(o_ref, heads)


def _win_attn(q, k, v, *, bsz, seq, t, k_off, v_off):
    assert t & (t - 1) == 0 and WINDOW % t == 0
    n = q.shape[0]
    nq = seq // t
    nb = WINDOW // t + 1
    ng = NSA_GROUPS
    qspec = pl.BlockSpec((t, ng * NSA_R * LANES), lambda b, qi: (b * nq + qi, 0))

    def kv_spec(j, off):
        return pl.BlockSpec((t, ng * LANES),
                            lambda b, qi: (b * nq + jnp.maximum(qi - (nb - 1) + j, 0), off // ng))

    ow = ng * NSA_R * NSA_HEAD_DIM
    return pl.pallas_call(
        functools.partial(_win_kernel, t=t, nb=nb, ng=ng),
        grid=(bsz, nq),
        in_specs=[qspec] + [kv_spec(j, k_off) for j in range(nb)] + [kv_spec(j, v_off) for j in range(nb)],
        out_specs=pl.BlockSpec((t, ow), lambda b, qi: (b * nq + qi, 0)),
        out_shape=jax.ShapeDtypeStruct((n, ow), BF16),
        compiler_params=_cparams(("arbitrary", "arbitrary"), 48),
        name="nsa_win_attn",
    )(q, *([k] * nb), *([v] * nb))


def _flash(q, k, v, *, mode, bsz, seq, nslots, nr, tq, tk, nh=1, k_off=0, v_off=0, bias=None,
           name):
    assert tq & (tq - 1) == 0
    n = q.shape[0]
    nq = seq // tq
    nkb = seq // tk
    if mode == "win":
        assert tq % tk == 0 and WINDOW % tk == 0
        grid = (bsz, nslots, nq, WINDOW // tk + tq // tk)

        def qrow(b, p, s):
            return b * nq + p

        def krow(b, p, s):
            return b * nkb + jnp.maximum(p * (tq // tk) - WINDOW // tk + s, 0)
    else:
        assert tq == tk
        grid = (bsz, nslots, (nq + 1) // 2, nq + 1)

        def qrow(b, p, s):
            return b * nq + _tri_step(p, s, nq)[0]

        def krow(b, p, s):
            return b * nkb + _tri_step(p, s, nq)[1]

    qspec = pl.BlockSpec((tq, nh * nr * LANES), lambda b, g, p, s: (qrow(b, p, s), g))
    in_specs = [qspec,
                pl.BlockSpec((tk, nh * LANES), lambda b, g, p, s: (krow(b, p, s), k_off // nh + g)),
                pl.BlockSpec((tk, nh * LANES), lambda b, g, p, s: (krow(b, p, s), v_off // nh + g))]
    args = [q, k, v]
    if mode == "sel":
        in_specs.append(pl.BlockSpec((tq, nh * LANES), lambda b, g, p, s: (qrow(b, p, s), g)))
        args.append(bias)
    rows = nh * nr * tq
    ow = nh * nr * NSA_HEAD_DIM
    return pl.pallas_call(
        functools.partial(_flash_kernel, mode=mode, tq=tq, tk=tk, nr=nr, nh=nh, window=WINDOW, nq=nq),
        grid=grid,
        in_specs=in_specs,
        out_specs=pl.BlockSpec((tq, ow), lambda b, g, p, s: (qrow(b, p, s), g)),
        out_shape=jax.ShapeDtypeStruct((n, nslots * ow), BF16),
        scratch_shapes=[pltpu.VMEM((rows, LANES), BF16), pltpu.VMEM((rows, LANES), F32),
                        pltpu.VMEM((rows, LANES), F32), pltpu.VMEM((rows, LANES), F32)],
        compiler_params=_cparams(("arbitrary",) * 4, 48),
        name=name,
    )(*args)


def _combine_kernel(oc_ref, os_ref, ow_ref, g_ref, e_ref, o_ref):
    g = g_ref[...]
    acc = None
    for br, src in enumerate((oc_ref, os_ref, ow_ref)):
        ge = jnp.dot(g, e_ref[br], preferred_element_type=F32)
        term = ge * src[...].astype(F32)
        acc = term if acc is None else acc + term
    o_ref[...] = acc.astype(o_ref.dtype)


def _combine(o_cmp, o_sel, o_win, gates, gates_col, tm=512):
    n, w = o_cmp.shape
    e = np.zeros((3, LANES, w), np.float32)
    for h in range(NSA_HEADS):
        for br in range(3):
            e[br, h * 3 + br, h * NSA_HEAD_DIM:(h + 1) * NSA_HEAD_DIM] = 1.0
    row = pl.BlockSpec((tm, w), lambda i: (i, 0))
    return pl.pallas_call(
        _combine_kernel,
        grid=(n // tm,),
        in_specs=[row, row, row, pl.BlockSpec((tm, LANES), lambda i: (i, gates_col)),
                  pl.BlockSpec(e.shape, lambda i: (0, 0, 0))],
        out_specs=row,
        out_shape=jax.ShapeDtypeStruct((n, w), BF16),
        compiler_params=_cparams(("arbitrary",)),
        name="nsa_combine",
    )(o_cmp, o_sel, o_win, gates, jnp.asarray(e, BF16))


def _merge_kernel(h_ref, wg0_ref, wg1_ref, a_ref, wa_ref, b_ref, wb_ref, o_ref):
    h = h_ref[...]
    g0 = jax.nn.sigmoid(jnp.dot(h, wg0_ref[...], preferred_element_type=F32))
    g1 = jax.nn.sigmoid(jnp.dot(h, wg1_ref[...], preferred_element_type=F32))
    ya = jnp.dot(a_ref[...], wa_ref[...], preferred_element_type=F32)
    yb = jnp.dot(b_ref[...], wb_ref[...], preferred_element_type=F32)
    o_ref[...] = (g0 * ya + g1 * yb).astype(o_ref.dtype)


def _merge(h, w_gm, o_nsa, w_nsa, o_mla, w_mla, tm, tn=512):
    n, k = o_nsa.shape
    kh = h.shape[1]
    d = w_nsa.shape[1]
    nj = d // tn
    row = pl.BlockSpec((tm, k), lambda i, j: (i, 0))
    wsp = pl.BlockSpec((k, tn), lambda i, j: (0, j))
    return pl.pallas_call(
        _merge_kernel,
        grid=(n // tm, nj),
        in_specs=[pl.BlockSpec((tm, kh), lambda i, j: (i, 0)),
                  pl.BlockSpec((kh, tn), lambda i, j: (0, j)),
                  pl.BlockSpec((kh, tn), lambda i, j: (0, nj + j)),
                  row, wsp, row, wsp],
        out_specs=pl.BlockSpec((tm, tn), lambda i, j: (i, j)),
        out_shape=jax.ShapeDtypeStruct((n, d), BF16),
        compiler_params=_cparams(("arbitrary", "arbitrary"), 48),
        name="branch_merge",
    )(h, w_gm, w_gm, o_nsa, w_nsa, o_mla, w_mla)


def _merge_desc(xs):
    xs = list(xs)
    n = len(xs)
    stride = n // 2
    while stride >= 1:
        for i in range(n):
            p = i ^ stride
            if p > i:
                xs[i], xs[p] = jnp.maximum(xs[i], xs[p]), jnp.minimum(xs[i], xs[p])
        stride //= 2
    return xs


def _top16_sorted(xs):
    xs = list(xs)
    n = len(xs)
    size = 2
    while size <= n:
        stride = size // 2
        while stride >= 1:
            for i in range(n):
                p = i ^ stride
                if p > i:
                    hi, lo = jnp.maximum(xs[i], xs[p]), jnp.minimum(xs[i], xs[p])
                    xs[i], xs[p] = (hi, lo) if (i & size) == 0 else (lo, hi)
            stride //= 2
        size *= 2
    shift = SUBLANES // 2
    while shift >= 1:
        ys = [pltpu.roll(x, shift, 0) for x in xs]
        xs = _merge_desc([jnp.maximum(xs[k], ys[n - 1 - k]) for k in range(n)])
        shift //= 2
    return xs


def _prefix_len(pred, vs):
    p8 = pred(vs[7])
    p4 = pred(jnp.where(p8, vs[11], vs[3]))
    p2 = pred(jnp.where(p8, jnp.where(p4, vs[13], vs[9]), jnp.where(p4, vs[5], vs[1])))
    q = [jnp.where(p2, vs[4 * i + 2], vs[4 * i]) for i in range(4)]
    p1 = pred(jnp.where(p8, jnp.where(p4, q[3], q[2]), jnp.where(p4, q[1], q[0])))
    n = (jnp.where(p8, 8.0, 0.0) + jnp.where(p4, 4.0, 0.0)) + (jnp.where(p2, 2.0, 0.0) + jnp.where(p1, 1.0, 0.0))
    return jnp.where(pred(vs[15]), 16.0, n)


def _rows8(x):
    return [x[k * SUBLANES:(k + 1) * SUBLANES, :] for k in range(x.shape[0] // SUBLANES)]


def _peer_route_kernel(q_ref, sk_ref, tb_ref, tf_ref):
    nk = PEER_N_KEYS
    sec = PEER_HEADS * nk
    tr = q_ref.shape[0]
    sub = lax.broadcasted_iota(I32, (SUBLANES, tr), 0)
    ninf = jnp.full((SUBLANES, tr), -jnp.inf, F32)

    def spread(vs):
        out = vs[SUBLANES - 1]
        for j in range(SUBLANES - 2, -1, -1):
            out = jnp.where(sub == j, vs[j], out)
        return out

    for h in range(PEER_HEADS):
        s = []
        for p in range(2):
            c = (2 * h + p) * nk
            s.append(lax.dot_general(sk_ref[2 * h + p], q_ref[:, c:c + nk], (((1,), (1,)), ((), ())),
                                     preferred_element_type=F32))
        x1, x2 = _rows8(s[0]), _rows8(s[1])
        v1, v2 = _top16_sorted(x1), _top16_sorted(x2)
        v2lo, v2hi = spread(v2[:SUBLANES]), spread(v2[SUBLANES:])
        cands = [v1[0] + v2lo, v1[0] + v2hi]
        for i in range(1, SUBLANES):
            c = v1[i] + v2lo
            cnt = PEER_TOPK // (i + 1)
            cands.append(jnp.where(sub < cnt, c, -jnp.inf) if cnt < SUBLANES else c)
        cands.append(spread(v1[SUBLANES:]) + v2[0])
        top = _top16_sorted(cands + [ninf] * (PEER_TOPK - len(cands)))
        tau = top[PEER_TOPK - 1]
        z = None
        for k in range(PEER_TOPK):
            ek = jnp.exp(top[k] - top[0])
            z = ek if z is None else z + ek
        rank2 = jnp.concatenate([_prefix_len(lambda v, x=x: v > x, v2) for x in x2], axis=0)
        count = jnp.concatenate([_prefix_len(lambda v, x=x: (x + v) >= tau, v2) for x in x1], axis=0)
        tb_ref[h * nk:(h + 1) * nk, :] = rank2.astype(tb_ref.dtype)
        tb_ref[sec + h * nk:sec + (h + 1) * nk, :] = jnp.exp(s[1] - v2[0][0:1, :]).astype(tb_ref.dtype)
        apb = tf_ref.shape[1] // (2 * PEER_HEADS)
        for k, vals in enumerate((count, jnp.exp(s[0] - v1[0][0:1, :]) / z[0:1, :])):
            r0 = (k * PEER_HEADS + h) * apb
            tf_ref[:, r0:r0 + apb, :] = vals.reshape(nk // apb, apb, tr)


def _peer_route(qry, sk, apb, tr=256):
    n, w = qry.shape
    nj = PEER_N_KEYS // apb
    rows = 2 * PEER_HEADS * apb
    return pl.pallas_call(
        _peer_route_kernel,
        grid=(n // tr,),
        in_specs=[pl.BlockSpec((tr, w), lambda i: (i, 0)),
                  pl.BlockSpec(sk.shape, lambda i: (0, 0, 0))],
        out_specs=[pl.BlockSpec((w, tr), lambda i: (0, i)),
                   pl.BlockSpec((nj, rows, tr), lambda i: (0, 0, i))],
        out_shape=[jax.ShapeDtypeStruct((w, n), BF16), jax.ShapeDtypeStruct((nj, rows, n), F32)],
        compiler_params=_cparams(("arbitrary",), 40),
        name="peer_route",
    )(qry, sk)


PEER_CHUNK = 32
PEER_TILE = 1024


def _peer_dense_kernel(h_ref, u_ref, vt_ref, tb_ref, tf_ref, x_ref, pg_ref, gf_ref, o_ref, acc_s, pt_s,
                       *, te):
    j = pl.program_id(1)
    nk = PEER_N_KEYS
    sec = PEER_HEADS * nk
    apb = te // nk
    gdt = pt_s.dtype

    @pl.when(j == 0)
    def _init():
        acc_s[...] = jnp.zeros(acc_s.shape, F32)

    def row_tile(row):
        r = tf_ref[row:row + 1, :]
        return jnp.broadcast_to(r.astype(gdt), (PEER_CHUNK, r.shape[1]))

    zt = lax.dot_general(u_ref[...], h_ref[...], (((1,), (1,)), ((), ())), preferred_element_type=F32)
    for al in range(apb):
        cnt = [row_tile(h * apb + al) for h in range(PEER_HEADS)]
        e1 = [row_tile((PEER_HEADS + h) * apb + al) for h in range(PEER_HEADS)]
        for c in range(nk // PEER_CHUNK):
            lo = c * PEER_CHUNK
            g = None
            for h in range(PEER_HEADS):
                rank2 = tb_ref[h * nk + lo:h * nk + lo + PEER_CHUNK, :]
                e2 = tb_ref[sec + h * nk + lo:sec + h * nk + lo + PEER_CHUNK, :]
                term = jnp.where(rank2 < cnt[h], e1[h] * e2, jnp.zeros((), gdt))
                g = term if g is None else g + term
            r0 = al * nk + lo
            pt_s[r0:r0 + PEER_CHUNK, :] = g * _gelu(zt[r0:r0 + PEER_CHUNK, :]).astype(gdt)
    acc_s[...] += jnp.dot(vt_ref[...], pt_s[...], preferred_element_type=F32)

    @pl.when(j == pl.num_programs(1) - 1)
    def _fin():
        o_ref[...] = x_ref[...] + gf_ref[0] * _rms(acc_s[...].T, pg_ref[...])


def _peer_dense(h2, u_bf, vt_bf, tb, tf, x1, post_g, gate, seq, te, tm=512):
    n, d = h2.shape
    ne = u_bf.shape[0]
    w = tb.shape[0]
    bsz = gate.shape[0]
    row = pl.BlockSpec((tm, d), lambda i, j: (i, 0))
    return pl.pallas_call(
        functools.partial(_peer_dense_kernel, te=te),
        grid=(n // tm, ne // te),
        in_specs=[row,
                  pl.BlockSpec((te, d), lambda i, j: (j, 0)),
                  pl.BlockSpec((d, te), lambda i, j: (0, j)),
                  pl.BlockSpec((w, tm), lambda i, j: (0, i)),
                  pl.BlockSpec((None, tf.shape[1], tm), lambda i, j: (j, 0, i)),
                  row,
                  pl.BlockSpec((1, d), lambda i, j: (0, 0)),
                  pl.BlockSpec((1, 1, d), lambda i, j: ((i * tm) // seq, 0, 0))],
        out_specs=row,
        out_shape=jax.ShapeDtypeStruct((n, d), F32),
        scratch_shapes=[pltpu.VMEM((d, tm), F32), pltpu.VMEM((te, tm), BF16)],
        compiler_params=_cparams(("arbitrary", "arbitrary"), 56),
        name="peer_dense",
    )(h2, u_bf, vt_bf, tb, tf, x1, post_g.reshape(1, d), gate.reshape(bsz, 1, d))


def _slots(w, n, width, off=0):
    k = w.shape[0]
    w = w.reshape(k, n, width)
    w = jnp.pad(w, ((0, 0), (0, 0), (off, LANES - width - off)))
    return w.reshape(k, n * LANES)


def _overlap_t(seq):
    n_cmp_rows = seq // CMP_STRIDE
    n_sel = seq // SEL_BLOCK
    cs = np.arange(n_cmp_rows) * CMP_STRIDE
    ce = cs + CMP_BLOCK - 1
    ss = np.arange(n_sel) * SEL_BLOCK
    ov = (cs[None, :] < ss[:, None] + SEL_BLOCK) & (ce[None, :] >= ss[:, None])
    ov[:, n_cmp_rows - 1] = False
    return jnp.asarray(ov.astype(np.float32), BF16)


def _token_mixer(h, positions, bsz, seq, w_in, cmp_pe, cmp_w1, cmp_b1, cmp_w2, mla_q_norm_g,
                 mla_w_q_up, mla_kv_norm_g, mla_w_kv_up, w_branch_nsa, w_branch_mla):
    d = h.shape[1]
    g_, hd = NSA_GROUPS, NSA_HEAD_DIM
    tab = _rope_tables(positions)
    tm = min(PROJ_TM, seq)
    tab_spec = pl.BlockSpec((tm, 4 * LANES), lambda i, j: (i, 0))

    wb = w_in.astype(BF16)
    o1 = NSA_HEADS * hd
    kv = wb[:, o1:o1 + 3 * 2 * g_ * hd].reshape(d, 3, 2, g_ * hd)
    o2 = o1 + 3 * 2 * g_ * hd
    o3 = o2 + 3 * NSA_HEADS
    mq = mla_w_q_up.shape[0]
    mkv = mla_w_kv_up.shape[0]
    o4 = o3 + mq
    o5 = o4 + mkv
    o6 = o5 + MLA_ROPE_DIM
    w_q = _slots(wb[:, :o1], NSA_HEADS, hd)
    w_kk = jnp.concatenate([_slots(kv[:, 0, 0], g_, hd), _slots(kv[:, 2, 0], g_, hd)], axis=1)
    w_ks = _slots(kv[:, 1, 0], g_, hd)
    w_vv = jnp.concatenate([_slots(kv[:, 1, 1], g_, hd), _slots(kv[:, 2, 1], g_, hd),
                            kv[:, 0, 1], jnp.zeros((d, g_ * hd), BF16)], axis=1)
    w_c = wb[:, o3:o5]
    w_gk = jnp.concatenate([jnp.pad(wb[:, o2:o3], ((0, 0), (0, LANES - 3 * NSA_HEADS))),
                            _slots(wb[:, o5:o6], 1, MLA_ROPE_DIM, MLA_ROPE_OFF)], axis=1)
    w_gm = wb[:, o6:]

    qn = _mm(h, w_q, _make_epi_rope_a(hd ** -0.5 * LOG2E), tm=tm, tn=512, out_dtype=BF16, name="proj_q",
             extras=[(tab, tab_spec)])
    kk = _mm(h, w_kk, _make_epi_rope_a(1.0), tm=tm, tn=512, out_dtype=BF16, name="proj_k",
             extras=[(tab, tab_spec)])
    ks = _mm(h, w_ks, _make_epi_rope_a(1.0, seq, tm, True), tm=tm, tn=512, out_dtype=BF16,
             name="proj_ksel", extras=[(tab, tab_spec)])
    vv = _mm(h, w_vv, _epi_plain, tm=tm, tn=512, out_dtype=BF16, name="proj_v")
    cqkv = _mm(h, w_c, _epi_plain, tm=tm, tn=mq + mkv, out_dtype=F32, name="proj_c")
    gk = _mm(h, w_gk, _epi_gate_kr, tm=tm, tn=2 * LANES, out_dtype=BF16, name="proj_gate_kr",
             extras=[(tab, tab_spec)])

    def to16(a, width):
        a = a.reshape(bsz, seq, g_, width)[..., :hd]
        a = jnp.transpose(a, (0, 2, 1, 3))
        return a.reshape(bsz * g_, seq // CMP_STRIDE, CMP_STRIDE * hd)

    k16 = to16(kk[:, :g_ * LANES], LANES)
    v16 = to16(vv[:, 2 * g_ * LANES:2 * g_ * LANES + g_ * hd], hd)
    pe = jnp.pad(cmp_pe.reshape(2, 1, CMP_BLOCK * hd), ((0, 0), (0, SUBLANES - 1), (0, 0))).astype(BF16)
    w2p = jnp.pad(cmp_w2, ((0, 0), (0, 0), (0, LANES - hd))).astype(BF16)
    kc, vc = _compress(k16, v16, cmp_w1.astype(BF16), pe, cmp_b1.reshape(2, 1, CMP_HIDDEN), w2p)
    o_cmp, bias = _cmp_attn(qn, kc, vc, _overlap_t(seq), bsz, seq, tq=min(1024, seq))
    o_sel = _flash(qn, ks, vv, mode="sel", bsz=bsz, seq=seq, nslots=1, nr=NSA_R, nh=g_,
                   tq=512, tk=512, bias=bias, name="nsa_sel_attn")
    o_win = _win_attn(qn, kk, vv, bsz=bsz, seq=seq, t=256, k_off=g_, v_off=g_)
    o_nsa = _combine(o_cmp, o_sel, o_win, gk, 0)

    qd = MLA_NOPE_DIM + MLA_ROPE_DIM
    w_qup = _slots(mla_w_q_up.astype(BF16), MLA_HEADS, qd)
    kvu = mla_w_kv_up.astype(BF16).reshape(mkv, MLA_HEADS, MLA_NOPE_DIM + MLA_V_DIM)
    w_kup = _slots(kvu[:, :, :MLA_NOPE_DIM].reshape(mkv, -1), MLA_HEADS, MLA_NOPE_DIM)
    w_vup = _slots(kvu[:, :, MLA_NOPE_DIM:].reshape(mkv, -1), MLA_HEADS, MLA_V_DIM)
    gq_spec = pl.BlockSpec((1, mq), lambda i, j: (0, 0))
    gkv_spec = pl.BlockSpec((1, mkv), lambda i, j: (0, 0))
    ckv_spec = pl.BlockSpec((tm, mkv), lambda i, j: (i, mq // mkv))
    q_mla = _mm(cqkv, w_qup, _epi_rope_b_scaled, tm=tm, tn=512,
                out_dtype=BF16, name="mla_q_up", pro=_pro_rms,
                x_spec=pl.BlockSpec((tm, mq), lambda i, j: (i, 0)),
                extras=[(mla_q_norm_g.reshape(1, mq), gq_spec), (tab, tab_spec)])
    k_mla = _mm(cqkv, w_kup, _epi_add_kr, tm=tm, tn=512, out_dtype=BF16, name="mla_k_up",
                pro=_pro_rms, x_spec=ckv_spec,
                extras=[(mla_kv_norm_g.reshape(1, mkv), gkv_spec),
                        (gk, pl.BlockSpec((tm, LANES), lambda i, j: (i, 1)))])
    v_mla = _mm(cqkv, w_vup, _epi_plain, tm=tm, tn=512, out_dtype=BF16, name="mla_v_up",
                pro=_pro_rms, x_spec=ckv_spec,
                extras=[(mla_kv_norm_g.reshape(1, mkv), gkv_spec)])
    o_mla = _flash(q_mla, k_mla, v_mla, mode="causal", bsz=bsz, seq=seq, nslots=1, nr=1,
                   nh=MLA_HEADS, tq=512, tk=512, name="mla_attn")

    return _merge(h, w_gm, o_nsa, w_branch_nsa.astype(BF16), o_mla, w_branch_mla.astype(BF16), tm)


def _peer(h2, x1, post_g, gate, seq, peer_w_q, peer_sub_keys, peer_u, peer_v):
    qry = _mm(h2, peer_w_q.astype(BF16), _epi_plain, tm=min(PROJ_TM, h2.shape[0]), tn=512,
              out_dtype=BF16, name="peer_q")
    sk = peer_sub_keys.astype(BF16).reshape(2 * PEER_HEADS, PEER_N_KEYS, -1)
    tb, tf = _peer_route(qry, sk, PEER_TILE // PEER_N_KEYS)
    return _peer_dense(h2, peer_u.astype(BF16), jnp.transpose(peer_v).astype(BF16), tb, tf,
                       x1, post_g, gate, seq, PEER_TILE)


def _block(x, c, positions, ada_w, ada_b, attn_pre_g, attn_post_g, w_in, cmp_pe, cmp_w1, cmp_b1,
           cmp_w2, mla_q_norm_g, mla_w_q_up, mla_kv_norm_g, mla_w_kv_up, w_branch_nsa, w_branch_mla,
           w_out, ffn_pre_g, ffn_post_g, peer_w_q, peer_sub_keys, peer_u, peer_v):
    bsz, seq, d = x.shape
    x2 = x.reshape(bsz * seq, d)
    mod = _adaln(c, ada_w, ada_b)
    sh_a, sc_a, g_a, sh_f, sc_f, g_f = [mod[:, i * d:(i + 1) * d] for i in range(6)]

    h = _normmod(x2, attn_pre_g, sc_a, sh_a, seq)
    merged = _token_mixer(h, positions, bsz, seq, w_in, cmp_pe, cmp_w1, cmp_b1, cmp_w2, mla_q_norm_g,
                          mla_w_q_up, mla_kv_norm_g, mla_w_kv_up, w_branch_nsa, w_branch_mla)
    x1, h2 = _out_proj(merged, w_out.astype(BF16), x2, attn_post_g, g_a, ffn_pre_g, sc_f, sh_f, seq)

    out = _peer(h2, x1, ffn_post_g, g_f, seq, peer_w_q, peer_sub_keys, peer_u, peer_v)
    return out.reshape(bsz, seq, d)


def kernel(x, c, positions, ada_w, ada_b, attn_pre_g, attn_post_g, w_in, cmp_pe, cmp_w1, cmp_b1,
           cmp_w2, mla_q_norm_g, mla_w_q_up, mla_kv_norm_g, mla_w_kv_up, w_branch_nsa, w_branch_mla,
           w_out, ffn_pre_g, ffn_post_g, peer_w_q, peer_sub_keys, peer_u, peer_v):
    depth = ada_w.shape[0]
    for l in range(depth):
        x = _block(x, c, positions, ada_w[l], ada_b[l], attn_pre_g[l], attn_post_g[l], w_in[l],
                   cmp_pe[l], cmp_w1[l], cmp_b1[l], cmp_w2[l], mla_q_norm_g[l], mla_w_q_up[l],
                   mla_kv_norm_g[l], mla_w_kv_up[l], w_branch_nsa[l], w_branch_mla[l], w_out[l],
                   ffn_pre_g[l], ffn_post_g[l], peer_w_q[l], peer_sub_keys[l], peer_u[l], peer_v[l])
    return x
```
